```python
import math
import jax
import jax.numpy as jnp
from jax import lax
import numpy as np

D_MODEL = 2048
BATCH = 1
SEQ = 16384
DEPTH = 2

GRID_W = 64
HEAD_DIM = 128
EPS = 1e-6
NEG_INF = -1e30

A_HEADS = 8
A_KV_HEADS = 2
A_WIDTH = A_HEADS * HEAD_DIM
A_KV_WIDTH = A_KV_HEADS * HEAD_DIM
ROPE_THETA = 10000.0
Q_BLOCK = 128
S5_WIDTH = D_MODEL // 2
S5_GROUP = 16
S5_GROUPS = S5_WIDTH // S5_GROUP
S5_STATE = 64
S5_DT_MIN = 0.001
S5_DT_MAX = 0.1
C_HEADS = 8
C_WIDTH = C_HEADS * HEAD_DIM
NA_ROWS = 8
NA_COLS = 16
SSD_WIDTH = D_MODEL // 2
SSD_HEAD_DIM = 64
SSD_HEADS = SSD_WIDTH // SSD_HEAD_DIM
SSD_GROUPS = 2
SSD_STATE = 128
SSD_CONV = 5
SSD_CHUNK = 128
SSD_CONV_CH = SSD_WIDTH + 2 * SSD_GROUPS * SSD_STATE

IN_EVEN = 2 * A_WIDTH + 2 * A_KV_WIDTH + 2 * S5_WIDTH
IN_ODD = 4 * C_WIDTH + SSD_WIDTH + SSD_CONV_CH + 2 * SSD_HEADS
N_EVEN = (DEPTH + 1) // 2
N_ODD = DEPTH // 2

kernel_name = 'hybrid_gqa_s5_natten_ssd_encoder'


def split_cols(t, sizes):
    outs, start = [], 0
    for s in sizes:
        outs.append(t[..., start:start + s])
        start += s
    return outs


def rms_norm(x, g):
    xf = x.astype(jnp.float32)
    y = xf * lax.rsqrt(jnp.mean(xf * xf, axis=-1, keepdims=True) + EPS)
    return (y * g.astype(jnp.float32)).astype(x.dtype)


def ada_modulate(x, c, norm_g, ada_w, ada_b):
    mod = jax.nn.silu(c) @ ada_w + ada_b
    shift, scale, gate = jnp.split(mod, 3, axis=-1)
    h = rms_norm(x, norm_g) * (1.0 + scale[:, None]) + shift[:, None]
    return h, gate[:, None]


def axial_rope_tables(seq_len):
    t = jnp.arange(seq_len)
    row = (t // GRID_W).astype(jnp.float32)
    col = (t % GRID_W).astype(jnp.float32)
    n_axis = HEAD_DIM // 4
    inv = ROPE_THETA ** (-jnp.arange(n_axis, dtype=jnp.float32) / n_axis)
    ang = jnp.concatenate([row[:, None] * inv, col[:, None] * inv], axis=-1)
    return jnp.cos(ang), jnp.sin(ang)


def apply_rope(x, cos, sin):
    xp = x.astype(jnp.float32).reshape(x.shape[:-1] + (HEAD_DIM // 2, 2))
    x1, x2 = xp[..., 0], xp[..., 1]
    cs = cos[None, :, None, :]
    sn = sin[None, :, None, :]
    out = jnp.stack([x1 * cs - x2 * sn, x1 * sn + x2 * cs], axis=-1)
    return out.reshape(x.shape).astype(x.dtype)


def gqa_block_attention(q, k, v):
    bsz, seq, hq, dh = q.shape
    hkv = k.shape[2]
    grp = hq // hkv
    nb = seq // Q_BLOCK
    qb = q.reshape(bsz, nb, Q_BLOCK, hkv, grp, dh).transpose(1, 0, 2, 3, 4, 5)
    scale = dh ** -0.5

    def block(qi):
        s = jnp.einsum('bqkgd,bskd->bkgqs', qi, k).astype(jnp.float32) * scale
        p = jax.nn.softmax(s, axis=-1).astype(v.dtype)
        return jnp.einsum('bkgqs,bskd->bqkgd', p, v)

    o = lax.map(block, qb)
    return o.transpose(1, 0, 2, 3, 4, 5).reshape(bsz, seq, hq * dh)


def _complex_affine_combine(e1, e2):
    a1r, a1i, x1r, x1i = e1
    a2r, a2i, x2r, x2i = e2
    return (a1r * a2r - a1i * a2i,
            a1r * a2i + a1i * a2r,
            a2r * x1r - a2i * x1i + x2r,
            a2r * x1i + a2i * x1r + x2i)


def s5_scan_direction(u, lam_re, lam_im, log_step, b_re, b_im, c_re, c_im, reverse):
    f32 = jnp.float32
    lr = lam_re.astype(f32)
    li = lam_im.astype(f32)
    dt = jnp.exp(log_step.astype(f32))[:, None]
    mag = jnp.exp(lr * dt)
    ab_re = mag * jnp.cos(li * dt)
    ab_im = mag * jnp.sin(li * dt)
    den = lr * lr + li * li
    num_re = ab_re - 1.0
    f_re = (num_re * lr + ab_im * li) / den
    f_im = (ab_im * lr - num_re * li) / den
    br = b_re.astype(f32)
    bi = b_im.astype(f32)
    bb_re = f_re[..., None] * br - f_im[..., None] * bi
    bb_im = f_re[..., None] * bi + f_im[..., None] * br
    x_re = jnp.einsum('gph,blgh->blgp', bb_re, u)
    x_im = jnp.einsum('gph,blgh->blgp', bb_im, u)
    a_re = jnp.broadcast_to(ab_re, x_re.shape)
    a_im = jnp.broadcast_to(ab_im, x_im.shape)
    _, _, h_re, h_im = lax.associative_scan(
        _complex_affine_combine, (a_re, a_im, x_re, x_im), reverse=reverse, axis=1)
    return (jnp.einsum('ghp,blgp->blgh', c_re.astype(f32), h_re)
            - jnp.einsum('ghp,blgp->blgh', c_im.astype(f32), h_im))


def s5_mixer(u, lam_re, lam_im, log_step, b_re, b_im, c_re, c_im, s5_d, w_glu, b_glu):
    bsz, seq, _ = u.shape
    ug = u.astype(jnp.float32).reshape(bsz, seq, S5_GROUPS, S5_GROUP)
    y = s5_d.astype(jnp.float32).reshape(S5_GROUPS, S5_GROUP) * ug
    for direction in range(2):
        y = y + s5_scan_direction(ug, lam_re[direction], lam_im[direction], log_step[direction],
                                  b_re[direction], b_im[direction], c_re[direction],
                                  c_im[direction], reverse=(direction == 1))
    y = jax.nn.gelu(y.reshape(bsz, seq, S5_WIDTH)).astype(u.dtype)
    val, gt = jnp.split(y @ w_glu + b_glu, 2, axis=-1)
    return val * jax.nn.sigmoid(gt)


def neighbourhood_attention(q, k, v, rpb):
    bsz, seq, heads, dh = q.shape
    rows = seq // GRID_W
    kr = min(NA_ROWS, rows)
    qg = q.reshape(bsz, rows, GRID_W, heads, dh)
    kg = k.reshape(bsz, rows, GRID_W, heads, dh)
    vg = v.reshape(bsz, rows, GRID_W, heads, dh)
    col = jnp.arange(GRID_W)
    col_start = jnp.clip(col - NA_COLS // 2, 0, GRID_W - NA_COLS)
    col_mask = (col[None, :] >= col_start[:, None]) & (col[None, :] < col_start[:, None] + NA_COLS)
    dc = jnp.clip(col[None, :] - col[:, None], -(NA_COLS - 1), NA_COLS - 1) + NA_COLS - 1
    rpb_c = rpb[:, :, dc].astype(jnp.float32)
    scale = dh ** -0.5

    def row_block(r):
        rs = jnp.clip(r - NA_ROWS // 2, 0, rows - kr)
        kb = lax.dynamic_slice_in_dim(kg, rs, kr, axis=1)
        vb = lax.dynamic_slice_in_dim(vg, rs, kr, axis=1)
        qr = lax.dynamic_index_in_dim(qg, r, axis=1, keepdims=False)
        s = jnp.einsum('bqhd,brkhd->bhqrk', qr, kb).astype(jnp.float32) * scale
        dr = rs + jnp.arange(kr) - r + NA_ROWS - 1
        bias = jnp.take(rpb_c, dr, axis=1).transpose(0, 2, 1, 3)
        s = jnp.where(col_mask[None, None, :, None, :], s + bias[None], NEG_INF)
        p = jax.nn.softmax(s.reshape(bsz, heads, GRID_W, kr * GRID_W), axis=-1)
        p = p.reshape(bsz, heads, GRID_W, kr, GRID_W).astype(v.dtype)
        return jnp.einsum('bhqrk,brkhd->bqhd', p, vb)

    o = lax.map(row_block, jnp.arange(rows))
    return o.transpose(1, 0, 2, 3, 4).reshape(bsz, seq, heads * dh)


def depthwise_conv_centred(x, w, bias):
    ch = x.shape[-1]
    y = lax.conv_general_dilated(x, w[:, None, :].astype(x.dtype), window_strides=(1,),
                                 padding=[(SSD_CONV // 2, SSD_CONV // 2)],
                                 dimension_numbers=('NWC', 'WIO', 'NWC'),
                                 feature_group_count=ch)
    return y + bias


def segsum(a):
    t = a.shape[-1]
    cs = jnp.cumsum(a, axis=-1)
    diff = cs[..., :, None] - cs[..., None, :]
    return jnp.where(jnp.tril(jnp.ones((t, t), dtype=bool)), diff, -jnp.inf)


def ssd_scan(x, dt, a, bm, cm):
    bsz, seq, heads, hp = x.shape
    n = bm.shape[-1]
    nc = seq // SSD_CHUNK
    xd = (x * dt[..., None]).reshape(bsz, nc, SSD_CHUNK, heads, hp)
    adt = (dt * a).reshape(bsz, nc, SSD_CHUNK, heads).transpose(0, 3, 1, 2)
    bc = bm.reshape(bsz, nc, SSD_CHUNK, heads, n)
    cc = cm.reshape(bsz, nc, SSD_CHUNK, heads, n)
    a_cum = jnp.cumsum(adt, axis=-1)
    scores = jnp.einsum('bclhn,bcshn->bhcls', cc, bc) * jnp.exp(segsum(adt))
    y_diag = jnp.einsum('bhcls,bcshp->bclhp', scores, xd)
    decay_states = jnp.exp(a_cum[..., -1:] - a_cum).transpose(0, 2, 3, 1)
    states = jnp.einsum('bclhn,bclhp->bchpn', bc * decay_states[..., None], xd)
    chunk_tot = jnp.pad(a_cum[..., -1], ((0, 0), (0, 0), (1, 0)))
    decay_chunk = jnp.exp(segsum(chunk_tot))
    states = jnp.concatenate([jnp.zeros_like(states[:, :1]), states], axis=1)
    states = jnp.einsum('bhzc,bchpn->bzhpn', decay_chunk, states)[:, :-1]
    out_decay = jnp.exp(a_cum).transpose(0, 2, 3, 1)
    y_off = jnp.einsum('bclhn,bchpn->bclhp', cc, states) * out_decay[..., None]
    return (y_diag + y_off).reshape(bsz, seq, heads, hp)


def ssd_mixer(z, xbc, dt_raw, conv_w, conv_b, dt_bias, a_log, ssd_d, norm_w):
    f32 = jnp.float32
    bsz, seq, _ = z.shape
    xbc = jax.nn.silu(depthwise_conv_centred(xbc, conv_w, conv_b)).astype(f32)
    xs, bm, cm = split_cols(xbc, (SSD_WIDTH, SSD_GROUPS * SSD_STATE, SSD_GROUPS * SSD_STATE))
    xs = xs.reshape(bsz, seq, SSD_HEADS, SSD_HEAD_DIM)
    rep = SSD_HEADS // SSD_GROUPS
    bm = jnp.repeat(bm.reshape(bsz, seq, SSD_GROUPS, SSD_STATE), rep, axis=2)
    cm = jnp.repeat(cm.reshape(bsz, seq, SSD_GROUPS, SSD_STATE), rep, axis=2)
    dt = jax.nn.softplus(dt_raw.astype(f32).reshape(bsz, seq, 2, SSD_HEADS) + dt_bias.astype(f32))
    a = -jnp.exp(a_log.astype(f32))
    y_fwd = ssd_scan(xs, dt[:, :, 0], a[0], bm, cm)
    flip = lambda t: jnp.flip(t, axis=1)
    y_bwd = flip(ssd_scan(flip(xs), flip(dt[:, :, 1]), a[1], flip(bm), flip(cm)))
    y = y_fwd + y_bwd + ssd_d.astype(f32)[:, None] * xs
    y = y.reshape(bsz, seq, SSD_WIDTH) * jax.nn.silu(z.astype(f32))
    return rms_norm(y, norm_w).astype(z.dtype)


def layer_attn_s5(x, c, norm_g, ada_w, ada_b, w_in, q_norm, k_norm, lam_re, lam_im, log_step,
                  b_re, b_im, c_re, c_im, s5_d, w_glu, b_glu, w_out):
    bsz, seq, _ = x.shape
    h, gate = ada_modulate(x, c, norm_g, ada_w, ada_b)
    q, k, v, g_a, u, g_b = split_cols(
        h @ w_in, (A_WIDTH, A_KV_WIDTH, A_KV_WIDTH, A_WIDTH, S5_WIDTH, S5_WIDTH))
    q = rms_norm(q.reshape(bsz, seq, A_HEADS, HEAD_DIM), q_norm)
    k = rms_norm(k.reshape(bsz, seq, A_KV_HEADS, HEAD_DIM), k_norm)
    cos, sin = axial_rope_tables(seq)
    q = apply_rope(q, cos, sin)
    k = apply_rope(k, cos, sin)
    o_a = gqa_block_attention(q, k, v.reshape(bsz, seq, A_KV_HEADS, HEAD_DIM)) * jax.nn.silu(g_a)
    o_b = s5_mixer(u, lam_re, lam_im, log_step, b_re, b_im, c_re, c_im, s5_d, w_glu, b_glu) * jax.nn.silu(g_b)
    out = jnp.concatenate([o_a, o_b], axis=-1) @ w_out
    return x + gate * out


def layer_na_ssd(x, c, norm_g, ada_w, ada_b, w_in, q_norm, k_norm, rpb, conv_w, conv_b,
                 dt_bias, a_log, ssd_d, norm_w, w_out):
    bsz, seq, _ = x.shape
    h, gate = ada_modulate(x, c, norm_g, ada_w, ada_b)
    q, k, v, g_c, z, xbc, dt_raw = split_cols(
        h @ w_in, (C_WIDTH, C_WIDTH, C_WIDTH, C_WIDTH, SSD_WIDTH, SSD_CONV_CH, 2 * SSD_HEADS))
    q = rms_norm(q.reshape(bsz, seq, C_HEADS, HEAD_DIM), q_norm)
    k = rms_norm(k.reshape(bsz, seq, C_HEADS, HEAD_DIM), k_norm)
    o_c = neighbourhood_attention(q, k, v.reshape(bsz, seq, C_HEADS, HEAD_DIM), rpb) * jax.nn.silu(g_c)
    o_d = ssd_mixer(z, xbc, dt_raw, conv_w, conv_b, dt_bias, a_log, ssd_d, norm_w)
    out = jnp.concatenate([o_c, o_d], axis=-1) @ w_out
    return x + gate * out


def setup_inputs(seed: int = 0) -> dict:
    key = jax.random.key(seed)
    ks = jax.random.split(key, 40)
    f32 = jnp.float32
    D = D_MODEL
    NE, NO = N_EVEN, N_ODD
    G, P, H = S5_GROUPS, S5_STATE, S5_GROUP

    def nrm(i, shape, s):
        return jax.random.normal(ks[i], shape, f32) * s

    n_idx = jnp.arange(P, dtype=f32)
    dt0 = jnp.exp(jax.random.uniform(ks[28], (NO, 2, SSD_HEADS), f32,
                                     minval=math.log(1e-3), maxval=math.log(1e-1)))
    return {
        'x': nrm(0, (BATCH, SEQ, D), 1.0),
        'c': nrm(1, (BATCH, D), 1.0),
        'e_norm_g': 1.0 + nrm(2, (NE, D), 0.02),
        'e_ada_w': nrm(3, (NE, D, 3 * D), D ** -0.5),
        'e_ada_b': nrm(4, (NE, 3 * D), 0.01),
        'e_w_in': nrm(5, (NE, D, IN_EVEN), D ** -0.5),
        'e_q_norm': 1.0 + nrm(6, (NE, HEAD_DIM), 0.02),
        'e_k_norm': 1.0 + nrm(7, (NE, HEAD_DIM), 0.02),
        's5_lam_re': -0.5 + nrm(8, (NE, 2, G, P), 0.01),
        's5_lam_im': math.pi * n_idx + nrm(9, (NE, 2, G, P), 0.01),
        's5_log_step': jax.random.uniform(ks[10], (NE, 2, G), f32,
                                          minval=math.log(S5_DT_MIN), maxval=math.log(S5_DT_MAX)),
        's5_b_re': nrm(11, (NE, 2, G, P, H), (2 * H) ** -0.5),
        's5_b_im': nrm(12, (NE, 2, G, P, H), (2 * H) ** -0.5),
        's5_c_re': nrm(13, (NE, 2, G, H, P), P ** -0.5),
        's5_c_im': nrm(14, (NE, 2, G, H, P), P ** -0.5),
        's5_d': nrm(15, (NE, S5_WIDTH), 1.0),
        's5_w_glu': nrm(16, (NE, S5_WIDTH, 2 * S5_WIDTH), S5_WIDTH ** -0.5),
        's5_b_glu': nrm(17, (NE, 2 * S5_WIDTH), 0.01),
        'e_w_out': nrm(18, (NE, A_WIDTH + S5_WIDTH, D), (A_WIDTH + S5_WIDTH) ** -0.5),
        'o_norm_g': 1.0 + nrm(19, (NO, D), 0.02),
        'o_ada_w': nrm(20, (NO, D, 3 * D), D ** -0.5),
        'o_ada_b': nrm(21, (NO, 3 * D), 0.01),
        'o_w_in': nrm(22, (NO, D, IN_ODD), D ** -0.5),
        'o_q_norm': 1.0 + nrm(23, (NO, HEAD_DIM), 0.02),
        'o_k_norm': 1.0 + nrm(24, (NO, HEAD_DIM), 0.02),
        'na_rpb': nrm(25, (NO, C_HEADS, 2 * NA_ROWS - 1, 2 * NA_COLS - 1), 0.02),
        'ssd_conv_w': nrm(26, (NO, SSD_CONV, SSD_CONV_CH), SSD_CONV ** -0.5),
        'ssd_conv_b': nrm(27, (NO, SSD_CONV_CH), 0.01),
        'ssd_dt_bias': dt0 + jnp.log(-jnp.expm1(-dt0)),
        'ssd_a_log': jnp.log(jax.random.uniform(ks[29], (NO, 2, SSD_HEADS), f32, minval=1.0, maxval=16.0)),
        'ssd_d': 1.0 + nrm(30, (NO, SSD_HEADS), 0.1),
        'ssd_norm_w': 1.0 + nrm(31, (NO, SSD_WIDTH), 0.02),
        'o_w_out': nrm(32, (NO, C_WIDTH + SSD_WIDTH, D), (C_WIDTH + SSD_WIDTH) ** -0.5),
    }


def reference(x, c, e_norm_g, e_ada_w, e_ada_b, e_w_in, e_q_norm, e_k_norm, s5_lam_re, s5_lam_im,
              s5_log_step, s5_b_re, s5_b_im, s5_c_re, s5_c_im, s5_d, s5_w_glu, s5_b_glu, e_w_out,
              o_norm_g, o_ada_w, o_ada_b, o_w_in, o_q_norm, o_k_norm, na_rpb, ssd_conv_w, ssd_conv_b,
              ssd_dt_bias, ssd_a_log, ssd_d, ssd_norm_w, o_w_out):
    for layer in range(DEPTH):
        i = layer // 2
        if layer % 2 == 0:
            x = layer_attn_s5(x, c, e_norm_g[i], e_ada_w[i], e_ada_b[i], e_w_in[i], e_q_norm[i],
                              e_k_norm[i], s5_lam_re[i], s5_lam_im[i], s5_log_step[i], s5_b_re[i],
                              s5_b_im[i], s5_c_re[i], s5_c_im[i], s5_d[i], s5_w_glu[i], s5_b_glu[i],
                              e_w_out[i])
        else:
            x = layer_na_ssd(x, c, o_norm_g[i], o_ada_w[i], o_ada_b[i], o_w_in[i], o_q_norm[i],
                             o_k_norm[i], na_rpb[i], ssd_conv_w[i], ssd_conv_b[i], ssd_dt_bias[i],
                             ssd_a_log[i], ssd_d[i], ssd_norm_w[i], o_w_out[i])
    return x
```

```python
import functools
import math

import jax
import jax.numpy as jnp
import numpy as np
from jax import lax
from jax.experimental import pallas as pl
from jax.experimental.pallas import tpu as pltpu

F32 = jnp.float32
BF16 = jnp.bfloat16
HIGHEST = lax.Precision.HIGHEST

GRID_W = 64
HEAD_DIM = 128
EPS = 1e-6
NEG_INF = -1e30
ROPE_THETA = 10000.0

A_HEADS = 8
A_KV_HEADS = 2
A_GROUP = A_HEADS // A_KV_HEADS
S5_GROUP = 16
S5_GROUPS = 64
S5_STATE = 64
S5_CHUNK = 32
C_HEADS = 8
NA_ROWS = 8
NA_COLS = 16
NA_QROWS = 4
NA_KROWS = 12
SSD_HEADS = 16
SSD_HEAD_DIM = 64
SSD_GROUPS = 2
SSD_STATE = 128
SSD_CONV = 5
SSD_CHUNK = 128
SSD_WIDTH = SSD_HEADS * SSD_HEAD_DIM
SSD_BC = SSD_GROUPS * SSD_STATE

V7X_VMEM_BYTES = 64 * 1024 * 1024
MiB = 1024 * 1024


def _cparams(semantics, vmem_mib):
    assert vmem_mib * MiB < V7X_VMEM_BYTES
    return pltpu.CompilerParams(dimension_semantics=semantics, vmem_limit_bytes=vmem_mib * MiB)


def _silu(x):
    return x * (1.0 / (1.0 + jnp.exp(-x)))


def _sigmoid(x):
    return 1.0 / (1.0 + jnp.exp(-x))


def _ada_kernel(c_ref, w_ref, b_ref, o_ref):
    c = c_ref[...]
    o_ref[...] = jnp.dot(_silu(c), w_ref[...], precision=HIGHEST,
                         preferred_element_type=F32) + b_ref[...]


def _ada_mod(c, w, b):
    d, n = w.shape
    tn = 512
    c8 = jnp.broadcast_to(c.astype(F32), (8, d))
    out = pl.pallas_call(
        _ada_kernel,
        grid=(n // tn,),
        in_specs=[pl.BlockSpec((8, d), lambda j: (0, 0)),
                  pl.BlockSpec((d, tn), lambda j: (0, j)),
                  pl.BlockSpec((1, tn), lambda j: (0, j))],
        out_specs=pl.BlockSpec((8, tn), lambda j: (0, j)),
        out_shape=jax.ShapeDtypeStruct((8, n), F32),
        compiler_params=_cparams(("arbitrary",), 24),
        name="ada_mod",
    )(c8, w, b.reshape(1, n))
    shift, scale, gate = jnp.split(out[0:1], 3, axis=-1)
    return shift, 1.0 + scale, gate


IN_TN = 256
IN_ROW_CHUNK = 64


def _in_proj_kernel(x_ref, g_ref, sc_ref, sh_ref, w_ref, qn_ref, kn_ref, cos_ref, sin_ref,
                    *out_and_scratch, roles, rope, q_scale, has_aux):
    if has_aux:
        o_ref, aux_ref, h_scr = out_and_scratch
    else:
        o_ref, h_scr = out_and_scratch
        aux_ref = None
    j = pl.program_id(1)
    tm = x_ref.shape[0]

    @pl.when(j == 0)
    def _():
        def body(r, carry):
            rows = pl.ds(pl.multiple_of(r * IN_ROW_CHUNK, IN_ROW_CHUNK), IN_ROW_CHUNK)
            xf = x_ref[rows, :]
            ms = jnp.mean(xf * xf, axis=-1, keepdims=True)
            y = xf * lax.rsqrt(ms + EPS) * g_ref[...]
            h_scr[rows, :] = (y * sc_ref[...] + sh_ref[...]).astype(BF16)
            return carry
        lax.fori_loop(0, tm // IN_ROW_CHUNK, body, 0)

    acc = jnp.dot(h_scr[...], w_ref[...], preferred_element_type=F32)

    def head_norm(a, gain):
        ms = jnp.mean(a * a, axis=-1, keepdims=True)
        return a * lax.rsqrt(ms + EPS) * gain

    def qk_epilogue(gain_ref, scale):
        outs = []
        for h in range(IN_TN // HEAD_DIM):
            a = head_norm(acc[:, h * HEAD_DIM:(h + 1) * HEAD_DIM], gain_ref[...])
            if rope:
                a = a * cos_ref[...] + pltpu.roll(a, HEAD_DIM // 2, axis=1) * sin_ref[...]
            if scale != 1.0:
                a = a * scale
            outs.append(a)
        o_ref[...] = jnp.concatenate(outs, axis=-1).astype(o_ref.dtype)

    for lo, hi, role in roles:
        @pl.when((j >= lo) & (j < hi))
        def _(role=role):
            if role == "q":
                qk_epilogue(qn_ref, q_scale)
            elif role == "k":
                qk_epilogue(kn_ref, 1.0)
            elif role == "silu":
                o_ref[...] = _silu(acc).astype(o_ref.dtype)
            elif role == "plain":
                o_ref[...] = acc.astype(o_ref.dtype)
            elif role == "aux":
                o_ref[...] = acc.astype(o_ref.dtype)
                aux_ref[...] = acc
            else:
                raise ValueError(role)


def _in_proj(x, norm_g, scale1p, shift, w_bf16, q_gain, k_gain, cos2, sin2, *, roles, rope,
             q_scale, has_aux, tm=1024):
    seq, d = x.shape
    n = w_bf16.shape[1]
    assert seq % tm == 0 and n % IN_TN == 0
    assert roles[-1][1] == n // IN_TN
    row = lambda i, j: (i, 0)
    const = lambda i, j: (0, 0)
    out_shape = [jax.ShapeDtypeStruct((seq, n), BF16)]
    out_specs = [pl.BlockSpec((tm, IN_TN), lambda i, j: (i, j))]
    if has_aux:
        out_shape.append(jax.ShapeDtypeStruct((seq, IN_TN), F32))
        out_specs.append(pl.BlockSpec((tm, IN_TN), row))
    kern = functools.partial(_in_proj_kernel, roles=roles, rope=rope, q_scale=q_scale,
                             has_aux=has_aux)
    return pl.pallas_call(
        kern,
        grid=(seq // tm, n // IN_TN),
        in_specs=[pl.BlockSpec((tm, d), row),
                  pl.BlockSpec((1, d), const), pl.BlockSpec((1, d), const),
                  pl.BlockSpec((1, d), const),
                  pl.BlockSpec((d, IN_TN), lambda i, j: (0, j)),
                  pl.BlockSpec((1, HEAD_DIM), const), pl.BlockSpec((1, HEAD_DIM), const),
                  pl.BlockSpec((tm, HEAD_DIM), row), pl.BlockSpec((tm, HEAD_DIM), row)],
        out_specs=out_specs,
        out_shape=out_shape,
        scratch_shapes=[pltpu.VMEM((tm, d), BF16)],
        compiler_params=_cparams(("arbitrary", "arbitrary"), 48),
        name="in_proj_rope" if rope else "in_proj",
    )(x, norm_g.reshape(1, d), scale1p, shift, w_bf16, q_gain.reshape(1, HEAD_DIM),
      k_gain.reshape(1, HEAD_DIM), cos2, sin2)


def _gqa_kernel(q_ref, k_ref, v_ref, g_ref, o_ref, m_scr, l_scr, acc_scr):
    ki = pl.program_id(2)

    @pl.when(ki == 0)
    def _():
        m_scr[...] = jnp.full(m_scr.shape, -jnp.inf, F32)
        l_scr[...] = jnp.zeros(l_scr.shape, F32)
        acc_scr[...] = jnp.zeros(acc_scr.shape, F32)

    k = k_ref[...]
    v = v_ref[...]
    for h in range(A_GROUP):
        q = q_ref[:, h * HEAD_DIM:(h + 1) * HEAD_DIM]
        s = lax.dot_general(q, k, (((1,), (1,)), ((), ())), preferred_element_type=F32)
        m_prev = m_scr[h]
        m_new = jnp.maximum(m_prev, jnp.max(s, axis=-1, keepdims=True))
        alpha = jnp.exp(m_prev - m_new)
        p = jnp.exp(s - m_new)
        l_scr[h] = alpha * l_scr[h] + jnp.sum(p, axis=-1, keepdims=True)
        acc_scr[h] = alpha * acc_scr[h] + jnp.dot(p.astype(BF16), v, preferred_element_type=F32)
        m_scr[h] = m_new

    @pl.when(ki == pl.num_programs(2) - 1)
    def _():
        for h in range(A_GROUP):
            cols = slice(h * HEAD_DIM, (h + 1) * HEAD_DIM)
            o = acc_scr[h] * (1.0 / l_scr[h])
            o_ref[:, cols] = (o * g_ref[:, cols].astype(F32)).astype(o_ref.dtype)


def _gqa_attention(proj, *, q_col, k_col, v_col, g_col, tq=512, tk=512):
    seq = proj.shape[0]
    gw = A_GROUP * HEAD_DIM
    assert q_col % gw == 0 and g_col % gw == 0 and k_col % HEAD_DIM == 0 and v_col % HEAD_DIM == 0
    return pl.pallas_call(
        _gqa_kernel,
        grid=(A_KV_HEADS, seq // tq, seq // tk),
        in_specs=[pl.BlockSpec((tq, gw), lambda kh, qi, ki: (qi, q_col // gw + kh)),
                  pl.BlockSpec((tk, HEAD_DIM), lambda kh, qi, ki: (ki, k_col // HEAD_DIM + kh)),
                  pl.BlockSpec((tk, HEAD_DIM), lambda kh, qi, ki: (ki, v_col // HEAD_DIM + kh)),
                  pl.BlockSpec((tq, gw), lambda kh, qi, ki: (qi, g_col // gw + kh))],
        out_specs=pl.BlockSpec((tq, gw), lambda kh, qi, ki: (qi, kh)),
        out_shape=jax.ShapeDtypeStruct((seq, A_HEADS * HEAD_DIM), BF16),
        scratch_shapes=[pltpu.VMEM((A_GROUP, tq, 1), F32), pltpu.VMEM((A_GROUP, tq, 1), F32),
                        pltpu.VMEM((A_GROUP, tq, HEAD_DIM), F32)],
        compiler_params=_cparams(("arbitrary", "arbitrary", "arbitrary"), 40),
        name="gqa_attention",
    )(proj, proj, proj, proj)


def _s5_tables(lam_re, lam_im, log_step, b_re, b_im, c_re, c_im, s5_d):
    t = S5_CHUNK
    g, p, hh = S5_GROUPS, S5_STATE, S5_GROUP
    lr = lam_re.astype(F32)
    li = lam_im.astype(F32)
    dt = jnp.exp(log_step.astype(F32))[..., None]
    mag = jnp.exp(lr * dt)
    ab_re = mag * jnp.cos(li * dt)
    ab_im = mag * jnp.sin(li * dt)
    den = lr * lr + li * li
    num_re = ab_re - 1.0
    f_re = (num_re * lr + ab_im * li) / den
    f_im = (ab_im * lr - num_re * li) / den
    br = b_re.astype(F32)
    bi = b_im.astype(F32)
    bb_re = f_re[..., None] * br - f_im[..., None] * bi
    bb_im = f_re[..., None] * bi + f_im[..., None] * br
    kk = jnp.arange(t + 1, dtype=F32)[None, None, :, None]
    pmag = jnp.exp(kk * (lr * dt)[:, :, None, :])
    ang = kk * (li * dt)[:, :, None, :]
    pw_re = pmag * jnp.cos(ang)
    pw_im = pmag * jnp.sin(ang)
    w_re = pw_re[..., None] * bb_re[:, :, None] - pw_im[..., None] * bb_im[:, :, None]
    w_im = pw_re[..., None] * bb_im[:, :, None] + pw_im[..., None] * bb_re[:, :, None]
    cr = c_re.astype(F32)
    ci = c_im.astype(F32)
    kmat = (jnp.einsum('dgip,dgkpj->dgkij', cr, w_re, precision=HIGHEST)
            - jnp.einsum('dgip,dgkpj->dgkij', ci, w_im, precision=HIGHEST))
    dmat = s5_d.astype(F32).reshape(g, hh)[:, :, None] * jnp.eye(hh, dtype=F32)[None]
    k0 = kmat[0, :, 0] + kmat[1, :, 0] + dmat
    kb = kmat[1, :, 1:t][:, ::-1]
    kf = kmat[0, :, 1:t]
    kcomb = jnp.concatenate([kb, k0[:, None], kf], axis=1)
    tau = jnp.arange(t)
    lag = tau[None, :] - tau[:, None] + t - 1
    m = kcomb[:, lag]
    m = m.transpose(0, 1, 4, 2, 3).reshape(g, t * hh, t * hh)
    def f_part(wr, wi):
        return jnp.concatenate([wr.transpose(0, 1, 3, 2), wi.transpose(0, 1, 3, 2)],
                               axis=-1).reshape(g, t * hh, 2 * p)
    f_f = f_part(w_re[0, :, :t][:, ::-1], w_im[0, :, :t][:, ::-1])
    f_b = f_part(w_re[1, :, :t], w_im[1, :, :t])
    fmat = jnp.concatenate([f_f, f_b], axis=-1)
    cl_re = cr[:, :, None] * pw_re[:, :, :, None, :] - ci[:, :, None] * pw_im[:, :, :, None, :]
    cl_im = cr[:, :, None] * pw_im[:, :, :, None, :] + ci[:, :, None] * pw_re[:, :, :, None, :]
    def e_part(er, ei):
        return jnp.concatenate([er.transpose(0, 3, 1, 2), -ei.transpose(0, 3, 1, 2)],
                               axis=1).reshape(g, 2 * p, t * hh)
    e_f = e_part(cl_re[0, :, 1:t + 1], cl_im[0, :, 1:t + 1])
    e_b = e_part(cl_re[1, :, 1:t + 1][:, ::-1], cl_im[1, :, 1:t + 1][:, ::-1])
    emat = jnp.concatenate([e_f, e_b], axis=1)
    a1 = jnp.concatenate([pw_re[:, :, t], pw_re[:, :, t]], axis=-1)
    a2 = jnp.concatenate([-pw_im[:, :, t], pw_im[:, :, t]], axis=-1)
    return m.astype(BF16), fmat.astype(BF16), emat.astype(BF16), a1, a2


def _s5_state_kernel(u_ref, f_ref, o_ref):
    o_ref[0] = jnp.dot(u_ref[0], f_ref[0], preferred_element_type=F32)


def _s5_scan_kernel(x_ref, a1_ref, a2_ref, o_ref, st_scr):
    d = pl.program_id(0)
    cb = x_ref.shape[0]

    @pl.when(pl.program_id(1) == 0)
    def _():
        st_scr[...] = jnp.zeros(st_scr.shape, F32)

    a1 = a1_ref[0]
    a2 = a2_ref[0]

    def body(i, st):
        idx = jnp.where(d == 0, i, cb - 1 - i)
        o_ref[idx] = st.astype(o_ref.dtype)
        return st * a1 + pltpu.roll(st, S5_STATE, axis=1) * a2 + x_ref[idx]

    st_scr[...] = lax.fori_loop(0, cb, body, st_scr[...])


def _s5_out_kernel(u_ref, m_ref, h_ref, e_ref, o_ref):
    y = jnp.dot(u_ref[0], m_ref[0], preferred_element_type=F32)
    y = y + jnp.dot(h_ref[0], e_ref[0], preferred_element_type=F32)
    c0 = math.sqrt(2.0 / math.pi)
    y = 0.5 * y * (1.0 + jnp.tanh(c0 * (y + 0.044715 * (y * y * y))))
    o_ref[0] = y.astype(o_ref.dtype)


def _s5_mixer(u, tables):
    m, fmat, emat, a1, a2 = tables
    seq = u.shape[0]
    t, g, hh, p = S5_CHUNK, S5_GROUPS, S5_GROUP, S5_STATE
    nc = seq // t
    th = t * hh
    ug = u.reshape(nc, t, g, hh).transpose(2, 0, 1, 3).reshape(g, nc, th)
    hend = pl.pallas_call(
        _s5_state_kernel,
        grid=(g,),
        in_specs=[pl.BlockSpec((1, nc, th), lambda i: (i, 0, 0)),
                  pl.BlockSpec((1, th, 4 * p), lambda i: (i, 0, 0))],
        out_specs=pl.BlockSpec((1, nc, 4 * p), lambda i: (i, 0, 0)),
        out_shape=jax.ShapeDtypeStruct((g, nc, 4 * p), F32),
        compiler_params=_cparams(("arbitrary",), 32),
        name="s5_chunk_state",
    )(ug, fmat)
    hend_t = hend.transpose(1, 0, 2)
    cb = min(nc, 128)
    nb = nc // cb
    flip = lambda d, c: c + d * (nb - 1 - 2 * c)
    hs_t = pl.pallas_call(
        _s5_scan_kernel,
        grid=(2, nb),
        in_specs=[pl.BlockSpec((cb, g, 2 * p), lambda d, c: (flip(d, c), 0, d)),
                  pl.BlockSpec((1, g, 2 * p), lambda d, c: (d, 0, 0)),
                  pl.BlockSpec((1, g, 2 * p), lambda d, c: (d, 0, 0))],
        out_specs=pl.BlockSpec((cb, g, 2 * p), lambda d, c: (flip(d, c), 0, d)),
        out_shape=jax.ShapeDtypeStruct((nc, g, 4 * p), BF16),
        scratch_shapes=[pltpu.VMEM((g, 2 * p), F32)],
        compiler_params=_cparams(("arbitrary", "arbitrary"), 32),
        name="s5_chunk_scan",
    )(hend_t, a1, a2)
    hs = hs_t.transpose(1, 0, 2)
    yg = pl.pallas_call(
        _s5_out_kernel,
        grid=(g,),
        in_specs=[pl.BlockSpec((1, nc, th), lambda i: (i, 0, 0)),
                  pl.BlockSpec((1, th, th), lambda i: (i, 0, 0)),
                  pl.BlockSpec((1, nc, 4 * p), lambda i: (i, 0, 0)),
                  pl.BlockSpec((1, 4 * p, th), lambda i: (i, 0, 0))],
        out_specs=pl.BlockSpec((1, nc, th), lambda i: (i, 0, 0)),
        out_shape=jax.ShapeDtypeStruct((g, nc, th), BF16),
        compiler_params=_cparams(("arbitrary",), 32),
        name="s5_output",
    )(ug, m, hs, emat)
    return yg.reshape(g, nc, t, hh).transpose(1, 2, 0, 3).reshape(seq, g * hh)


def _glu_kernel(y_ref, wv_ref, wg_ref, bv_ref, bg_ref, s_ref, o_ref):
    y = y_ref[...]
    val = jnp.dot(y, wv_ref[...], preferred_element_type=F32) + bv_ref[...]
    gt = jnp.dot(y, wg_ref[...], preferred_element_type=F32) + bg_ref[...]
    o_ref[...] = (val * _sigmoid(gt) * s_ref[...].astype(F32)).astype(o_ref.dtype)


def _glu(y, w_bf16, b, proj, *, s_col, tm=1024, tn=512):
    seq, kdim = y.shape
    width = w_bf16.shape[1] // 2
    assert s_col % tn == 0
    nj = width // tn
    b2 = b.reshape(1, 2 * width).astype(F32)
    return pl.pallas_call(
        _glu_kernel,
        grid=(seq // tm, nj),
        in_specs=[pl.BlockSpec((tm, kdim), lambda i, j: (i, 0)),
                  pl.BlockSpec((kdim, tn), lambda i, j: (0, j)),
                  pl.BlockSpec((kdim, tn), lambda i, j: (0, nj + j)),
                  pl.BlockSpec((1, tn), lambda i, j: (0, j)),
                  pl.BlockSpec((1, tn), lambda i, j: (0, nj + j)),
                  pl.BlockSpec((tm, tn), lambda i, j: (i, s_col // tn + j))],
        out_specs=pl.BlockSpec((tm, tn), lambda i, j: (i, j)),
        out_shape=jax.ShapeDtypeStruct((seq, width), BF16),
        compiler_params=_cparams(("arbitrary", "arbitrary"), 40),
        name="s5_glu",
    )(y, w_bf16, w_bf16, b2, b2, proj)


def _out_proj_kernel(a_ref, b_ref, wa_ref, wb_ref, x_ref, gate_ref, o_ref):
    acc = jnp.dot(a_ref[...], wa_ref[...], preferred_element_type=F32)
    acc = acc + jnp.dot(b_ref[...], wb_ref[...], preferred_element_type=F32)
    o_ref[...] = x_ref[...] + gate_ref[...] * acc


def _out_proj(oa, ob, w_bf16, x, gate, *, tm=1024, tn=512):
    seq, half = oa.shape
    d = w_bf16.shape[1]
    return pl.pallas_call(
        _out_proj_kernel,
        grid=(seq // tm, d // tn),
        in_specs=[pl.BlockSpec((tm, half), lambda i, j: (i, 0)),
                  pl.BlockSpec((tm, half), lambda i, j: (i, 0)),
                  pl.BlockSpec((half, tn), lambda i, j: (0, j)),
                  pl.BlockSpec((half, tn), lambda i, j: (1, j)),
                  pl.BlockSpec((tm, tn), lambda i, j: (i, j)),
                  pl.BlockSpec((1, tn), lambda i, j: (0, j))],
        out_specs=pl.BlockSpec((tm, tn), lambda i, j: (i, j)),
        out_shape=jax.ShapeDtypeStruct((seq, d), F32),
        compiler_params=_cparams(("arbitrary", "arbitrary"), 40),
        name="out_proj",
    )(oa, ob, w_bf16, w_bf16, x, gate)


def _na_bias_tables(rpb, rows):
    w = GRID_W
    nrb = rows // NA_QROWS
    assert rows >= NA_KROWS + NA_QROWS
    rl = jnp.arange(NA_QROWS)
    kl = jnp.arange(NA_KROWS)
    col = jnp.arange(w)
    col_start = jnp.clip(col - NA_COLS // 2, 0, w - NA_COLS)
    col_ok = (col[None, :] >= col_start[:, None]) & (col[None, :] < col_start[:, None] + NA_COLS)
    dc = jnp.clip(col[None, :] - col[:, None], -(NA_COLS - 1), NA_COLS - 1) + NA_COLS - 1
    tabs = []
    for rb in (0, 1, nrb - 1):
        r = rb * NA_QROWS + rl
        ks = min(max(rb * NA_QROWS - NA_ROWS // 2, 0), rows - NA_KROWS)
        rs = jnp.clip(r - NA_ROWS // 2, 0, rows - NA_ROWS)
        kr = ks + kl
        row_ok = (kr[None, :] >= rs[:, None]) & (kr[None, :] < rs[:, None] + NA_ROWS)
        dr = jnp.clip(kr[None, :] - r[:, None] + NA_ROWS - 1, 0, 2 * NA_ROWS - 2)
        b = rpb.astype(F32)[:, dr[:, None, :, None], dc[None, :, None, :]]
        ok = row_ok[:, None, :, None] & col_ok[None, :, None, :]
        b = jnp.where(ok[None], b, NEG_INF)
        tabs.append(b.reshape(rpb.shape[0], NA_QROWS * w, NA_KROWS * w))
    return jnp.stack(tabs)


def _na_kernel(q_ref, k0_ref, k1_ref, k2_ref, v0_ref, v1_ref, v2_ref, b_ref, g_ref, o_ref):
    q = q_ref[...]
    k = jnp.concatenate([k0_ref[...], k1_ref[...], k2_ref[...]], axis=0)
    v = jnp.concatenate([v0_ref[...], v1_ref[...], v2_ref[...]], axis=0)
    s = lax.dot_general(q, k, (((1,), (1,)), ((), ())), preferred_element_type=F32) + b_ref[0, 0]
    m = jnp.max(s, axis=-1, keepdims=True)
    p = jnp.exp(s - m)
    l = jnp.sum(p, axis=-1, keepdims=True)
    o = jnp.dot(p.astype(BF16), v, preferred_element_type=F32) * (1.0 / l)
    o_ref[...] = (o * g_ref[...].astype(F32)).astype(o_ref.dtype)


def _na_attention(proj, bias, *, q_col, k_col, v_col, g_col):
    seq = proj.shape[0]
    tq = NA_QROWS * GRID_W
    nrb = seq // tq
    nkb = NA_KROWS // NA_QROWS
    hd = HEAD_DIM

    def kv_spec(col, off):
        return pl.BlockSpec(
            (tq, hd), lambda h, rb: (jnp.clip(rb - 1, 0, nrb - nkb) + off, col // hd + h))

    btype = lambda h, rb: (jnp.where(rb == 0, 0, jnp.where(rb == nrb - 1, 2, 1)), h, 0, 0)
    return pl.pallas_call(
        _na_kernel,
        grid=(C_HEADS, nrb),
        in_specs=[pl.BlockSpec((tq, hd), lambda h, rb: (rb, q_col // hd + h)),
                  kv_spec(k_col, 0), kv_spec(k_col, 1), kv_spec(k_col, 2),
                  kv_spec(v_col, 0), kv_spec(v_col, 1), kv_spec(v_col, 2),
                  pl.BlockSpec((1, 1, tq, nkb * tq), btype),
                  pl.BlockSpec((tq, hd), lambda h, rb: (rb, g_col // hd + h))],
        out_specs=pl.BlockSpec((tq, hd), lambda h, rb: (rb, h)),
        out_shape=jax.ShapeDtypeStruct((seq, C_HEADS * hd), BF16),
        compiler_params=_cparams(("arbitrary", "arbitrary"), 32),
        name="na_attention",
    )(proj, proj, proj, proj, proj, proj, proj, bias, proj)


CONV_HALO = 8


def _conv_kernel(prev_ref, cur_ref, next_ref, w_ref, b_ref, o_ref):
    i = pl.program_id(0)
    tm = cur_ref.shape[0]
    prev = jnp.where(i == 0, 0.0, prev_ref[...].astype(F32))
    nxt = jnp.where(i == pl.num_programs(0) - 1, 0.0, next_ref[...].astype(F32))
    ext = jnp.concatenate([prev, cur_ref[...].astype(F32), nxt], axis=0)
    acc = jnp.zeros(cur_ref.shape, F32) + b_ref[...]
    for kk in range(SSD_CONV):
        start = CONV_HALO - SSD_CONV // 2 + kk
        acc = acc + ext[start:start + tm, :] * w_ref[kk:kk + 1, :]
    o_ref[...] = _silu(acc).astype(o_ref.dtype)


def _ssd_conv(proj, conv_w, conv_b, *, col, tm=256, tc=512):
    seq = proj.shape[0]
    ch = conv_w.shape[1]
    assert col % tc == 0 and ch % tc == 0
    nh = tm // CONV_HALO
    nblk = seq // CONV_HALO
    cb = col // tc
    return pl.pallas_call(
        _conv_kernel,
        grid=(seq // tm, ch // tc),
        in_specs=[pl.BlockSpec((CONV_HALO, tc), lambda i, j: (jnp.maximum(i * nh - 1, 0), cb + j)),
                  pl.BlockSpec((tm, tc), lambda i, j: (i, cb + j)),
                  pl.BlockSpec((CONV_HALO, tc),
                               lambda i, j: (jnp.minimum((i + 1) * nh, nblk - 1), cb + j)),
                  pl.BlockSpec((SSD_CONV, tc), lambda i, j: (0, j)),
                  pl.BlockSpec((1, tc), lambda i, j: (0, j))],
        out_specs=pl.BlockSpec((tm, tc), lambda i, j: (i, j)),
        out_shape=jax.ShapeDtypeStruct((seq, ch), BF16),
        compiler_params=_cparams(("arbitrary", "arbitrary"), 32),
        name="ssd_conv",
    )(proj, proj, proj, conv_w.astype(F32), conv_b.reshape(1, ch).astype(F32))


def _split_dot(a, b_bf16):
    hi = a.astype(BF16)
    lo = (a - hi.astype(F32)).astype(BF16)
    return (jnp.dot(hi, b_bf16, preferred_element_type=F32)
            + jnp.dot(lo, b_bf16, preferred_element_type=F32))


def _ssd_scan_kernel(xs_ref, b_ref, c_ref, dt_ref, bias_ref, a_ref, ex_ref, o_ref, st_scr):
    d = pl.program_id(0)
    t = xs_ref.shape[0]
    gw = SSD_WIDTH // SSD_GROUPS
    hpg = SSD_HEADS // SSD_GROUPS

    @pl.when(pl.program_id(1) == 0)
    def _():
        st_scr[...] = jnp.zeros(st_scr.shape, F32)

    row = lax.broadcasted_iota(jnp.int32, (t, t), 0)
    colm = lax.broadcasted_iota(jnp.int32, (t, t), 1)
    tri = (row - colm) * (1 - 2 * d) >= 0
    tri_f = tri.astype(F32)

    z = dt_ref[...] + bias_ref[0]
    dt = jnp.maximum(z, 0.0) + jnp.log(1.0 + jnp.exp(-jnp.abs(z)))
    adt = dt * a_ref[0]
    r = jnp.dot(tri_f, adt, precision=HIGHEST, preferred_element_type=F32)
    tot = jnp.where(d == 0, r[t - 1:t, :], r[0:1, :])
    ex = ex_ref[...]
    dt_x = _split_dot(dt, ex)
    er_x = _split_dot(jnp.exp(r), ex)
    sd_x = _split_dot(jnp.exp(tot - r), ex)
    et_x = _split_dot(jnp.exp(tot), ex)
    r_t = r.T

    xs = xs_ref[...].astype(F32)
    xd = xs * dt_x
    xd_b = xd.astype(BF16)
    xdd_b = (xd * sd_x).astype(BF16)

    y_parts = []
    for g in range(SSD_GROUPS):
        bg = b_ref[:, g * SSD_STATE:(g + 1) * SSD_STATE]
        cg = c_ref[:, g * SSD_STATE:(g + 1) * SSD_STATE]
        cb = lax.dot_general(cg, bg, (((1,), (1,)), ((), ())), preferred_element_type=F32)
        lanes = slice(g * gw, (g + 1) * gw)
        s_prev = st_scr[g]
        y_off = jnp.dot(cg, s_prev.astype(BF16), preferred_element_type=F32) * er_x[:, lanes]
        s_loc = lax.dot_general(bg, xdd_b[:, lanes], (((0,), (0,)), ((), ())),
                                preferred_element_type=F32)
        st_scr[g] = s_prev * et_x[:, lanes] + s_loc
        for hh in range(hpg):
            h = g * hpg + hh
            decay = jnp.exp(jnp.where(tri, r[:, h:h + 1] - r_t[h:h + 1, :], NEG_INF))
            sc = (cb * decay).astype(BF16)
            hl = slice(h * SSD_HEAD_DIM, (h + 1) * SSD_HEAD_DIM)
            y_parts.append(jnp.dot(sc, xd_b[:, hl], preferred_element_type=F32)
                           + y_off[:, hh * SSD_HEAD_DIM:(hh + 1) * SSD_HEAD_DIM])
    o_ref[0] = jnp.concatenate(y_parts, axis=-1).astype(o_ref.dtype)


def _ssd_scan(conv, dt_raw, dt_bias, a_log):
    seq = conv.shape[0]
    t = SSD_CHUNK
    nc = seq // t
    a = -jnp.exp(a_log.astype(F32)).reshape(2, 1, SSD_HEADS)
    bias = dt_bias.astype(F32).reshape(2, 1, SSD_HEADS)
    ex = jnp.repeat(jnp.eye(SSD_HEADS, dtype=BF16), SSD_HEAD_DIM, axis=1)
    dt2 = dt_raw[:, :2 * SSD_HEADS].reshape(seq, 2, SSD_HEADS).transpose(1, 0, 2)
    flip = lambda d, c: c + d * (nc - 1 - 2 * c)
    nxb = SSD_WIDTH // SSD_BC
    return pl.pallas_call(
        _ssd_scan_kernel,
        grid=(2, nc),
        in_specs=[pl.BlockSpec((t, SSD_WIDTH), lambda d, c: (flip(d, c), 0)),
                  pl.BlockSpec((t, SSD_BC), lambda d, c: (flip(d, c), nxb)),
                  pl.BlockSpec((t, SSD_BC), lambda d, c: (flip(d, c), nxb + 1)),
                  pl.BlockSpec((None, t, SSD_HEADS), lambda d, c: (d, flip(d, c), 0)),
                  pl.BlockSpec((1, 1, SSD_HEADS), lambda d, c: (d, 0, 0)),
                  pl.BlockSpec((1, 1, SSD_HEADS), lambda d, c: (d, 0, 0)),
                  pl.BlockSpec((SSD_HEADS, SSD_WIDTH), lambda d, c: (0, 0))],
        out_specs=pl.BlockSpec((1, t, SSD_WIDTH), lambda d, c: (d, flip(d, c), 0)),
        out_shape=jax.ShapeDtypeStruct((2, seq, SSD_WIDTH), BF16),
        scratch_shapes=[pltpu.VMEM((SSD_GROUPS, SSD_STATE, SSD_WIDTH // SSD_GROUPS), F32)],
        compiler_params=_cparams(("arbitrary", "arbitrary"), 32),
        name="ssd_scan",
    )(conv, conv, conv, dt2, bias, a, ex)


def _gated_norm_kernel(y_ref, xs_ref, z_ref, d_ref, w_ref, o_ref):
    y = y_ref[0].astype(F32) + y_ref[1].astype(F32) + d_ref[...] * xs_ref[...].astype(F32)
    y = y * z_ref[...].astype(F32)
    ms = jnp.mean(y * y, axis=-1, keepdims=True)
    o_ref[...] = (y * lax.rsqrt(ms + EPS) * w_ref[...]).astype(o_ref.dtype)


def _gated_norm(y2, conv, proj, d_x, norm_w, *, z_col, tm=512):
    seq = conv.shape[0]
    w = SSD_WIDTH
    assert z_col % w == 0
    return pl.pallas_call(
        _gated_norm_kernel,
        grid=(seq // tm,),
        in_specs=[pl.BlockSpec((2, tm, w), lambda i: (0, i, 0)),
                  pl.BlockSpec((tm, w), lambda i: (i, 0)),
                  pl.BlockSpec((tm, w), lambda i: (i, z_col // w)),
                  pl.BlockSpec((1, w), lambda i: (0, 0)),
                  pl.BlockSpec((1, w), lambda i: (0, 0))],
        out_specs=pl.BlockSpec((tm, w), lambda i: (i, 0)),
        out_shape=jax.ShapeDtypeStruct((seq, w), BF16),
        compiler_params=_cparams(("arbitrary",), 32),
        name="ssd_gated_norm",
    )(y2, conv, proj, d_x, norm_w.reshape(1, w).astype(F32))


def _rope_tables(seq):
    t = jnp.arange(seq)
    row = (t // GRID_W).astype(F32)
    col = (t % GRID_W).astype(F32)
    n_axis = HEAD_DIM // 4
    inv = ROPE_THETA ** (-jnp.arange(n_axis, dtype=F32) / n_axis)
    ang = jnp.concatenate([row[:, None] * inv, col[:, None] * inv], axis=-1)
    cos, sin = jnp.cos(ang), jnp.sin(ang)
    return jnp.concatenate([cos, cos], axis=-1), jnp.concatenate([-sin, sin], axis=-1)


def _deinterleave_perm():
    return np.concatenate([np.arange(0, HEAD_DIM, 2), np.arange(1, HEAD_DIM, 2)])


def _layer_attn_s5(x, c, norm_g, ada_w, ada_b, w_in, q_norm, k_norm, lam_re, lam_im, log_step,
                   b_re, b_im, c_re, c_im, s5_d, w_glu, b_glu, w_out):
    seq, d = x.shape
    shift, scale1p, gate = _ada_mod(c, ada_w, ada_b)
    aw = A_HEADS * HEAD_DIM
    akw = A_KV_HEADS * HEAD_DIM
    perm = _deinterleave_perm()
    nqk = (aw + akw) // HEAD_DIM
    colperm = (np.arange(nqk)[:, None] * HEAD_DIM + perm[None, :]).reshape(-1)
    w = jnp.concatenate([w_in[:, colperm], w_in[:, aw + akw:]], axis=1).astype(BF16)
    cos2, sin2 = _rope_tables(seq)
    t = IN_TN
    q_col, k_col, v_col = 0, aw, aw + akw
    g_col = aw + 2 * akw
    u_col = g_col + aw
    gb_col = u_col + d // 2
    roles = ((q_col // t, k_col // t, "q"), (k_col // t, v_col // t, "k"),
             (v_col // t, g_col // t, "plain"), (g_col // t, u_col // t, "silu"),
             (u_col // t, gb_col // t, "plain"), (gb_col // t, (gb_col + d // 2) // t, "silu"))
    (proj,) = _in_proj(x, norm_g, scale1p, shift, w, q_norm[perm], k_norm[perm], cos2, sin2,
                       roles=roles, rope=True, q_scale=HEAD_DIM ** -0.5, has_aux=False)
    o_a = _gqa_attention(proj, q_col=q_col, k_col=k_col, v_col=v_col, g_col=g_col)
    tables = _s5_tables(lam_re, lam_im, log_step, b_re, b_im, c_re, c_im, s5_d)
    y = _s5_mixer(proj[:, u_col:gb_col], tables)
    o_b = _glu(y, w_glu.astype(BF16), b_glu, proj, s_col=gb_col)
    return _out_proj(o_a, o_b, w_out.astype(BF16), x, gate)


def _layer_na_ssd(x, c, norm_g, ada_w, ada_b, w_in, q_norm, k_norm, rpb, conv_w, conv_b,
                  dt_bias, a_log, ssd_d, norm_w, w_out):
    seq, d = x.shape
    shift, scale1p, gate = _ada_mod(c, ada_w, ada_b)
    cw = C_HEADS * HEAD_DIM
    n_in = w_in.shape[1]
    t = IN_TN
    n_pad = -(-n_in // t) * t
    w = jnp.pad(w_in, ((0, 0), (0, n_pad - n_in))).astype(BF16)
    q_col, k_col, v_col, g_col, z_col = 0, cw, 2 * cw, 3 * cw, 4 * cw
    xbc_col = z_col + SSD_WIDTH
    dt_col = xbc_col + SSD_WIDTH + 2 * SSD_BC
    roles = ((q_col // t, k_col // t, "q"), (k_col // t, v_col // t, "k"),
             (v_col // t, g_col // t, "plain"), (g_col // t, xbc_col // t, "silu"),
             (xbc_col // t, dt_col // t, "plain"), (dt_col // t, n_pad // t, "aux"))
    dummy = jnp.zeros((seq, HEAD_DIM), F32)
    proj, dt_raw = _in_proj(x, norm_g, scale1p, shift, w, q_norm, k_norm, dummy, dummy,
                            roles=roles, rope=False, q_scale=HEAD_DIM ** -0.5, has_aux=True)
    bias = _na_bias_tables(rpb, seq // GRID_W)
    o_c = _na_attention(proj, bias, q_col=q_col, k_col=k_col, v_col=v_col, g_col=g_col)
    conv = _ssd_conv(proj, conv_w, conv_b, col=xbc_col)
    y2 = _ssd_scan(conv, dt_raw, dt_bias, a_log)
    d_x = jnp.repeat(ssd_d.astype(F32), SSD_HEAD_DIM).reshape(1, SSD_WIDTH)
    o_d = _gated_norm(y2, conv, proj, d_x, norm_w, z_col=z_col)
    return _out_proj(o_c, o_d, w_out.astype(BF16), x, gate)


def kernel(x, c, e_norm_g, e_ada_w, e_ada_b, e_w_in, e_q_norm, e_k_norm, s5_lam_re, s5_lam_im,
           s5_log_step, s5_b_re, s5_b_im, s5_c_re, s5_c_im, s5_d, s5_w_glu, s5_b_glu, e_w_out,
           o_norm_g, o_ada_w, o_ada_b, o_w_in, o_q_norm, o_k_norm, na_rpb, ssd_conv_w, ssd_conv_b,
           ssd_dt_bias, ssd_a_log, ssd_d, ssd_norm_w, o_w_out):
    assert x.shape[0] == 1
    h = x[0]
    h = _layer_attn_s5(h, c, e_norm_g[0], e_ada_w[0], e_ada_b[0], e_w_in[0], e_q_norm[0],
                       e_k_norm[0], s5_lam_re[0], s5_lam_im[0], s5_log_step[0], s5_b_re[0],
                       s5_b_im[0], s5_c_re[0], s5_c_im[0], s5_d[0], s5_w_glu[0], s5_b_glu[0],
                       e_w_out[0])
    h = _layer_na_ssd(h, c, o_norm_g[0], o_ada_w[0], o_ada_b[0], o_w_in[0], o_q_norm[0],
                      o_k_norm[0], na_rpb[0], ssd_conv_w[0], ssd_conv_b[0], ssd_dt_bias[0],
                      ssd_a_log[0], ssd_d[0], ssd_norm_w[0], o_w_out[0])
    return h[None]
```

```python
import functools
import math

import jax
import jax.numpy as jnp
import numpy as np
from jax import lax
from jax.experimental import pallas as pl
from jax.experimental.pallas import tpu as pltpu

F32 = jnp.float32
BF16 = jnp.bfloat16
HIGHEST = lax.Precision.HIGHEST

GRID_W = 64
HEAD_DIM = 128
EPS = 1e-6
NEG_INF = -1e30
ROPE_THETA = 10000.0

A_HEADS = 8
A_KV_HEADS = 2
A_GROUP = A_HEADS // A_KV_HEADS
S5_GROUP = 16
S5_GROUPS = 64
S5_STATE = 64
S5_CHUNK = 32
C_HEADS = 8
NA_ROWS = 8
NA_COLS = 16
NA_QROWS = 4
NA_KROWS = 12
SSD_HEADS = 16
SSD_HEAD_DIM = 64
SSD_GROUPS = 2
SSD_STATE = 128
SSD_CONV = 5
SSD_CHUNK = 128
SSD_WIDTH = SSD_HEADS * SSD_HEAD_DIM
SSD_BC = SSD_GROUPS * SSD_STATE

V7X_VMEM_BYTES = 64 * 1024 * 1024
MiB = 1024 * 1024


def _cparams(semantics, vmem_mib):
    assert vmem_mib * MiB < V7X_VMEM_BYTES
    return pltpu.CompilerParams(dimension_semantics=semantics, vmem_limit_bytes=vmem_mib * MiB)


def _silu(x):
    return x * (1.0 / (1.0 + jnp.exp(-x)))


def _sigmoid(x):
    return 1.0 / (1.0 + jnp.exp(-x))


def _ada_kernel(c_ref, w_ref, b_ref, o_ref):
    c = c_ref[...]
    o_ref[...] = jnp.dot(_silu(c), w_ref[...], precision=HIGHEST,
                         preferred_element_type=F32) + b_ref[...]


def _ada_mod(c, w, b):
    d, n = w.shape
    tn = 512
    c8 = jnp.broadcast_to(c.astype(F32), (8, d))
    out = pl.pallas_call(
        _ada_kernel,
        grid=(n // tn,),
        in_specs=[pl.BlockSpec((8, d), lambda j: (0, 0)),
                  pl.BlockSpec((d, tn), lambda j: (0, j)),
                  pl.BlockSpec((1, tn), lambda j: (0, j))],
        out_specs=pl.BlockSpec((8, tn), lambda j: (0, j)),
        out_shape=jax.ShapeDtypeStruct((8, n), F32),
        compiler_params=_cparams(("arbitrary",), 24),
        name="ada_mod",
    )(c8, w, b.reshape(1, n))
    shift, scale, gate = jnp.split(out[0:1], 3, axis=-1)
    return shift, 1.0 + scale, gate


IN_TN = 256
IN_ROW_CHUNK = 64


def _in_proj_kernel(x_ref, g_ref, sc_ref, sh_ref, w_ref, qn_ref, kn_ref, cos_ref, sin_ref,
                    *out_and_scratch, roles, rope, q_scale, has_aux):
    if has_aux:
        o_ref, aux_ref, h_scr = out_and_scratch
    else:
        o_ref, h_scr = out_and_scratch
        aux_ref = None
    j = pl.program_id(1)
    tm = x_ref.shape[0]

    @pl.when(j == 0)
    def _():
        def body(r, carry):
            rows = pl.ds(pl.multiple_of(r * IN_ROW_CHUNK, IN_ROW_CHUNK), IN_ROW_CHUNK)
            xf = x_ref[rows, :]
            ms = jnp.mean(xf * xf, axis=-1, keepdims=True)
            y = xf * lax.rsqrt(ms + EPS) * g_ref[...]
            h_scr[rows, :] = (y * sc_ref[...] + sh_ref[...]).astype(BF16)
            return carry
        lax.fori_loop(0, tm // IN_ROW_CHUNK, body, 0)

    acc = jnp.dot(h_scr[...], w_ref[...], preferred_element_type=F32)

    def head_norm(a, gain):
        ms = jnp.mean(a * a, axis=-1, keepdims=True)
        return a * lax.rsqrt(ms + EPS) * gain

    def qk_epilogue(gain_ref, scale):
        outs = []
        for h in range(IN_TN // HEAD_DIM):
            a = head_norm(acc[:, h * HEAD_DIM:(h + 1) * HEAD_DIM], gain_ref[...])
            if rope:
                a = a * cos_ref[...] + pltpu.roll(a, HEAD_DIM // 2, axis=1) * sin_ref[...]
            if scale != 1.0:
                a = a * scale
            outs.append(a)
        o_ref[...] = jnp.concatenate(outs, axis=-1).astype(o_ref.dtype)

    for lo, hi, role in roles:
        @pl.when((j >= lo) & (j < hi))
        def _(role=role):
            if role == "q":
                qk_epilogue(qn_ref, q_scale)
            elif role == "k":
                qk_epilogue(kn_ref, 1.0)
            elif role == "silu":
                o_ref[...] = _silu(acc).astype(o_ref.dtype)
            elif role == "plain":
                o_ref[...] = acc.astype(o_ref.dtype)
            elif role == "aux":
                o_ref[...] = acc.astype(o_ref.dtype)
                aux_ref[...] = acc
            else:
                raise ValueError(role)


def _in_proj(x, norm_g, scale1p, shift, w_bf16, q_gain, k_gain, cos2, sin2, *, roles, rope,
             q_scale, has_aux, tm=1024):
    seq, d = x.shape
    n = w_bf16.shape[1]
    assert seq % tm == 0 and n % IN_TN == 0
    assert roles[-1][1] == n // IN_TN
    row = lambda i, j: (i, 0)
    const = lambda i, j: (0, 0)
    out_shape = [jax.ShapeDtypeStruct((seq, n), BF16)]
    out_specs = [pl.BlockSpec((tm, IN_TN), lambda i, j: (i, j))]
    if has_aux:
        out_shape.append(jax.ShapeDtypeStruct((seq, IN_TN), F32))
        out_specs.append(pl.BlockSpec((tm, IN_TN), row))
    kern = functools.partial(_in_proj_kernel, roles=roles, rope=rope, q_scale=q_scale,
                             has_aux=has_aux)
    return pl.pallas_call(
        kern,
        grid=(seq // tm, n // IN_TN),
        in_specs=[pl.BlockSpec((tm, d), row),
                  pl.BlockSpec((1, d), const), pl.BlockSpec((1, d), const),
                  pl.BlockSpec((1, d), const),
                  pl.BlockSpec((d, IN_TN), lambda i, j: (0, j)),
                  pl.BlockSpec((1, HEAD_DIM), const), pl.BlockSpec((1, HEAD_DIM), const),
                  pl.BlockSpec((tm, HEAD_DIM), row), pl.BlockSpec((tm, HEAD_DIM), row)],
        out_specs=out_specs,
        out_shape=out_shape,
        scratch_shapes=[pltpu.VMEM((tm, d), BF16)],
        compiler_params=_cparams(("arbitrary", "arbitrary"), 48),
        name="in_proj_rope" if rope else "in_proj",
    )(x, norm_g.reshape(1, d), scale1p, shift, w_bf16, q_gain.reshape(1, HEAD_DIM),
      k_gain.reshape(1, HEAD_DIM), cos2, sin2)


GQA_TK = 512


def _gqa_kernel(q_ref, k_ref, v_ref, g_ref, o_ref, acc_scr, m_scr):
    tq = q_ref.shape[0]
    seq = k_ref.shape[0]
    q_all = jnp.concatenate(
        [q_ref[:, h * HEAD_DIM:(h + 1) * HEAD_DIM] for h in range(A_GROUP)], axis=0)
    acc_scr[...] = jnp.zeros(acc_scr.shape, F32)
    m_scr[...] = jnp.full(m_scr.shape, -jnp.inf, F32)
    ones = jnp.ones((GQA_TK, HEAD_DIM), BF16)

    def step(kc, carry):
        rows = pl.ds(pl.multiple_of(kc * GQA_TK, GQA_TK), GQA_TK)
        k = k_ref[rows, :]
        v1 = jnp.concatenate([v_ref[rows, :], ones], axis=1)
        s = lax.dot_general(q_all, k, (((1,), (1,)), ((), ())), preferred_element_type=F32)
        m_prev = m_scr[...]
        m_new = jnp.maximum(m_prev, jnp.max(s, axis=-1, keepdims=True))
        alpha = jnp.exp2(m_prev - m_new)
        p = jnp.concatenate(
            [jnp.exp2(s[:, j * HEAD_DIM:(j + 1) * HEAD_DIM] - m_new).astype(BF16)
             for j in range(GQA_TK // HEAD_DIM)], axis=1)
        pv = jnp.dot(p, v1, preferred_element_type=F32)
        acc_scr[...] = jnp.concatenate([alpha, alpha], axis=1) * acc_scr[...] + pv
        m_scr[...] = m_new
        return carry

    lax.fori_loop(0, seq // GQA_TK, step, 0)
    for h in range(A_GROUP):
        cols = slice(h * HEAD_DIM, (h + 1) * HEAD_DIM)
        a = acc_scr[h * tq:(h + 1) * tq, :]
        o = a[:, :HEAD_DIM] * (1.0 / a[:, HEAD_DIM:])
        o_ref[:, cols] = (o * g_ref[:, cols].astype(F32)).astype(o_ref.dtype)


def _gqa_attention(proj, *, q_col, k_col, v_col, g_col, tq=256):
    seq = proj.shape[0]
    gw = A_GROUP * HEAD_DIM
    assert q_col % gw == 0 and g_col % gw == 0 and k_col % HEAD_DIM == 0 and v_col % HEAD_DIM == 0
    assert seq % GQA_TK == 0 and seq % tq == 0
    return pl.pallas_call(
        _gqa_kernel,
        grid=(A_KV_HEADS, seq // tq),
        in_specs=[pl.BlockSpec((tq, gw), lambda kh, qi: (qi, q_col // gw + kh)),
                  pl.BlockSpec((seq, HEAD_DIM), lambda kh, qi: (0, k_col // HEAD_DIM + kh)),
                  pl.BlockSpec((seq, HEAD_DIM), lambda kh, qi: (0, v_col // HEAD_DIM + kh)),
                  pl.BlockSpec((tq, gw), lambda kh, qi: (qi, g_col // gw + kh))],
        out_specs=pl.BlockSpec((tq, gw), lambda kh, qi: (qi, kh)),
        out_shape=jax.ShapeDtypeStruct((seq, A_HEADS * HEAD_DIM), BF16),
        scratch_shapes=[pltpu.VMEM((A_GROUP * tq, 2 * HEAD_DIM), F32),
                        pltpu.VMEM((A_GROUP * tq, HEAD_DIM), F32)],
        compiler_params=_cparams(("arbitrary", "arbitrary"), 48),
        name="gqa_attention",
    )(proj, proj, proj, proj)


def _s5_tables(lam_re, lam_im, log_step, b_re, b_im, c_re, c_im, s5_d):
    t = S5_CHUNK
    g, p, hh = S5_GROUPS, S5_STATE, S5_GROUP
    lr = lam_re.astype(F32)
    li = lam_im.astype(F32)
    dt = jnp.exp(log_step.astype(F32))[..., None]
    mag = jnp.exp(lr * dt)
    ab_re = mag * jnp.cos(li * dt)
    ab_im = mag * jnp.sin(li * dt)
    den = lr * lr + li * li
    num_re = ab_re - 1.0
    f_re = (num_re * lr + ab_im * li) / den
    f_im = (ab_im * lr - num_re * li) / den
    br = b_re.astype(F32)
    bi = b_im.astype(F32)
    bb_re = f_re[..., None] * br - f_im[..., None] * bi
    bb_im = f_re[..., None] * bi + f_im[..., None] * br
    kk = jnp.arange(t + 1, dtype=F32)[None, None, :, None]
    pmag = jnp.exp(kk * (lr * dt)[:, :, None, :])
    ang = kk * (li * dt)[:, :, None, :]
    pw_re = pmag * jnp.cos(ang)
    pw_im = pmag * jnp.sin(ang)
    w_re = pw_re[..., None] * bb_re[:, :, None] - pw_im[..., None] * bb_im[:, :, None]
    w_im = pw_re[..., None] * bb_im[:, :, None] + pw_im[..., None] * bb_re[:, :, None]
    cr = c_re.astype(F32)
    ci = c_im.astype(F32)
    kmat = (jnp.einsum('dgip,dgkpj->dgkij', cr, w_re, precision=HIGHEST)
            - jnp.einsum('dgip,dgkpj->dgkij', ci, w_im, precision=HIGHEST))
    dmat = s5_d.astype(F32).reshape(g, hh)[:, :, None] * jnp.eye(hh, dtype=F32)[None]
    k0 = kmat[0, :, 0] + kmat[1, :, 0] + dmat
    kb = kmat[1, :, 1:t][:, ::-1]
    kf = kmat[0, :, 1:t]
    kcomb = jnp.concatenate([kb, k0[:, None], kf], axis=1)
    tau = jnp.arange(t)
    lag = tau[None, :] - tau[:, None] + t - 1
    m = kcomb[:, lag]
    m = m.transpose(0, 1, 4, 2, 3).reshape(g, t * hh, t * hh)
    def f_part(wr, wi):
        return jnp.concatenate([wr.transpose(0, 1, 3, 2), wi.transpose(0, 1, 3, 2)],
                               axis=-1).reshape(g, t * hh, 2 * p)
    f_f = f_part(w_re[0, :, :t][:, ::-1], w_im[0, :, :t][:, ::-1])
    f_b = f_part(w_re[1, :, :t], w_im[1, :, :t])
    fmat = jnp.concatenate([f_f, f_b], axis=-1)
    cl_re = cr[:, :, None] * pw_re[:, :, :, None, :] - ci[:, :, None] * pw_im[:, :, :, None, :]
    cl_im = cr[:, :, None] * pw_im[:, :, :, None, :] + ci[:, :, None] * pw_re[:, :, :, None, :]
    def e_part(er, ei):
        return jnp.concatenate([er.transpose(0, 3, 1, 2), -ei.transpose(0, 3, 1, 2)],
                               axis=1).reshape(g, 2 * p, t * hh)
    e_f = e_part(cl_re[0, :, 1:t + 1], cl_im[0, :, 1:t + 1])
    e_b = e_part(cl_re[1, :, 1:t + 1][:, ::-1], cl_im[1, :, 1:t + 1][:, ::-1])
    emat = jnp.concatenate([e_f, e_b], axis=1)
    a1 = jnp.concatenate([pw_re[:, :, t], pw_re[:, :, t]], axis=-1)
    a2 = jnp.concatenate([-pw_im[:, :, t], pw_im[:, :, t]], axis=-1)
    return m.astype(BF16), fmat.astype(BF16), emat.astype(BF16), a1, a2


def _s5_state_kernel(u_ref, f_ref, o_ref):
    o_ref[0] = jnp.dot(u_ref[0], f_ref[0], preferred_element_type=F32)


def _s5_scan_kernel(x_ref, a1_ref, a2_ref, o_ref, st_scr):
    d = pl.program_id(0)
    cb = x_ref.shape[0]

    @pl.when(pl.program_id(1) == 0)
    def _():
        st_scr[...] = jnp.zeros(st_scr.shape, F32)

    a1 = a1_ref[0]
    a2 = a2_ref[0]

    def body(i, st):
        idx = jnp.where(d == 0, i, cb - 1 - i)
        o_ref[idx] = st.astype(o_ref.dtype)
        return st * a1 + pltpu.roll(st, S5_STATE, axis=1) * a2 + x_ref[idx]

    st_scr[...] = lax.fori_loop(0, cb, body, st_scr[...])


def _s5_out_kernel(u_ref, m_ref, h_ref, e_ref, o_ref):
    y = jnp.dot(u_ref[0], m_ref[0], preferred_element_type=F32)
    y = y + jnp.dot(h_ref[0], e_ref[0], preferred_element_type=F32)
    c0 = math.sqrt(2.0 / math.pi)
    y = 0.5 * y * (1.0 + jnp.tanh(c0 * (y + 0.044715 * (y * y * y))))
    o_ref[0] = y.astype(o_ref.dtype)


def _s5_mixer(u, tables):
    m, fmat, emat, a1, a2 = tables
    seq = u.shape[0]
    t, g, hh, p = S5_CHUNK, S5_GROUPS, S5_GROUP, S5_STATE
    nc = seq // t
    th = t * hh
    ug = u.reshape(nc, t, g, hh).transpose(2, 0, 1, 3).reshape(g, nc, th)
    hend = pl.pallas_call(
        _s5_state_kernel,
        grid=(g,),
        in_specs=[pl.BlockSpec((1, nc, th), lambda i: (i, 0, 0)),
                  pl.BlockSpec((1, th, 4 * p), lambda i: (i, 0, 0))],
        out_specs=pl.BlockSpec((1, nc, 4 * p), lambda i: (i, 0, 0)),
        out_shape=jax.ShapeDtypeStruct((g, nc, 4 * p), F32),
        compiler_params=_cparams(("arbitrary",), 32),
        name="s5_chunk_state",
    )(ug, fmat)
    hend_t = hend.transpose(1, 0, 2)
    cb = min(nc, 128)
    nb = nc // cb
    flip = lambda d, c: c + d * (nb - 1 - 2 * c)
    hs_t = pl.pallas_call(
        _s5_scan_kernel,
        grid=(2, nb),
        in_specs=[pl.BlockSpec((cb, g, 2 * p), lambda d, c: (flip(d, c), 0, d)),
                  pl.BlockSpec((1, g, 2 * p), lambda d, c: (d, 0, 0)),
                  pl.BlockSpec((1, g, 2 * p), lambda d, c: (d, 0, 0))],
        out_specs=pl.BlockSpec((cb, g, 2 * p), lambda d, c: (flip(d, c), 0, d)),
        out_shape=jax.ShapeDtypeStruct((nc, g, 4 * p), BF16),
        scratch_shapes=[pltpu.VMEM((g, 2 * p), F32)],
        compiler_params=_cparams(("arbitrary", "arbitrary"), 32),
        name="s5_chunk_scan",
    )(hend_t, a1, a2)
    hs = hs_t.transpose(1, 0, 2)
    yg = pl.pallas_call(
        _s5_out_kernel,
        grid=(g,),
        in_specs=[pl.BlockSpec((1, nc, th), lambda i: (i, 0, 0)),
                  pl.BlockSpec((1, th, th), lambda i: (i, 0, 0)),
                  pl.BlockSpec((1, nc, 4 * p), lambda i: (i, 0, 0)),
                  pl.BlockSpec((1, 4 * p, th), lambda i: (i, 0, 0))],
        out_specs=pl.BlockSpec((1, nc, th), lambda i: (i, 0, 0)),
        out_shape=jax.ShapeDtypeStruct((g, nc, th), BF16),
        compiler_params=_cparams(("arbitrary",), 32),
        name="s5_output",
    )(ug, m, hs, emat)
    return yg.reshape(g, nc, t, hh).transpose(1, 2, 0, 3).reshape(seq, g * hh)


def _glu_kernel(y_ref, wv_ref, wg_ref, bv_ref, bg_ref, s_ref, o_ref):
    y = y_ref[...]
    val = jnp.dot(y, wv_ref[...], preferred_element_type=F32) + bv_ref[...]
    gt = jnp.dot(y, wg_ref[...], preferred_element_type=F32) + bg_ref[...]
    o_ref[...] = (val * _sigmoid(gt) * s_ref[...].astype(F32)).astype(o_ref.dtype)


def _glu(y, w_bf16, b, proj, *, s_col, tm=1024, tn=512):
    seq, kdim = y.shape
    width = w_bf16.shape[1] // 2
    assert s_col % tn == 0
    nj = width // tn
    b2 = b.reshape(1, 2 * width).astype(F32)
    return pl.pallas_call(
        _glu_kernel,
        grid=(seq // tm, nj),
        in_specs=[pl.BlockSpec((tm, kdim), lambda i, j: (i, 0)),
                  pl.BlockSpec((kdim, tn), lambda i, j: (0, j)),
                  pl.BlockSpec((kdim, tn), lambda i, j: (0, nj + j)),
                  pl.BlockSpec((1, tn), lambda i, j: (0, j)),
                  pl.BlockSpec((1, tn), lambda i, j: (0, nj + j)),
                  pl.BlockSpec((tm, tn), lambda i, j: (i, s_col // tn + j))],
        out_specs=pl.BlockSpec((tm, tn), lambda i, j: (i, j)),
        out_shape=jax.ShapeDtypeStruct((seq, width), BF16),
        compiler_params=_cparams(("arbitrary", "arbitrary"), 40),
        name="s5_glu",
    )(y, w_bf16, w_bf16, b2, b2, proj)


def _out_proj_kernel(a_ref, b_ref, wa_ref, wb_ref, x_ref, gate_ref, o_ref):
    acc = jnp.dot(a_ref[...], wa_ref[...], preferred_element_type=F32)
    acc = acc + jnp.dot(b_ref[...], wb_ref[...], preferred_element_type=F32)
    o_ref[...] = x_ref[...] + gate_ref[...] * acc


def _out_proj(oa, ob, w_bf16, x, gate, *, tm=1024, tn=512):
    seq, half = oa.shape
    d = w_bf16.shape[1]
    return pl.pallas_call(
        _out_proj_kernel,
        grid=(seq // tm, d // tn),
        in_specs=[pl.BlockSpec((tm, half), lambda i, j: (i, 0)),
                  pl.BlockSpec((tm, half), lambda i, j: (i, 0)),
                  pl.BlockSpec((half, tn), lambda i, j: (0, j)),
                  pl.BlockSpec((half, tn), lambda i, j: (1, j)),
                  pl.BlockSpec((tm, tn), lambda i, j: (i, j)),
                  pl.BlockSpec((1, tn), lambda i, j: (0, j))],
        out_specs=pl.BlockSpec((tm, tn), lambda i, j: (i, j)),
        out_shape=jax.ShapeDtypeStruct((seq, d), F32),
        compiler_params=_cparams(("arbitrary", "arbitrary"), 40),
        name="out_proj",
    )(oa, ob, w_bf16, w_bf16, x, gate)


def _na_bias_tables(rpb, rows):
    w = GRID_W
    nrb = rows // NA_QROWS
    assert rows >= NA_KROWS + NA_QROWS
    col = np.arange(w)
    col_start = np.clip(col - NA_COLS // 2, 0, w - NA_COLS)
    col_ok = (col[None, :] >= col_start[:, None]) & (col[None, :] < col_start[:, None] + NA_COLS)
    dc = np.clip(col[None, :] - col[:, None], -(NA_COLS - 1), NA_COLS - 1) + NA_COLS - 1
    onehot = (dc[None] == np.arange(2 * NA_COLS - 1)[:, None, None]).astype(np.float32)
    tt = jnp.einsum('hrd,dqk->hrqk', rpb.astype(F32), jnp.asarray(onehot), precision=HIGHEST)
    tt = jnp.where(jnp.asarray(col_ok)[None, None], tt, NEG_INF)
    masked = jnp.full((rpb.shape[0], w, w), NEG_INF, F32)
    tabs = []
    for rb in (0, 1, nrb - 1):
        ks = min(max(rb * NA_QROWS - NA_ROWS // 2, 0), rows - NA_KROWS)
        qrows = []
        for rl in range(NA_QROWS):
            r = rb * NA_QROWS + rl
            rs = min(max(r - NA_ROWS // 2, 0), rows - NA_ROWS)
            blocks = []
            for kl in range(NA_KROWS):
                kr = ks + kl
                blocks.append(tt[:, kr - r + NA_ROWS - 1] if rs <= kr < rs + NA_ROWS else masked)
            qrows.append(jnp.concatenate(blocks, axis=-1))
        tabs.append(jnp.concatenate(qrows, axis=1))
    return jnp.stack(tabs)


def _na_kernel(q_ref, k0_ref, k1_ref, k2_ref, v0_ref, v1_ref, v2_ref, b_ref, g_ref, o_ref):
    q = q_ref[...]
    k = jnp.concatenate([k0_ref[...], k1_ref[...], k2_ref[...]], axis=0)
    v = jnp.concatenate([v0_ref[...], v1_ref[...], v2_ref[...]], axis=0)
    s = lax.dot_general(q, k, (((1,), (1,)), ((), ())), preferred_element_type=F32) + b_ref[0, 0]
    m = jnp.max(s, axis=-1, keepdims=True)
    p = jnp.exp(s - m)
    l = jnp.sum(p, axis=-1, keepdims=True)
    o = jnp.dot(p.astype(BF16), v, preferred_element_type=F32) * (1.0 / l)
    o_ref[...] = (o * g_ref[...].astype(F32)).astype(o_ref.dtype)


def _na_attention(proj, bias, *, q_col, k_col, v_col, g_col):
    seq = proj.shape[0]
    tq = NA_QROWS * GRID_W
    nrb = seq // tq
    nkb = NA_KROWS // NA_QROWS
    hd = HEAD_DIM

    def kv_spec(col, off):
        return pl.BlockSpec(
            (tq, hd), lambda h, rb: (jnp.clip(rb - 1, 0, nrb - nkb) + off, col // hd + h))

    btype = lambda h, rb: (jnp.where(rb == 0, 0, jnp.where(rb == nrb - 1, 2, 1)), h, 0, 0)
    return pl.pallas_call(
        _na_kernel,
        grid=(C_HEADS, nrb),
        in_specs=[pl.BlockSpec((tq, hd), lambda h, rb: (rb, q_col // hd + h)),
                  kv_spec(k_col, 0), kv_spec(k_col, 1), kv_spec(k_col, 2),
                  kv_spec(v_col, 0), kv_spec(v_col, 1), kv_spec(v_col, 2),
                  pl.BlockSpec((1, 1, tq, nkb * tq), btype),
                  pl.BlockSpec((tq, hd), lambda h, rb: (rb, g_col // hd + h))],
        out_specs=pl.BlockSpec((tq, hd), lambda h, rb: (rb, h)),
        out_shape=jax.ShapeDtypeStruct((seq, C_HEADS * hd), BF16),
        compiler_params=_cparams(("arbitrary", "arbitrary"), 32),
        name="na_attention",
    )(proj, proj, proj, proj, proj, proj, proj, bias, proj)


CONV_HALO = 8


def _conv_kernel(prev_ref, cur_ref, next_ref, w_ref, b_ref, o_ref):
    i = pl.program_id(0)
    tm = cur_ref.shape[0]
    prev = jnp.where(i == 0, 0.0, prev_ref[...].astype(F32))
    nxt = jnp.where(i == pl.num_programs(0) - 1, 0.0, next_ref[...].astype(F32))
    ext = jnp.concatenate([prev, cur_ref[...].astype(F32), nxt], axis=0)
    acc = jnp.zeros(cur_ref.shape, F32) + b_ref[...]
    for kk in range(SSD_CONV):
        start = CONV_HALO - SSD_CONV // 2 + kk
        acc = acc + ext[start:start + tm, :] * w_ref[kk:kk + 1, :]
    o_ref[...] = _silu(acc).astype(o_ref.dtype)


def _ssd_conv(proj, conv_w, conv_b, *, col, tm=256, tc=512):
    seq = proj.shape[0]
    ch = conv_w.shape[1]
    assert col % tc == 0 and ch % tc == 0
    nh = tm // CONV_HALO
    nblk = seq // CONV_HALO
    cb = col // tc
    return pl.pallas_call(
        _conv_kernel,
        grid=(seq // tm, ch // tc),
        in_specs=[pl.BlockSpec((CONV_HALO, tc), lambda i, j: (jnp.maximum(i * nh - 1, 0), cb + j)),
                  pl.BlockSpec((tm, tc), lambda i, j: (i, cb + j)),
                  pl.BlockSpec((CONV_HALO, tc),
                               lambda i, j: (jnp.minimum((i + 1) * nh, nblk - 1), cb + j)),
                  pl.BlockSpec((SSD_CONV, tc), lambda i, j: (0, j)),
                  pl.BlockSpec((1, tc), lambda i, j: (0, j))],
        out_specs=pl.BlockSpec((tm, tc), lambda i, j: (i, j)),
        out_shape=jax.ShapeDtypeStruct((seq, ch), BF16),
        compiler_params=_cparams(("arbitrary", "arbitrary"), 32),
        name="ssd_conv",
    )(proj, proj, proj, conv_w.astype(F32), conv_b.reshape(1, ch).astype(F32))


def _split_dot(a, b_bf16):
    hi = a.astype(BF16)
    lo = (a - hi.astype(F32)).astype(BF16)
    return (jnp.dot(hi, b_bf16, preferred_element_type=F32)
            + jnp.dot(lo, b_bf16, preferred_element_type=F32))


def _ssd_scan_kernel(xs_ref, b_ref, c_ref, dt_ref, bias_ref, a_ref, ex_ref, o_ref, st_scr):
    d = pl.program_id(0)
    t = xs_ref.shape[0]
    gw = SSD_WIDTH // SSD_GROUPS
    hpg = SSD_HEADS // SSD_GROUPS

    @pl.when(pl.program_id(1) == 0)
    def _():
        st_scr[...] = jnp.zeros(st_scr.shape, F32)

    row = lax.broadcasted_iota(jnp.int32, (t, t), 0)
    colm = lax.broadcasted_iota(jnp.int32, (t, t), 1)
    tri = (row - colm) * (1 - 2 * d) >= 0
    tri_f = tri.astype(F32)

    z = dt_ref[...] + bias_ref[0]
    dt = jnp.maximum(z, 0.0) + jnp.log(1.0 + jnp.exp(-jnp.abs(z)))
    adt = dt * a_ref[0]
    r = jnp.dot(tri_f, adt, precision=HIGHEST, preferred_element_type=F32)
    tot = jnp.where(d == 0, r[t - 1:t, :], r[0:1, :])
    ex = ex_ref[...]
    dt_x = _split_dot(dt, ex)
    er_x = _split_dot(jnp.exp(r), ex)
    sd_x = _split_dot(jnp.exp(tot - r), ex)
    et_x = _split_dot(jnp.exp(tot), ex)
    r_t = r.T

    xs = xs_ref[...].astype(F32)
    xd = xs * dt_x
    xd_b = xd.astype(BF16)
    xdd_b = (xd * sd_x).astype(BF16)

    y_parts = []
    for g in range(SSD_GROUPS):
        bg = b_ref[:, g * SSD_STATE:(g + 1) * SSD_STATE]
        cg = c_ref[:, g * SSD_STATE:(g + 1) * SSD_STATE]
        cb = lax.dot_general(cg, bg, (((1,), (1,)), ((), ())), preferred_element_type=F32)
        lanes = slice(g * gw, (g + 1) * gw)
        s_prev = st_scr[g]
        y_off = jnp.dot(cg, s_prev.astype(BF16), preferred_element_type=F32) * er_x[:, lanes]
        s_loc = lax.dot_general(bg, xdd_b[:, lanes], (((0,), (0,)), ((), ())),
                                preferred_element_type=F32)
        st_scr[g] = s_prev * et_x[:, lanes] + s_loc
        for hh in range(hpg):
            h = g * hpg + hh
            decay = jnp.exp(jnp.where(tri, r[:, h:h + 1] - r_t[h:h + 1, :], NEG_INF))
            sc = (cb * decay).astype(BF16)
            hl = slice(h * SSD_HEAD_DIM, (h + 1) * SSD_HEAD_DIM)
            y_parts.append(jnp.dot(sc, xd_b[:, hl], preferred_element_type=F32)
                           + y_off[:, hh * SSD_HEAD_DIM:(hh + 1) * SSD_HEAD_DIM])
    o_ref[0] = jnp.concatenate(y_parts, axis=-1).astype(o_ref.dtype)


def _ssd_scan(conv, dt_raw, dt_bias, a_log):
    seq = conv.shape[0]
    t = SSD_CHUNK
    nc = seq // t
    a = -jnp.exp(a_log.astype(F32)).reshape(2, 1, SSD_HEADS)
    bias = dt_bias.astype(F32).reshape(2, 1, SSD_HEADS)
    ex = jnp.repeat(jnp.eye(SSD_HEADS, dtype=BF16), SSD_HEAD_DIM, axis=1)
    dt2 = dt_raw[:, :2 * SSD_HEADS].reshape(seq, 2, SSD_HEADS).transpose(1, 0, 2)
    flip = lambda d, c: c + d * (nc - 1 - 2 * c)
    nxb = SSD_WIDTH // SSD_BC
    return pl.pallas_call(
        _ssd_scan_kernel,
        grid=(2, nc),
        in_specs=[pl.BlockSpec((t, SSD_WIDTH), lambda d, c: (flip(d, c), 0)),
                  pl.BlockSpec((t, SSD_BC), lambda d, c: (flip(d, c), nxb)),
                  pl.BlockSpec((t, SSD_BC), lambda d, c: (flip(d, c), nxb + 1)),
                  pl.BlockSpec((None, t, SSD_HEADS), lambda d, c: (d, flip(d, c), 0)),
                  pl.BlockSpec((1, 1, SSD_HEADS), lambda d, c: (d, 0, 0)),
                  pl.BlockSpec((1, 1, SSD_HEADS), lambda d, c: (d, 0, 0)),
                  pl.BlockSpec((SSD_HEADS, SSD_WIDTH), lambda d, c: (0, 0))],
        out_specs=pl.BlockSpec((1, t, SSD_WIDTH), lambda d, c: (d, flip(d, c), 0)),
        out_shape=jax.ShapeDtypeStruct((2, seq, SSD_WIDTH), BF16),
        scratch_shapes=[pltpu.VMEM((SSD_GROUPS, SSD_STATE, SSD_WIDTH // SSD_GROUPS), F32)],
        compiler_params=_cparams(("arbitrary", "arbitrary"), 32),
        name="ssd_scan",
    )(conv, conv, conv, dt2, bias, a, ex)


def _gated_norm_kernel(y_ref, xs_ref, z_ref, d_ref, w_ref, o_ref):
    y = y_ref[0].astype(F32) + y_ref[1].astype(F32) + d_ref[...] * xs_ref[...].astype(F32)
    y = y * z_ref[...].astype(F32)
    ms = jnp.mean(y * y, axis=-1, keepdims=True)
    o_ref[...] = (y * lax.rsqrt(ms + EPS) * w_ref[...]).astype(o_ref.dtype)


def _gated_norm(y2, conv, proj, d_x, norm_w, *, z_col, tm=512):
    seq = conv.shape[0]
    w = SSD_WIDTH
    assert z_col % w == 0
    return pl.pallas_call(
        _gated_norm_kernel,
        grid=(seq // tm,),
        in_specs=[pl.BlockSpec((2, tm, w), lambda i: (0, i, 0)),
                  pl.BlockSpec((tm, w), lambda i: (i, 0)),
                  pl.BlockSpec((tm, w), lambda i: (i, z_col // w)),
                  pl.BlockSpec((1, w), lambda i: (0, 0)),
                  pl.BlockSpec((1, w), lambda i: (0, 0))],
        out_specs=pl.BlockSpec((tm, w), lambda i: (i, 0)),
        out_shape=jax.ShapeDtypeStruct((seq, w), BF16),
        compiler_params=_cparams(("arbitrary",), 32),
        name="ssd_gated_norm",
    )(y2, conv, proj, d_x, norm_w.reshape(1, w).astype(F32))


def _rope_tables(seq):
    t = jnp.arange(seq)
    row = (t // GRID_W).astype(F32)
    col = (t % GRID_W).astype(F32)
    n_axis = HEAD_DIM // 4
    inv = ROPE_THETA ** (-jnp.arange(n_axis, dtype=F32) / n_axis)
    ang = jnp.concatenate([row[:, None] * inv, col[:, None] * inv], axis=-1)
    cos, sin = jnp.cos(ang), jnp.sin(ang)
    return jnp.concatenate([cos, cos], axis=-1), jnp.concatenate([-sin, sin], axis=-1)


def _deinterleave_perm():
    return np.concatenate([np.arange(0, HEAD_DIM, 2), np.arange(1, HEAD_DIM, 2)])


def _layer_attn_s5(x, c, norm_g, ada_w, ada_b, w_in, q_norm, k_norm, lam_re, lam_im, log_step,
                   b_re, b_im, c_re, c_im, s5_d, w_glu, b_glu, w_out):
    seq, d = x.shape
    shift, scale1p, gate = _ada_mod(c, ada_w, ada_b)
    aw = A_HEADS * HEAD_DIM
    akw = A_KV_HEADS * HEAD_DIM
    perm = _deinterleave_perm()
    nqk = (aw + akw) // HEAD_DIM
    colperm = (np.arange(nqk)[:, None] * HEAD_DIM + perm[None, :]).reshape(-1)
    w = jnp.concatenate([w_in[:, colperm], w_in[:, aw + akw:]], axis=1).astype(BF16)
    cos2, sin2 = _rope_tables(seq)
    t = IN_TN
    q_col, k_col, v_col = 0, aw, aw + akw
    g_col = aw + 2 * akw
    u_col = g_col + aw
    gb_col = u_col + d // 2
    roles = ((q_col // t, k_col // t, "q"), (k_col // t, v_col // t, "k"),
             (v_col // t, g_col // t, "plain"), (g_col // t, u_col // t, "silu"),
             (u_col // t, gb_col // t, "plain"), (gb_col // t, (gb_col + d // 2) // t, "silu"))
    (proj,) = _in_proj(x, norm_g, scale1p, shift, w, q_norm[perm], k_norm[perm], cos2, sin2,
                       roles=roles, rope=True, q_scale=HEAD_DIM ** -0.5 * math.log2(math.e),
                       has_aux=False)
    o_a = _gqa_attention(proj, q_col=q_col, k_col=k_col, v_col=v_col, g_col=g_col)
    tables = _s5_tables(lam_re, lam_im, log_step, b_re, b_im, c_re, c_im, s5_d)
    y = _s5_mixer(proj[:, u_col:gb_col], tables)
    o_b = _glu(y, w_glu.astype(BF16), b_glu, proj, s_col=gb_col)
    return _out_proj(o_a, o_b, w_out.astype(BF16), x, gate)


def _layer_na_ssd(x, c, norm_g, ada_w, ada_b, w_in, q_norm, k_norm, rpb, conv_w, conv_b,
                  dt_bias, a_log, ssd_d, norm_w, w_out):
    seq, d = x.shape
    shift, scale1p, gate = _ada_mod(c, ada_w, ada_b)
    cw = C_HEADS * HEAD_DIM
    n_in = w_in.shape[1]
    t = IN_TN
    n_pad = -(-n_in // t) * t
    w = jnp.pad(w_in, ((0, 0), (0, n_pad - n_in))).astype(BF16)
    q_col, k_col, v_col, g_col, z_col = 0, cw, 2 * cw, 3 * cw, 4 * cw
    xbc_col = z_col + SSD_WIDTH
    dt_col = xbc_col + SSD_WIDTH + 2 * SSD_BC
    roles = ((q_col // t, k_col // t, "q"), (k_col // t, v_col // t, "k"),
             (v_col // t, g_col // t, "plain"), (g_col // t, xbc_col // t, "silu"),
             (xbc_col // t, dt_col // t, "plain"), (dt_col // t, n_pad // t, "aux"))
    dummy = jnp.zeros((seq, HEAD_DIM), F32)
    proj, dt_raw = _in_proj(x, norm_g, scale1p, shift, w, q_norm, k_norm, dummy, dummy,
                            roles=roles, rope=False, q_scale=HEAD_DIM ** -0.5, has_aux=True)
    bias = _na_bias_tables(rpb, seq // GRID_W)
    o_c = _na_attention(proj, bias, q_col=q_col, k_col=k_col, v_col=v_col, g_col=g_col)
    conv = _ssd_conv(proj, conv_w, conv_b, col=xbc_col)
    y2 = _ssd_scan(conv, dt_raw, dt_bias, a_log)
    d_x = jnp.repeat(ssd_d.astype(F32), SSD_HEAD_DIM).reshape(1, SSD_WIDTH)
    o_d = _gated_norm(y2, conv, proj, d_x, norm_w, z_col=z_col)
    return _out_proj(o_c, o_d, w_out.astype(BF16), x, gate)


def kernel(x, c, e_norm_g, e_ada_w, e_ada_b, e_w_in, e_q_norm, e_k_norm, s5_lam_re, s5_lam_im,
           s5_log_step, s5_b_re, s5_b_im, s5_c_re, s5_c_im, s5_d, s5_w_glu, s5_b_glu, e_w_out,
           o_norm_g, o_ada_w, o_ada_b, o_w_in, o_q_norm, o_k_norm, na_rpb, ssd_conv_w, ssd_conv_b,
           ssd_dt_bias, ssd_a_log, ssd_d, ssd_norm_w, o_w_out):
    assert x.shape[0] == 1
    h = x[0]
    h = _layer_attn_s5(h, c, e_norm_g[0], e_ada_w[0], e_ada_b[0], e_w_in[0], e_q_norm[0],
                       e_k_norm[0], s5_lam_re[0], s5_lam_im[0], s5_log_step[0], s5_b_re[0],
                       s5_b_im[0], s5_c_re[0], s5_c_im[0], s5_d[0], s5_w_glu[0], s5_b_glu[0],
                       e_w_out[0])
    h = _layer_na_ssd(h, c, o_norm_g[0], o_ada_w[0], o_ada_b[0], o_w_in[0], o_q_norm[0],
                      o_k_norm[0], na_rpb[0], ssd_conv_w[0], ssd_conv_b[0], ssd_dt_bias[0],
                      ssd_a_log[0], ssd_d[0], ssd_norm_w[0], o_w_out[0])
    return h[None]
```

```python
import functools
import math

import jax
import jax.numpy as jnp
import numpy as np
from jax import lax
from jax.experimental import pallas as pl
from jax.experimental.pallas import tpu as pltpu

F32 = jnp.float32
BF16 = jnp.bfloat16
HIGHEST = lax.Precision.HIGHEST

GRID_W = 64
HEAD_DIM = 128
EPS = 1e-6
NEG_INF = -1e30
ROPE_THETA = 10000.0

A_HEADS = 8
A_KV_HEADS = 2
A_GROUP = A_HEADS // A_KV_HEADS
S5_GROUP = 16
S5_GROUPS = 64
S5_STATE = 64
S5_CHUNK = 32
C_HEADS = 8
NA_ROWS = 8
NA_COLS = 16
NA_QROWS = 4
NA_KROWS = 12
SSD_HEADS = 16
SSD_HEAD_DIM = 64
SSD_GROUPS = 2
SSD_STATE = 128
SSD_CONV = 5
SSD_CHUNK = 128
SSD_WIDTH = SSD_HEADS * SSD_HEAD_DIM
SSD_BC = SSD_GROUPS * SSD_STATE

V7X_VMEM_BYTES = 64 * 1024 * 1024
MiB = 1024 * 1024


def _cparams(semantics, vmem_mib):
    assert vmem_mib * MiB < V7X_VMEM_BYTES
    return pltpu.CompilerParams(dimension_semantics=semantics, vmem_limit_bytes=vmem_mib * MiB)


def _silu(x):
    return x * (1.0 / (1.0 + jnp.exp(-x)))


def _sigmoid(x):
    return 1.0 / (1.0 + jnp.exp(-x))


def _ada_kernel(c_ref, w_ref, b_ref, o_ref):
    c = c_ref[...]
    o_ref[...] = jnp.dot(_silu(c), w_ref[...], precision=HIGHEST,
                         preferred_element_type=F32) + b_ref[...]


def _ada_mod(c, w, b):
    d, n = w.shape
    tn = 512
    c8 = jnp.broadcast_to(c.astype(F32), (8, d))
    out = pl.pallas_call(
        _ada_kernel,
        grid=(n // tn,),
        in_specs=[pl.BlockSpec((8, d), lambda j: (0, 0)),
                  pl.BlockSpec((d, tn), lambda j: (0, j)),
                  pl.BlockSpec((1, tn), lambda j: (0, j))],
        out_specs=pl.BlockSpec((8, tn), lambda j: (0, j)),
        out_shape=jax.ShapeDtypeStruct((8, n), F32),
        compiler_params=_cparams(("arbitrary",), 24),
        name="ada_mod",
    )(c8, w, b.reshape(1, n))
    shift, scale, gate = jnp.split(out[0:1], 3, axis=-1)
    return shift, 1.0 + scale, gate


IN_TN = 256
IN_ROW_CHUNK = 64


def _in_proj_kernel(x_ref, g_ref, sc_ref, sh_ref, w_ref, qn_ref, kn_ref, cos_ref, sin_ref,
                    *out_and_scratch, roles, rope, q_scale, has_aux):
    if has_aux:
        o_ref, aux_ref, h_scr = out_and_scratch
    else:
        o_ref, h_scr = out_and_scratch
        aux_ref = None
    j = pl.program_id(1)
    tm = x_ref.shape[0]

    @pl.when(j == 0)
    def _():
        def body(r, carry):
            rows = pl.ds(pl.multiple_of(r * IN_ROW_CHUNK, IN_ROW_CHUNK), IN_ROW_CHUNK)
            xf = x_ref[rows, :]
            ms = jnp.mean(xf * xf, axis=-1, keepdims=True)
            y = xf * lax.rsqrt(ms + EPS) * g_ref[...]
            h_scr[rows, :] = (y * sc_ref[...] + sh_ref[...]).astype(BF16)
            return carry
        lax.fori_loop(0, tm // IN_ROW_CHUNK, body, 0)

    acc = jnp.dot(h_scr[...], w_ref[...], preferred_element_type=F32)

    def head_norm(a, gain):
        ms = jnp.mean(a * a, axis=-1, keepdims=True)
        return a * lax.rsqrt(ms + EPS) * gain

    def qk_epilogue(gain_ref, scale):
        outs = []
        for h in range(IN_TN // HEAD_DIM):
            a = head_norm(acc[:, h * HEAD_DIM:(h + 1) * HEAD_DIM], gain_ref[...])
            if rope:
                a = a * cos_ref[...] + pltpu.roll(a, HEAD_DIM // 2, axis=1) * sin_ref[...]
            if scale != 1.0:
                a = a * scale
            outs.append(a)
        o_ref[...] = jnp.concatenate(outs, axis=-1).astype(o_ref.dtype)

    for lo, hi, role in roles:
        @pl.when((j >= lo) & (j < hi))
        def _(role=role):
            if role == "q":
                qk_epilogue(qn_ref, q_scale)
            elif role == "k":
                qk_epilogue(kn_ref, 1.0)
            elif role == "silu":
                o_ref[...] = _silu(acc).astype(o_ref.dtype)
            elif role == "plain":
                o_ref[...] = acc.astype(o_ref.dtype)
            elif role == "aux":
                o_ref[...] = acc.astype(o_ref.dtype)
                aux_ref[...] = acc
            else:
                raise ValueError(role)


def _in_proj(x, norm_g, scale1p, shift, w_bf16, q_gain, k_gain, cos2, sin2, *, roles, rope,
             q_scale, has_aux, tm=1024):
    seq, d = x.shape
    n = w_bf16.shape[1]
    assert seq % tm == 0 and n % IN_TN == 0
    assert roles[-1][1] == n // IN_TN
    row = lambda i, j: (i, 0)
    const = lambda i, j: (0, 0)
    out_shape = [jax.ShapeDtypeStruct((seq, n), BF16)]
    out_specs = [pl.BlockSpec((tm, IN_TN), lambda i, j: (i, j))]
    if has_aux:
        out_shape.append(jax.ShapeDtypeStruct((seq, IN_TN), F32))
        out_specs.append(pl.BlockSpec((tm, IN_TN), row))
    kern = functools.partial(_in_proj_kernel, roles=roles, rope=rope, q_scale=q_scale,
                             has_aux=has_aux)
    return pl.pallas_call(
        kern,
        grid=(seq // tm, n // IN_TN),
        in_specs=[pl.BlockSpec((tm, d), row),
                  pl.BlockSpec((1, d), const), pl.BlockSpec((1, d), const),
                  pl.BlockSpec((1, d), const),
                  pl.BlockSpec((d, IN_TN), lambda i, j: (0, j)),
                  pl.BlockSpec((1, HEAD_DIM), const), pl.BlockSpec((1, HEAD_DIM), const),
                  pl.BlockSpec((tm, HEAD_DIM), row), pl.BlockSpec((tm, HEAD_DIM), row)],
        out_specs=out_specs,
        out_shape=out_shape,
        scratch_shapes=[pltpu.VMEM((tm, d), BF16)],
        compiler_params=_cparams(("arbitrary", "arbitrary"), 48),
        name="in_proj_rope" if rope else "in_proj",
    )(x, norm_g.reshape(1, d), scale1p, shift, w_bf16, q_gain.reshape(1, HEAD_DIM),
      k_gain.reshape(1, HEAD_DIM), cos2, sin2)


GQA_TK = 1024


def _gqa_kernel(q_ref, k_ref, v_ref, g_ref, o_ref, acc_scr, m_scr, s_scr):
    tq = q_ref.shape[0]
    nk = k_ref.shape[0] // GQA_TK
    q_all = jnp.concatenate(
        [q_ref[:, h * HEAD_DIM:(h + 1) * HEAD_DIM] for h in range(A_GROUP)], axis=0)
    acc_scr[...] = jnp.zeros(acc_scr.shape, F32)
    m_scr[...] = jnp.full(m_scr.shape, -jnp.inf, F32)
    ones = jnp.ones((GQA_TK, HEAD_DIM), BF16)

    def key_rows(kc):
        return pl.ds(pl.multiple_of(kc * GQA_TK, GQA_TK), GQA_TK)

    def scores(kc):
        return lax.dot_general(q_all, k_ref[key_rows(kc), :], (((1,), (1,)), ((), ())),
                               preferred_element_type=F32)

    def softmax_pv(slot, kc):
        s = s_scr[slot]
        v1 = jnp.concatenate([v_ref[key_rows(kc), :], ones], axis=1)
        m_prev = m_scr[...]
        m_new = jnp.maximum(m_prev, jnp.max(s, axis=-1, keepdims=True))
        alpha = jnp.exp2(m_prev - m_new)
        p = jnp.concatenate(
            [jnp.exp2(s[:, j * HEAD_DIM:(j + 1) * HEAD_DIM] - m_new).astype(BF16)
             for j in range(GQA_TK // HEAD_DIM)], axis=1)
        pv = jnp.dot(p, v1, preferred_element_type=F32)
        acc_scr[...] = jnp.concatenate([alpha, alpha], axis=1) * acc_scr[...] + pv
        m_scr[...] = m_new

    s_scr[0] = scores(0)

    def pair(i, carry):
        kc = 2 * i
        s_scr[1] = scores(kc + 1)
        softmax_pv(0, kc)
        s_scr[0] = scores(kc + 2)
        softmax_pv(1, kc + 1)
        return carry

    lax.fori_loop(0, nk // 2 - 1, pair, 0)
    s_scr[1] = scores(nk - 1)
    softmax_pv(0, nk - 2)
    softmax_pv(1, nk - 1)
    for h in range(A_GROUP):
        cols = slice(h * HEAD_DIM, (h + 1) * HEAD_DIM)
        a = acc_scr[h * tq:(h + 1) * tq, :]
        o = a[:, :HEAD_DIM] * (1.0 / a[:, HEAD_DIM:])
        o_ref[:, cols] = (o * g_ref[:, cols].astype(F32)).astype(o_ref.dtype)


def _gqa_attention(proj, *, q_col, k_col, v_col, g_col, tq=256):
    seq = proj.shape[0]
    gw = A_GROUP * HEAD_DIM
    assert q_col % gw == 0 and g_col % gw == 0 and k_col % HEAD_DIM == 0 and v_col % HEAD_DIM == 0
    assert seq % (2 * GQA_TK) == 0 and seq % tq == 0
    return pl.pallas_call(
        _gqa_kernel,
        grid=(A_KV_HEADS, seq // tq),
        in_specs=[pl.BlockSpec((tq, gw), lambda kh, qi: (qi, q_col // gw + kh)),
                  pl.BlockSpec((seq, HEAD_DIM), lambda kh, qi: (0, k_col // HEAD_DIM + kh)),
                  pl.BlockSpec((seq, HEAD_DIM), lambda kh, qi: (0, v_col // HEAD_DIM + kh)),
                  pl.BlockSpec((tq, gw), lambda kh, qi: (qi, g_col // gw + kh))],
        out_specs=pl.BlockSpec((tq, gw), lambda kh, qi: (qi, kh)),
        out_shape=jax.ShapeDtypeStruct((seq, A_HEADS * HEAD_DIM), BF16),
        scratch_shapes=[pltpu.VMEM((A_GROUP * tq, 2 * HEAD_DIM), F32),
                        pltpu.VMEM((A_GROUP * tq, HEAD_DIM), F32),
                        pltpu.VMEM((2, A_GROUP * tq, GQA_TK), F32)],
        compiler_params=_cparams(("arbitrary", "arbitrary"), 56),
        name="gqa_attention",
    )(proj, proj, proj, proj)


def _s5_tables(lam_re, lam_im, log_step, b_re, b_im, c_re, c_im, s5_d):
    t = S5_CHUNK
    g, p, hh = S5_GROUPS, S5_STATE, S5_GROUP
    lr = lam_re.astype(F32)
    li = lam_im.astype(F32)
    dt = jnp.exp(log_step.astype(F32))[..., None]
    mag = jnp.exp(lr * dt)
    ab_re = mag * jnp.cos(li * dt)
    ab_im = mag * jnp.sin(li * dt)
    den = lr * lr + li * li
    num_re = ab_re - 1.0
    f_re = (num_re * lr + ab_im * li) / den
    f_im = (ab_im * lr - num_re * li) / den
    br = b_re.astype(F32)
    bi = b_im.astype(F32)
    bb_re = f_re[..., None] * br - f_im[..., None] * bi
    bb_im = f_re[..., None] * bi + f_im[..., None] * br
    kk = jnp.arange(t + 1, dtype=F32)[None, None, :, None]
    pmag = jnp.exp(kk * (lr * dt)[:, :, None, :])
    ang = kk * (li * dt)[:, :, None, :]
    pw_re = pmag * jnp.cos(ang)
    pw_im = pmag * jnp.sin(ang)
    w_re = pw_re[..., None] * bb_re[:, :, None] - pw_im[..., None] * bb_im[:, :, None]
    w_im = pw_re[..., None] * bb_im[:, :, None] + pw_im[..., None] * bb_re[:, :, None]
    cr = c_re.astype(F32)
    ci = c_im.astype(F32)
    kmat = (jnp.einsum('dgip,dgkpj->dgkij', cr, w_re, precision=HIGHEST)
            - jnp.einsum('dgip,dgkpj->dgkij', ci, w_im, precision=HIGHEST))
    dmat = s5_d.astype(F32).reshape(g, hh)[:, :, None] * jnp.eye(hh, dtype=F32)[None]
    k0 = kmat[0, :, 0] + kmat[1, :, 0] + dmat
    kb = kmat[1, :, 1:t][:, ::-1]
    kf = kmat[0, :, 1:t]
    kcomb = jnp.concatenate([kb, k0[:, None], kf], axis=1)
    tau = jnp.arange(t)
    lag = tau[None, :] - tau[:, None] + t - 1
    m = kcomb[:, lag]
    m = m.transpose(0, 1, 4, 2, 3).reshape(g, t * hh, t * hh)
    def f_part(wr, wi):
        return jnp.concatenate([wr.transpose(0, 1, 3, 2), wi.transpose(0, 1, 3, 2)],
                               axis=-1).reshape(g, t * hh, 2 * p)
    f_f = f_part(w_re[0, :, :t][:, ::-1], w_im[0, :, :t][:, ::-1])
    f_b = f_part(w_re[1, :, :t], w_im[1, :, :t])
    fmat = jnp.concatenate([f_f, f_b], axis=-1)
    cl_re = cr[:, :, None] * pw_re[:, :, :, None, :] - ci[:, :, None] * pw_im[:, :, :, None, :]
    cl_im = cr[:, :, None] * pw_im[:, :, :, None, :] + ci[:, :, None] * pw_re[:, :, :, None, :]
    def e_part(er, ei):
        return jnp.concatenate([er.transpose(0, 3, 1, 2), -ei.transpose(0, 3, 1, 2)],
                               axis=1).reshape(g, 2 * p, t * hh)
    e_f = e_part(cl_re[0, :, 1:t + 1], cl_im[0, :, 1:t + 1])
    e_b = e_part(cl_re[1, :, 1:t + 1][:, ::-1], cl_im[1, :, 1:t + 1][:, ::-1])
    emat = jnp.concatenate([e_f, e_b], axis=1)
    a1 = jnp.concatenate([pw_re[:, :, t], pw_re[:, :, t]], axis=-1)
    a2 = jnp.concatenate([-pw_im[:, :, t], pw_im[:, :, t]], axis=-1)
    return m.astype(BF16), fmat.astype(BF16), emat.astype(BF16), a1, a2


def _s5_state_kernel(u_ref, f_ref, o_ref):
    o_ref[0] = jnp.dot(u_ref[0], f_ref[0], preferred_element_type=F32)


def _s5_scan_kernel(x_ref, a1_ref, a2_ref, o_ref, st_scr):
    d = pl.program_id(0)
    cb = x_ref.shape[0]

    @pl.when(pl.program_id(1) == 0)
    def _():
        st_scr[...] = jnp.zeros(st_scr.shape, F32)

    a1 = a1_ref[0]
    a2 = a2_ref[0]

    def body(i, st):
        idx = jnp.where(d == 0, i, cb - 1 - i)
        o_ref[idx] = st.astype(o_ref.dtype)
        return st * a1 + pltpu.roll(st, S5_STATE, axis=1) * a2 + x_ref[idx]

    st_scr[...] = lax.fori_loop(0, cb, body, st_scr[...])


def _s5_out_kernel(u_ref, m_ref, h_ref, e_ref, o_ref):
    y = jnp.dot(u_ref[0], m_ref[0], preferred_element_type=F32)
    y = y + jnp.dot(h_ref[0], e_ref[0], preferred_element_type=F32)
    c0 = math.sqrt(2.0 / math.pi)
    y = 0.5 * y * (1.0 + jnp.tanh(c0 * (y + 0.044715 * (y * y * y))))
    o_ref[0] = y.astype(o_ref.dtype)


def _s5_mixer(u, tables):
    m, fmat, emat, a1, a2 = tables
    seq = u.shape[0]
    t, g, hh, p = S5_CHUNK, S5_GROUPS, S5_GROUP, S5_STATE
    nc = seq // t
    th = t * hh
    ug = u.reshape(nc, t, g, hh).transpose(2, 0, 1, 3).reshape(g, nc, th)
    hend = pl.pallas_call(
        _s5_state_kernel,
        grid=(g,),
        in_specs=[pl.BlockSpec((1, nc, th), lambda i: (i, 0, 0)),
                  pl.BlockSpec((1, th, 4 * p), lambda i: (i, 0, 0))],
        out_specs=pl.BlockSpec((1, nc, 4 * p), lambda i: (i, 0, 0)),
        out_shape=jax.ShapeDtypeStruct((g, nc, 4 * p), F32),
        compiler_params=_cparams(("arbitrary",), 32),
        name="s5_chunk_state",
    )(ug, fmat)
    hend_t = hend.transpose(1, 0, 2)
    cb = min(nc, 128)
    nb = nc // cb
    flip = lambda d, c: c + d * (nb - 1 - 2 * c)
    hs_t = pl.pallas_call(
        _s5_scan_kernel,
        grid=(2, nb),
        in_specs=[pl.BlockSpec((cb, g, 2 * p), lambda d, c: (flip(d, c), 0, d)),
                  pl.BlockSpec((1, g, 2 * p), lambda d, c: (d, 0, 0)),
                  pl.BlockSpec((1, g, 2 * p), lambda d, c: (d, 0, 0))],
        out_specs=pl.BlockSpec((cb, g, 2 * p), lambda d, c: (flip(d, c), 0, d)),
        out_shape=jax.ShapeDtypeStruct((nc, g, 4 * p), BF16),
        scratch_shapes=[pltpu.VMEM((g, 2 * p), F32)],
        compiler_params=_cparams(("arbitrary", "arbitrary"), 32),
        name="s5_chunk_scan",
    )(hend_t, a1, a2)
    hs = hs_t.transpose(1, 0, 2)
    yg = pl.pallas_call(
        _s5_out_kernel,
        grid=(g,),
        in_specs=[pl.BlockSpec((1, nc, th), lambda i: (i, 0, 0)),
                  pl.BlockSpec((1, th, th), lambda i: (i, 0, 0)),
                  pl.BlockSpec((1, nc, 4 * p), lambda i: (i, 0, 0)),
                  pl.BlockSpec((1, 4 * p, th), lambda i: (i, 0, 0))],
        out_specs=pl.BlockSpec((1, nc, th), lambda i: (i, 0, 0)),
        out_shape=jax.ShapeDtypeStruct((g, nc, th), BF16),
        compiler_params=_cparams(("arbitrary",), 32),
        name="s5_output",
    )(ug, m, hs, emat)
    return yg.reshape(g, nc, t, hh).transpose(1, 2, 0, 3).reshape(seq, g * hh)


def _glu_kernel(y_ref, wv_ref, wg_ref, bv_ref, bg_ref, s_ref, o_ref):
    y = y_ref[...]
    val = jnp.dot(y, wv_ref[...], preferred_element_type=F32) + bv_ref[...]
    gt = jnp.dot(y, wg_ref[...], preferred_element_type=F32) + bg_ref[...]
    o_ref[...] = (val * _sigmoid(gt) * s_ref[...].astype(F32)).astype(o_ref.dtype)


def _glu(y, w_bf16, b, proj, *, s_col, tm=1024, tn=512):
    seq, kdim = y.shape
    width = w_bf16.shape[1] // 2
    assert s_col % tn == 0
    nj = width // tn
    b2 = b.reshape(1, 2 * width).astype(F32)
    return pl.pallas_call(
        _glu_kernel,
        grid=(seq // tm, nj),
        in_specs=[pl.BlockSpec((tm, kdim), lambda i, j: (i, 0)),
                  pl.BlockSpec((kdim, tn), lambda i, j: (0, j)),
                  pl.BlockSpec((kdim, tn), lambda i, j: (0, nj + j)),
                  pl.BlockSpec((1, tn), lambda i, j: (0, j)),
                  pl.BlockSpec((1, tn), lambda i, j: (0, nj + j)),
                  pl.BlockSpec((tm, tn), lambda i, j: (i, s_col // tn + j))],
        out_specs=pl.BlockSpec((tm, tn), lambda i, j: (i, j)),
        out_shape=jax.ShapeDtypeStruct((seq, width), BF16),
        compiler_params=_cparams(("arbitrary", "arbitrary"), 40),
        name="s5_glu",
    )(y, w_bf16, w_bf16, b2, b2, proj)


def _out_proj_kernel(a_ref, b_ref, wa_ref, wb_ref, x_ref, gate_ref, o_ref):
    acc = jnp.dot(a_ref[...], wa_ref[...], preferred_element_type=F32)
    acc = acc + jnp.dot(b_ref[...], wb_ref[...], preferred_element_type=F32)
    o_ref[...] = x_ref[...] + gate_ref[...] * acc


def _out_proj(oa, ob, w_bf16, x, gate, *, tm=1024, tn=512):
    seq, half = oa.shape
    d = w_bf16.shape[1]
    return pl.pallas_call(
        _out_proj_kernel,
        grid=(seq // tm, d // tn),
        in_specs=[pl.BlockSpec((tm, half), lambda i, j: (i, 0)),
                  pl.BlockSpec((tm, half), lambda i, j: (i, 0)),
                  pl.BlockSpec((half, tn), lambda i, j: (0, j)),
                  pl.BlockSpec((half, tn), lambda i, j: (1, j)),
                  pl.BlockSpec((tm, tn), lambda i, j: (i, j)),
                  pl.BlockSpec((1, tn), lambda i, j: (0, j))],
        out_specs=pl.BlockSpec((tm, tn), lambda i, j: (i, j)),
        out_shape=jax.ShapeDtypeStruct((seq, d), F32),
        compiler_params=_cparams(("arbitrary", "arbitrary"), 40),
        name="out_proj",
    )(oa, ob, w_bf16, w_bf16, x, gate)


def _na_bias_tables(rpb, rows):
    w = GRID_W
    nrb = rows // NA_QROWS
    assert rows >= NA_KROWS + NA_QROWS
    col = np.arange(w)
    col_start = np.clip(col - NA_COLS // 2, 0, w - NA_COLS)
    col_ok = (col[None, :] >= col_start[:, None]) & (col[None, :] < col_start[:, None] + NA_COLS)
    dc = np.clip(col[None, :] - col[:, None], -(NA_COLS - 1), NA_COLS - 1) + NA_COLS - 1
    onehot = (dc[None] == np.arange(2 * NA_COLS - 1)[:, None, None]).astype(np.float32)
    tt = jnp.einsum('hrd,dqk->hrqk', rpb.astype(F32), jnp.asarray(onehot), precision=HIGHEST)
    tt = jnp.where(jnp.asarray(col_ok)[None, None], tt, NEG_INF)
    masked = jnp.full((rpb.shape[0], w, w), NEG_INF, F32)
    tabs = []
    for rb in (0, 1, nrb - 1):
        ks = min(max(rb * NA_QROWS - NA_ROWS // 2, 0), rows - NA_KROWS)
        qrows = []
        for rl in range(NA_QROWS):
            r = rb * NA_QROWS + rl
            rs = min(max(r - NA_ROWS // 2, 0), rows - NA_ROWS)
            blocks = []
            for kl in range(NA_KROWS):
                kr = ks + kl
                blocks.append(tt[:, kr - r + NA_ROWS - 1] if rs <= kr < rs + NA_ROWS else masked)
            qrows.append(jnp.concatenate(blocks, axis=-1))
        tabs.append(jnp.concatenate(qrows, axis=1))
    return jnp.stack(tabs)


def _na_kernel(q_ref, k0_ref, k1_ref, k2_ref, v0_ref, v1_ref, v2_ref, b_ref, g_ref, o_ref):
    q = q_ref[...]
    k = jnp.concatenate([k0_ref[...], k1_ref[...], k2_ref[...]], axis=0)
    v = jnp.concatenate([v0_ref[...], v1_ref[...], v2_ref[...]], axis=0)
    s = lax.dot_general(q, k, (((1,), (1,)), ((), ())), preferred_element_type=F32) + b_ref[0, 0]
    m = jnp.max(s, axis=-1, keepdims=True)
    p = jnp.exp(s - m)
    l = jnp.sum(p, axis=-1, keepdims=True)
    o = jnp.dot(p.astype(BF16), v, preferred_element_type=F32) * (1.0 / l)
    o_ref[...] = (o * g_ref[...].astype(F32)).astype(o_ref.dtype)


def _na_attention(proj, bias, *, q_col, k_col, v_col, g_col):
    seq = proj.shape[0]
    tq = NA_QROWS * GRID_W
    nrb = seq // tq
    nkb = NA_KROWS // NA_QROWS
    hd = HEAD_DIM

    def kv_spec(col, off):
        return pl.BlockSpec(
            (tq, hd), lambda h, rb: (jnp.clip(rb - 1, 0, nrb - nkb) + off, col // hd + h))

    btype = lambda h, rb: (jnp.where(rb == 0, 0, jnp.where(rb == nrb - 1, 2, 1)), h, 0, 0)
    return pl.pallas_call(
        _na_kernel,
        grid=(C_HEADS, nrb),
        in_specs=[pl.BlockSpec((tq, hd), lambda h, rb: (rb, q_col // hd + h)),
                  kv_spec(k_col, 0), kv_spec(k_col, 1), kv_spec(k_col, 2),
                  kv_spec(v_col, 0), kv_spec(v_col, 1), kv_spec(v_col, 2),
                  pl.BlockSpec((1, 1, tq, nkb * tq), btype),
                  pl.BlockSpec((tq, hd), lambda h, rb: (rb, g_col // hd + h))],
        out_specs=pl.BlockSpec((tq, hd), lambda h, rb: (rb, h)),
        out_shape=jax.ShapeDtypeStruct((seq, C_HEADS * hd), BF16),
        compiler_params=_cparams(("arbitrary", "arbitrary"), 32),
        name="na_attention",
    )(proj, proj, proj, proj, proj, proj, proj, bias, proj)


CONV_HALO = 8


def _conv_kernel(prev_ref, cur_ref, next_ref, w_ref, b_ref, o_ref):
    i = pl.program_id(0)
    tm = cur_ref.shape[0]
    prev = jnp.where(i == 0, 0.0, prev_ref[...].astype(F32))
    nxt = jnp.where(i == pl.num_programs(0) - 1, 0.0, next_ref[...].astype(F32))
    ext = jnp.concatenate([prev, cur_ref[...].astype(F32), nxt], axis=0)
    acc = jnp.zeros(cur_ref.shape, F32) + b_ref[...]
    for kk in range(SSD_CONV):
        start = CONV_HALO - SSD_CONV // 2 + kk
        acc = acc + ext[start:start + tm, :] * w_ref[kk:kk + 1, :]
    o_ref[...] = _silu(acc).astype(o_ref.dtype)


def _ssd_conv(proj, conv_w, conv_b, *, col, tm=256, tc=512):
    seq = proj.shape[0]
    ch = conv_w.shape[1]
    assert col % tc == 0 and ch % tc == 0
    nh = tm // CONV_HALO
    nblk = seq // CONV_HALO
    cb = col // tc
    return pl.pallas_call(
        _conv_kernel,
        grid=(seq // tm, ch // tc),
        in_specs=[pl.BlockSpec((CONV_HALO, tc), lambda i, j: (jnp.maximum(i * nh - 1, 0), cb + j)),
                  pl.BlockSpec((tm, tc), lambda i, j: (i, cb + j)),
                  pl.BlockSpec((CONV_HALO, tc),
                               lambda i, j: (jnp.minimum((i + 1) * nh, nblk - 1), cb + j)),
                  pl.BlockSpec((SSD_CONV, tc), lambda i, j: (0, j)),
                  pl.BlockSpec((1, tc), lambda i, j: (0, j))],
        out_specs=pl.BlockSpec((tm, tc), lambda i, j: (i, j)),
        out_shape=jax.ShapeDtypeStruct((seq, ch), BF16),
        compiler_params=_cparams(("arbitrary", "arbitrary"), 32),
        name="ssd_conv",
    )(proj, proj, proj, conv_w.astype(F32), conv_b.reshape(1, ch).astype(F32))


def _split_dot(a, b_bf16):
    hi = a.astype(BF16)
    lo = (a - hi.astype(F32)).astype(BF16)
    return (jnp.dot(hi, b_bf16, preferred_element_type=F32)
            + jnp.dot(lo, b_bf16, preferred_element_type=F32))


def _ssd_scan_kernel(xs_ref, b_ref, c_ref, dt_ref, bias_ref, a_ref, ex_ref, o_ref, st_scr):
    d = pl.program_id(0)
    t = xs_ref.shape[0]
    gw = SSD_WIDTH // SSD_GROUPS
    hpg = SSD_HEADS // SSD_GROUPS

    @pl.when(pl.program_id(1) == 0)
    def _():
        st_scr[...] = jnp.zeros(st_scr.shape, F32)

    row = lax.broadcasted_iota(jnp.int32, (t, t), 0)
    colm = lax.broadcasted_iota(jnp.int32, (t, t), 1)
    tri = (row - colm) * (1 - 2 * d) >= 0
    tri_f = tri.astype(F32)

    z = dt_ref[...] + bias_ref[0]
    dt = jnp.maximum(z, 0.0) + jnp.log(1.0 + jnp.exp(-jnp.abs(z)))
    adt = dt * a_ref[0]
    r = jnp.dot(tri_f, adt, precision=HIGHEST, preferred_element_type=F32)
    tot = jnp.where(d == 0, r[t - 1:t, :], r[0:1, :])
    ex = ex_ref[...]
    dt_x = _split_dot(dt, ex)
    er_x = _split_dot(jnp.exp(r), ex)
    sd_x = _split_dot(jnp.exp(tot - r), ex)
    et_x = _split_dot(jnp.exp(tot), ex)
    r_t = r.T

    xs = xs_ref[...].astype(F32)
    xd = xs * dt_x
    xd_b = xd.astype(BF16)
    xdd_b = (xd * sd_x).astype(BF16)

    y_parts = []
    for g in range(SSD_GROUPS):
        bg = b_ref[:, g * SSD_STATE:(g + 1) * SSD_STATE]
        cg = c_ref[:, g * SSD_STATE:(g + 1) * SSD_STATE]
        cb = lax.dot_general(cg, bg, (((1,), (1,)), ((), ())), preferred_element_type=F32)
        lanes = slice(g * gw, (g + 1) * gw)
        s_prev = st_scr[g]
        y_off = jnp.dot(cg, s_prev.astype(BF16), preferred_element_type=F32) * er_x[:, lanes]
        s_loc = lax.dot_general(bg, xdd_b[:, lanes], (((0,), (0,)), ((), ())),
                                preferred_element_type=F32)
        st_scr[g] = s_prev * et_x[:, lanes] + s_loc
        for hh in range(hpg):
            h = g * hpg + hh
            decay = jnp.exp(jnp.where(tri, r[:, h:h + 1] - r_t[h:h + 1, :], NEG_INF))
            sc = (cb * decay).astype(BF16)
            hl = slice(h * SSD_HEAD_DIM, (h + 1) * SSD_HEAD_DIM)
            y_parts.append(jnp.dot(sc, xd_b[:, hl], preferred_element_type=F32)
                           + y_off[:, hh * SSD_HEAD_DIM:(hh + 1) * SSD_HEAD_DIM])
    o_ref[0] = jnp.concatenate(y_parts, axis=-1).astype(o_ref.dtype)


def _ssd_scan(conv, dt_raw, dt_bias, a_log):
    seq = conv.shape[0]
    t = SSD_CHUNK
    nc = seq // t
    a = -jnp.exp(a_log.astype(F32)).reshape(2, 1, SSD_HEADS)
    bias = dt_bias.astype(F32).reshape(2, 1, SSD_HEADS)
    ex = jnp.repeat(jnp.eye(SSD_HEADS, dtype=BF16), SSD_HEAD_DIM, axis=1)
    dt2 = dt_raw[:, :2 * SSD_HEADS].reshape(seq, 2, SSD_HEADS).transpose(1, 0, 2)
    flip = lambda d, c: c + d * (nc - 1 - 2 * c)
    nxb = SSD_WIDTH // SSD_BC
    return pl.pallas_call(
        _ssd_scan_kernel,
        grid=(2, nc),
        in_specs=[pl.BlockSpec((t, SSD_WIDTH), lambda d, c: (flip(d, c), 0)),
                  pl.BlockSpec((t, SSD_BC), lambda d, c: (flip(d, c), nxb)),
                  pl.BlockSpec((t, SSD_BC), lambda d, c: (flip(d, c), nxb + 1)),
                  pl.BlockSpec((None, t, SSD_HEADS), lambda d, c: (d, flip(d, c), 0)),
                  pl.BlockSpec((1, 1, SSD_HEADS), lambda d, c: (d, 0, 0)),
                  pl.BlockSpec((1, 1, SSD_HEADS), lambda d, c: (d, 0, 0)),
                  pl.BlockSpec((SSD_HEADS, SSD_WIDTH), lambda d, c: (0, 0))],
        out_specs=pl.BlockSpec((1, t, SSD_WIDTH), lambda d, c: (d, flip(d, c), 0)),
        out_shape=jax.ShapeDtypeStruct((2, seq, SSD_WIDTH), BF16),
        scratch_shapes=[pltpu.VMEM((SSD_GROUPS, SSD_STATE, SSD_WIDTH // SSD_GROUPS), F32)],
        compiler_params=_cparams(("arbitrary", "arbitrary"), 32),
        name="ssd_scan",
    )(conv, conv, conv, dt2, bias, a, ex)


def _gated_norm_kernel(y_ref, xs_ref, z_ref, d_ref, w_ref, o_ref):
    y = y_ref[0].astype(F32) + y_ref[1].astype(F32) + d_ref[...] * xs_ref[...].astype(F32)
    y = y * z_ref[...].astype(F32)
    ms = jnp.mean(y * y, axis=-1, keepdims=True)
    o_ref[...] = (y * lax.rsqrt(ms + EPS) * w_ref[...]).astype(o_ref.dtype)


def _gated_norm(y2, conv, proj, d_x, norm_w, *, z_col, tm=512):
    seq = conv.shape[0]
    w = SSD_WIDTH
    assert z_col % w == 0
    return pl.pallas_call(
        _gated_norm_kernel,
        grid=(seq // tm,),
        in_specs=[pl.BlockSpec((2, tm, w), lambda i: (0, i, 0)),
                  pl.BlockSpec((tm, w), lambda i: (i, 0)),
                  pl.BlockSpec((tm, w), lambda i: (i, z_col // w)),
                  pl.BlockSpec((1, w), lambda i: (0, 0)),
                  pl.BlockSpec((1, w), lambda i: (0, 0))],
        out_specs=pl.BlockSpec((tm, w), lambda i: (i, 0)),
        out_shape=jax.ShapeDtypeStruct((seq, w), BF16),
        compiler_params=_cparams(("arbitrary",), 32),
        name="ssd_gated_norm",
    )(y2, conv, proj, d_x, norm_w.reshape(1, w).astype(F32))


def _rope_tables(seq):
    rows = seq // GRID_W
    n_axis = HEAD_DIM // 4
    inv = ROPE_THETA ** (-np.arange(n_axis, dtype=np.float64) / n_axis)
    ang_r = np.arange(rows, dtype=np.float64)[:, None] * inv
    ang_c = np.arange(GRID_W, dtype=np.float64)[:, None] * inv

    def expand(fr, fc):
        fr = jnp.asarray(fr.astype(np.float32))
        fc = jnp.asarray(fc.astype(np.float32))
        tab = jnp.concatenate(
            [jnp.broadcast_to(fr[:, None, :], (rows, GRID_W, n_axis)),
             jnp.broadcast_to(fc[None, :, :], (rows, GRID_W, n_axis))], axis=-1)
        return tab.reshape(seq, 2 * n_axis)

    cos = expand(np.cos(ang_r), np.cos(ang_c))
    sin = expand(np.sin(ang_r), np.sin(ang_c))
    return jnp.concatenate([cos, cos], axis=-1), jnp.concatenate([-sin, sin], axis=-1)


def _deinterleave_perm():
    return np.concatenate([np.arange(0, HEAD_DIM, 2), np.arange(1, HEAD_DIM, 2)])


def _layer_attn_s5(x, c, norm_g, ada_w, ada_b, w_in, q_norm, k_norm, lam_re, lam_im, log_step,
                   b_re, b_im, c_re, c_im, s5_d, w_glu, b_glu, w_out):
    seq, d = x.shape
    shift, scale1p, gate = _ada_mod(c, ada_w, ada_b)
    aw = A_HEADS * HEAD_DIM
    akw = A_KV_HEADS * HEAD_DIM
    perm = _deinterleave_perm()
    nqk = (aw + akw) // HEAD_DIM
    colperm = (np.arange(nqk)[:, None] * HEAD_DIM + perm[None, :]).reshape(-1)
    w = jnp.concatenate([w_in[:, colperm], w_in[:, aw + akw:]], axis=1).astype(BF16)
    cos2, sin2 = _rope_tables(seq)
    t = IN_TN
    q_col, k_col, v_col = 0, aw, aw + akw
    g_col = aw + 2 * akw
    u_col = g_col + aw
    gb_col = u_col + d // 2
    roles = ((q_col // t, k_col // t, "q"), (k_col // t, v_col // t, "k"),
             (v_col // t, g_col // t, "plain"), (g_col // t, u_col // t, "silu"),
             (u_col // t, gb_col // t, "plain"), (gb_col // t, (gb_col + d // 2) // t, "silu"))
    (proj,) = _in_proj(x, norm_g, scale1p, shift, w, q_norm[perm], k_norm[perm], cos2, sin2,
                       roles=roles, rope=True, q_scale=HEAD_DIM ** -0.5 * math.log2(math.e),
                       has_aux=False)
    o_a = _gqa_attention(proj, q_col=q_col, k_col=k_col, v_col=v_col, g_col=g_col)
    tables = _s5_tables(lam_re, lam_im, log_step, b_re, b_im, c_re, c_im, s5_d)
    y = _s5_mixer(proj[:, u_col:gb_col], tables)
    o_b = _glu(y, w_glu.astype(BF16), b_glu, proj, s_col=gb_col)
    return _out_proj(o_a, o_b, w_out.astype(BF16), x, gate)


def _layer_na_ssd(x, c, norm_g, ada_w, ada_b, w_in, q_norm, k_norm, rpb, conv_w, conv_b,
                  dt_bias, a_log, ssd_d, norm_w, w_out):
    seq, d = x.shape
    shift, scale1p, gate = _ada_mod(c, ada_w, ada_b)
    cw = C_HEADS * HEAD_DIM
    n_in = w_in.shape[1]
    t = IN_TN
    n_pad = -(-n_in // t) * t
    w = jnp.pad(w_in, ((0, 0), (0, n_pad - n_in))).astype(BF16)
    q_col, k_col, v_col, g_col, z_col = 0, cw, 2 * cw, 3 * cw, 4 * cw
    xbc_col = z_col + SSD_WIDTH
    dt_col = xbc_col + SSD_WIDTH + 2 * SSD_BC
    roles = ((q_col // t, k_col // t, "q"), (k_col // t, v_col // t, "k"),
             (v_col // t, g_col // t, "plain"), (g_col // t, xbc_col // t, "silu"),
             (xbc_col // t, dt_col // t, "plain"), (dt_col // t, n_pad // t, "aux"))
    dummy = jnp.zeros((seq, HEAD_DIM), F32)
    proj, dt_raw = _in_proj(x, norm_g, scale1p, shift, w, q_norm, k_norm, dummy, dummy,
                            roles=roles, rope=False, q_scale=HEAD_DIM ** -0.5, has_aux=True)
    bias = _na_bias_tables(rpb, seq // GRID_W)
    o_c = _na_attention(proj, bias, q_col=q_col, k_col=k_col, v_col=v_col, g_col=g_col)
    conv = _ssd_conv(proj, conv_w, conv_b, col=xbc_col)
    y2 = _ssd_scan(conv, dt_raw, dt_bias, a_log)
    d_x = jnp.repeat(ssd_d.astype(F32), SSD_HEAD_DIM).reshape(1, SSD_WIDTH)
    o_d = _gated_norm(y2, conv, proj, d_x, norm_w, z_col=z_col)
    return _out_proj(o_c, o_d, w_out.astype(BF16), x, gate)


def kernel(x, c, e_norm_g, e_ada_w, e_ada_b, e_w_in, e_q_norm, e_k_norm, s5_lam_re, s5_lam_im,
           s5_log_step, s5_b_re, s5_b_im, s5_c_re, s5_c_im, s5_d, s5_w_glu, s5_b_glu, e_w_out,
           o_norm_g, o_ada_w, o_ada_b, o_w_in, o_q_norm, o_k_norm, na_rpb, ssd_conv_w, ssd_conv_b,
           ssd_dt_bias, ssd_a_log, ssd_d, ssd_norm_w, o_w_out):
    assert x.shape[0] == 1
    h = x[0]
    h = _layer_attn_s5(h, c, e_norm_g[0], e_ada_w[0], e_ada_b[0], e_w_in[0], e_q_norm[0],
                       e_k_norm[0], s5_lam_re[0], s5_lam_im[0], s5_log_step[0], s5_b_re[0],
                       s5_b_im[0], s5_c_re[0], s5_c_im[0], s5_d[0], s5_w_glu[0], s5_b_glu[0],
                       e_w_out[0])
    h = _layer_na_ssd(h, c, o_norm_g[0], o_ada_w[0], o_ada_b[0], o_w_in[0], o_q_norm[0],
                      o_k_norm[0], na_rpb[0], ssd_conv_w[0], ssd_conv_b[0], ssd_dt_bias[0],
                      ssd_a_log[0], ssd_d[0], ssd_norm_w[0], o_w_out[0])
    return h[None]
```

```python
import functools
import math

import jax
import jax.numpy as jnp
import numpy as np
from jax import lax
from jax.experimental import pallas as pl
from jax.experimental.pallas import tpu as pltpu

F32 = jnp.float32
BF16 = jnp.bfloat16
HIGHEST = lax.Precision.HIGHEST

GRID_W = 64
HEAD_DIM = 128
EPS = 1e-6
NEG_INF = -1e30
ROPE_THETA = 10000.0

A_HEADS = 8
A_KV_HEADS = 2
A_GROUP = A_HEADS // A_KV_HEADS
S5_GROUP = 16
S5_GROUPS = 64
S5_STATE = 64
S5_CHUNK = 32
C_HEADS = 8
NA_ROWS = 8
NA_COLS = 16
NA_QROWS = 4
NA_KROWS = 12
SSD_HEADS = 16
SSD_HEAD_DIM = 64
SSD_GROUPS = 2
SSD_STATE = 128
SSD_CONV = 5
SSD_CHUNK = 128
SSD_WIDTH = SSD_HEADS * SSD_HEAD_DIM
SSD_BC = SSD_GROUPS * SSD_STATE

V7X_VMEM_BYTES = 64 * 1024 * 1024
MiB = 1024 * 1024


def _cparams(semantics, vmem_mib):
    assert vmem_mib * MiB < V7X_VMEM_BYTES
    return pltpu.CompilerParams(dimension_semantics=semantics, vmem_limit_bytes=vmem_mib * MiB)


def _silu(x):
    return x * (1.0 / (1.0 + jnp.exp(-x)))


def _sigmoid(x):
    return 1.0 / (1.0 + jnp.exp(-x))


def _ada_kernel(c_ref, w_ref, b_ref, o_ref):
    c = c_ref[...]
    o_ref[...] = jnp.dot(_silu(c), w_ref[...], precision=HIGHEST,
                         preferred_element_type=F32) + b_ref[...]


def _ada_mod(c, w, b):
    d, n = w.shape
    tn = 512
    c8 = jnp.broadcast_to(c.astype(F32), (8, d))
    out = pl.pallas_call(
        _ada_kernel,
        grid=(n // tn,),
        in_specs=[pl.BlockSpec((8, d), lambda j: (0, 0)),
                  pl.BlockSpec((d, tn), lambda j: (0, j)),
                  pl.BlockSpec((1, tn), lambda j: (0, j))],
        out_specs=pl.BlockSpec((8, tn), lambda j: (0, j)),
        out_shape=jax.ShapeDtypeStruct((8, n), F32),
        compiler_params=_cparams(("arbitrary",), 24),
        name="ada_mod",
    )(c8, w, b.reshape(1, n))
    shift, scale, gate = jnp.split(out[0:1], 3, axis=-1)
    return shift, 1.0 + scale, gate


IN_TN = 256
IN_ROW_CHUNK = 64


def _in_proj_kernel(x_ref, g_ref, sc_ref, sh_ref, w_ref, qn_ref, kn_ref, cos_ref, sin_ref,
                    *out_and_scratch, roles, rope, q_scale, has_aux):
    if has_aux:
        o_ref, aux_ref, h_scr = out_and_scratch
    else:
        o_ref, h_scr = out_and_scratch
        aux_ref = None
    j = pl.program_id(1)
    tm = x_ref.shape[0]

    @pl.when(j == 0)
    def _():
        def body(r, carry):
            rows = pl.ds(pl.multiple_of(r * IN_ROW_CHUNK, IN_ROW_CHUNK), IN_ROW_CHUNK)
            xf = x_ref[rows, :]
            ms = jnp.mean(xf * xf, axis=-1, keepdims=True)
            y = xf * lax.rsqrt(ms + EPS) * g_ref[...]
            h_scr[rows, :] = (y * sc_ref[...] + sh_ref[...]).astype(BF16)
            return carry
        lax.fori_loop(0, tm // IN_ROW_CHUNK, body, 0)

    acc = jnp.dot(h_scr[...], w_ref[...], preferred_element_type=F32)

    def head_norm(a, gain):
        ms = jnp.mean(a * a, axis=-1, keepdims=True)
        return a * lax.rsqrt(ms + EPS) * gain

    def qk_epilogue(gain_ref, scale):
        outs = []
        for h in range(IN_TN // HEAD_DIM):
            a = head_norm(acc[:, h * HEAD_DIM:(h + 1) * HEAD_DIM], gain_ref[...])
            if rope:
                a = a * cos_ref[...] + pltpu.roll(a, HEAD_DIM // 2, axis=1) * sin_ref[...]
            if scale != 1.0:
                a = a * scale
            outs.append(a)
        o_ref[...] = jnp.concatenate(outs, axis=-1).astype(o_ref.dtype)

    for lo, hi, role in roles:
        @pl.when((j >= lo) & (j < hi))
        def _(role=role):
            if role == "q":
                qk_epilogue(qn_ref, q_scale)
            elif role == "k":
                qk_epilogue(kn_ref, 1.0)
            elif role == "silu":
                o_ref[...] = _silu(acc).astype(o_ref.dtype)
            elif role == "plain":
                o_ref[...] = acc.astype(o_ref.dtype)
            elif role == "aux":
                o_ref[...] = acc.astype(o_ref.dtype)
                aux_ref[...] = acc
            else:
                raise ValueError(role)


def _in_proj(x, norm_g, scale1p, shift, w_bf16, q_gain, k_gain, cos2, sin2, *, roles, rope,
             q_scale, has_aux, tm=1024):
    seq, d = x.shape
    n = w_bf16.shape[1]
    assert seq % tm == 0 and n % IN_TN == 0
    assert roles[-1][1] == n // IN_TN
    row = lambda i, j: (i, 0)
    const = lambda i, j: (0, 0)
    out_shape = [jax.ShapeDtypeStruct((seq, n), BF16)]
    out_specs = [pl.BlockSpec((tm, IN_TN), lambda i, j: (i, j))]
    if has_aux:
        out_shape.append(jax.ShapeDtypeStruct((seq, IN_TN), F32))
        out_specs.append(pl.BlockSpec((tm, IN_TN), row))
    kern = functools.partial(_in_proj_kernel, roles=roles, rope=rope, q_scale=q_scale,
                             has_aux=has_aux)
    return pl.pallas_call(
        kern,
        grid=(seq // tm, n // IN_TN),
        in_specs=[pl.BlockSpec((tm, d), row),
                  pl.BlockSpec((1, d), const), pl.BlockSpec((1, d), const),
                  pl.BlockSpec((1, d), const),
                  pl.BlockSpec((d, IN_TN), lambda i, j: (0, j)),
                  pl.BlockSpec((1, HEAD_DIM), const), pl.BlockSpec((1, HEAD_DIM), const),
                  pl.BlockSpec((tm, HEAD_DIM), row), pl.BlockSpec((tm, HEAD_DIM), row)],
        out_specs=out_specs,
        out_shape=out_shape,
        scratch_shapes=[pltpu.VMEM((tm, d), BF16)],
        compiler_params=_cparams(("arbitrary", "arbitrary"), 48),
        name="in_proj_rope" if rope else "in_proj",
    )(x, norm_g.reshape(1, d), scale1p, shift, w_bf16, q_gain.reshape(1, HEAD_DIM),
      k_gain.reshape(1, HEAD_DIM), cos2, sin2)


GQA_TK = 1024


def _gqa_kernel(q_ref, k_ref, v_ref, g_ref, o_ref, acc_scr, m_scr, s_scr):
    tq = q_ref.shape[0]
    nk = k_ref.shape[0] // GQA_TK
    q_all = jnp.concatenate(
        [q_ref[:, h * HEAD_DIM:(h + 1) * HEAD_DIM] for h in range(A_GROUP)], axis=0)
    acc_scr[...] = jnp.zeros(acc_scr.shape, F32)
    m_scr[...] = jnp.full(m_scr.shape, -jnp.inf, F32)
    ones = jnp.ones((GQA_TK, HEAD_DIM), BF16)

    def key_rows(kc):
        return pl.ds(pl.multiple_of(kc * GQA_TK, GQA_TK), GQA_TK)

    def scores(kc):
        return lax.dot_general(q_all, k_ref[key_rows(kc), :], (((1,), (1,)), ((), ())),
                               preferred_element_type=F32)

    def softmax_pv(slot, kc):
        s = s_scr[slot]
        v1 = jnp.concatenate([v_ref[key_rows(kc), :], ones], axis=1)
        m_prev = m_scr[...]
        m_new = jnp.maximum(m_prev, jnp.max(s, axis=-1, keepdims=True))
        alpha = jnp.exp2(m_prev - m_new)
        p = jnp.concatenate(
            [jnp.exp2(s[:, j * HEAD_DIM:(j + 1) * HEAD_DIM] - m_new).astype(BF16)
             for j in range(GQA_TK // HEAD_DIM)], axis=1)
        pv = jnp.dot(p, v1, preferred_element_type=F32)
        acc_scr[...] = jnp.concatenate([alpha, alpha], axis=1) * acc_scr[...] + pv
        m_scr[...] = m_new

    s_scr[0] = scores(0)

    def pair(i, carry):
        kc = 2 * i
        s_scr[1] = scores(kc + 1)
        softmax_pv(0, kc)
        s_scr[0] = scores(kc + 2)
        softmax_pv(1, kc + 1)
        return carry

    lax.fori_loop(0, nk // 2 - 1, pair, 0)
    s_scr[1] = scores(nk - 1)
    softmax_pv(0, nk - 2)
    softmax_pv(1, nk - 1)
    for h in range(A_GROUP):
        cols = slice(h * HEAD_DIM, (h + 1) * HEAD_DIM)
        a = acc_scr[h * tq:(h + 1) * tq, :]
        o = a[:, :HEAD_DIM] * (1.0 / a[:, HEAD_DIM:])
        o_ref[:, cols] = (o * g_ref[:, cols].astype(F32)).astype(o_ref.dtype)


def _gqa_attention(proj, *, q_col, k_col, v_col, g_col, tq=256):
    seq = proj.shape[0]
    gw = A_GROUP * HEAD_DIM
    assert q_col % gw == 0 and g_col % gw == 0 and k_col % HEAD_DIM == 0 and v_col % HEAD_DIM == 0
    assert seq % (2 * GQA_TK) == 0 and seq % tq == 0
    return pl.pallas_call(
        _gqa_kernel,
        grid=(A_KV_HEADS, seq // tq),
        in_specs=[pl.BlockSpec((tq, gw), lambda kh, qi: (qi, q_col // gw + kh)),
                  pl.BlockSpec((seq, HEAD_DIM), lambda kh, qi: (0, k_col // HEAD_DIM + kh)),
                  pl.BlockSpec((seq, HEAD_DIM), lambda kh, qi: (0, v_col // HEAD_DIM + kh)),
                  pl.BlockSpec((tq, gw), lambda kh, qi: (qi, g_col // gw + kh))],
        out_specs=pl.BlockSpec((tq, gw), lambda kh, qi: (qi, kh)),
        out_shape=jax.ShapeDtypeStruct((seq, A_HEADS * HEAD_DIM), BF16),
        scratch_shapes=[pltpu.VMEM((A_GROUP * tq, 2 * HEAD_DIM), F32),
                        pltpu.VMEM((A_GROUP * tq, HEAD_DIM), F32),
                        pltpu.VMEM((2, A_GROUP * tq, GQA_TK), F32)],
        compiler_params=_cparams(("arbitrary", "arbitrary"), 56),
        name="gqa_attention",
    )(proj, proj, proj, proj)


def _s5_tables(lam_re, lam_im, log_step, b_re, b_im, c_re, c_im, s5_d):
    t = S5_CHUNK
    g, p, hh = S5_GROUPS, S5_STATE, S5_GROUP
    lr = lam_re.astype(F32)
    li = lam_im.astype(F32)
    dt = jnp.exp(log_step.astype(F32))[..., None]
    mag = jnp.exp(lr * dt)
    ab_re = mag * jnp.cos(li * dt)
    ab_im = mag * jnp.sin(li * dt)
    den = lr * lr + li * li
    num_re = ab_re - 1.0
    f_re = (num_re * lr + ab_im * li) / den
    f_im = (ab_im * lr - num_re * li) / den
    br = b_re.astype(F32)
    bi = b_im.astype(F32)
    bb_re = f_re[..., None] * br - f_im[..., None] * bi
    bb_im = f_re[..., None] * bi + f_im[..., None] * br
    kk = jnp.arange(t + 1, dtype=F32)[None, None, :, None]
    pmag = jnp.exp(kk * (lr * dt)[:, :, None, :])
    ang = kk * (li * dt)[:, :, None, :]
    pw_re = pmag * jnp.cos(ang)
    pw_im = pmag * jnp.sin(ang)
    bt_re = jnp.swapaxes(bb_re, -1, -2)[:, :, None]
    bt_im = jnp.swapaxes(bb_im, -1, -2)[:, :, None]
    pr = pw_re[:, :, :, None, :]
    pi = pw_im[:, :, :, None, :]
    wt_re = pr * bt_re - pi * bt_im
    wt_im = pr * bt_im + pi * bt_re
    cr = c_re.astype(F32)
    ci = c_im.astype(F32)
    kt = (jnp.einsum('dgkjp,dgip->dgkji', wt_re, cr, precision=HIGHEST)
          - jnp.einsum('dgkjp,dgip->dgkji', wt_im, ci, precision=HIGHEST))
    dmat = s5_d.astype(F32).reshape(g, hh)[:, :, None] * jnp.eye(hh, dtype=F32)[None]
    k0 = kt[0, :, 0] + kt[1, :, 0] + dmat
    kb = kt[1, :, 1:t][:, ::-1]
    kf = kt[0, :, 1:t]
    kcomb = jnp.concatenate([kb, k0[:, None], kf], axis=1)
    kflat = kcomb.transpose(0, 2, 1, 3).reshape(g, hh, (2 * t - 1) * hh)
    m = jnp.stack([kflat[:, :, (t - 1 - tp) * hh:(2 * t - 1 - tp) * hh] for tp in range(t)],
                  axis=1).reshape(g, t * hh, t * hh)
    flat = lambda a: a.reshape(g, t * hh, p)
    fmat = jnp.concatenate([flat(wt_re[0, :, :t][:, ::-1]), flat(wt_re[1, :, :t]),
                            flat(wt_im[0, :, :t][:, ::-1]), flat(wt_im[1, :, :t])], axis=-1)
    cc_re = cr[:, :, None]
    cc_im = ci[:, :, None]
    cl_re = cc_re * pr - cc_im * pi
    cl_im = cc_re * pi + cc_im * pr
    e_t = lambda a: jnp.swapaxes(flat(a), 1, 2)
    emat = jnp.concatenate([e_t(cl_re[0, :, 1:t + 1]), e_t(cl_re[1, :, 1:t + 1][:, ::-1]),
                            -e_t(cl_im[0, :, 1:t + 1]), -e_t(cl_im[1, :, 1:t + 1][:, ::-1])],
                           axis=1)
    a_re = jnp.concatenate([pw_re[0, :, t], pw_re[1, :, t]], axis=-1)
    a_im = jnp.concatenate([pw_im[0, :, t], pw_im[1, :, t]], axis=-1)
    return m.astype(BF16), fmat.astype(BF16), emat.astype(BF16), a_re, a_im


def _s5_state_kernel(u_ref, f_ref, o_ref):
    o_ref[...] = jnp.dot(u_ref[0], f_ref[0], preferred_element_type=F32)


def _s5_fwd_lanes(shape):
    lane = lax.broadcasted_iota(jnp.int32, shape, len(shape) - 1)
    return (lane % (2 * S5_STATE)) < S5_STATE


def _s5_scan_kernel(xf_ref, xb_ref, ar_ref, ai_ref, hf_ref, hb_ref, re_scr, im_scr):
    cb = xf_ref.shape[0]
    w = 2 * S5_STATE

    @pl.when(pl.program_id(0) == 0)
    def _():
        re_scr[...] = jnp.zeros(re_scr.shape, F32)
        im_scr[...] = jnp.zeros(im_scr.shape, F32)

    ar = ar_ref[...]
    ai = ai_ref[...]
    fwd = _s5_fwd_lanes(ar.shape)

    def body(i, carry):
        re, im = carry
        j = cb - 1 - i
        st = jnp.concatenate([re, im], axis=-1).astype(hf_ref.dtype)
        hf_ref[i] = st
        hb_ref[j] = st
        xf = xf_ref[i]
        xb = xb_ref[j]
        x_re = jnp.where(fwd, xf[:, :w], xb[:, :w])
        x_im = jnp.where(fwd, xf[:, w:], xb[:, w:])
        return ar * re - ai * im + x_re, ar * im + ai * re + x_im

    re, im = lax.fori_loop(0, cb, body, (re_scr[...], im_scr[...]), unroll=2)
    re_scr[...] = re
    im_scr[...] = im


def _s5_out_kernel(u_ref, m_ref, hf_ref, hb_ref, e_ref, o_ref):
    y = jnp.dot(u_ref[0], m_ref[0], preferred_element_type=F32)
    h = jnp.where(_s5_fwd_lanes(hf_ref.shape), hf_ref[...], hb_ref[...])
    y = y + jnp.dot(h, e_ref[0], preferred_element_type=F32)
    c0 = math.sqrt(2.0 / math.pi)
    y = 0.5 * y * (1.0 + jnp.tanh(c0 * (y + 0.044715 * (y * y * y))))
    o_ref[0] = y.astype(o_ref.dtype)


def _s5_mixer(u, tables):
    m, fmat, emat, a_re, a_im = tables
    seq = u.shape[0]
    t, g, hh, p = S5_CHUNK, S5_GROUPS, S5_GROUP, S5_STATE
    nc = seq // t
    th = t * hh
    sw = 4 * p
    ug = u.reshape(nc, t, g, hh).transpose(2, 0, 1, 3).reshape(g, nc, th)
    hend = pl.pallas_call(
        _s5_state_kernel,
        grid=(g,),
        in_specs=[pl.BlockSpec((1, nc, th), lambda i: (i, 0, 0)),
                  pl.BlockSpec((1, th, sw), lambda i: (i, 0, 0))],
        out_specs=pl.BlockSpec((nc, sw), lambda i: (0, i)),
        out_shape=jax.ShapeDtypeStruct((nc, g * sw), F32),
        compiler_params=_cparams(("arbitrary",), 32),
        name="s5_chunk_state",
    )(ug, fmat)
    hend3 = hend.reshape(nc, g, sw)
    cb = min(nc, 64)
    nb = nc // cb
    fwd_blk = lambda c: (c, 0, 0)
    bwd_blk = lambda c: (nb - 1 - c, 0, 0)
    hf, hb = pl.pallas_call(
        _s5_scan_kernel,
        grid=(nb,),
        in_specs=[pl.BlockSpec((cb, g, sw), fwd_blk), pl.BlockSpec((cb, g, sw), bwd_blk),
                  pl.BlockSpec((g, 2 * p), lambda c: (0, 0)),
                  pl.BlockSpec((g, 2 * p), lambda c: (0, 0))],
        out_specs=[pl.BlockSpec((cb, g, sw), fwd_blk), pl.BlockSpec((cb, g, sw), bwd_blk)],
        out_shape=[jax.ShapeDtypeStruct((nc, g, sw), BF16)] * 2,
        scratch_shapes=[pltpu.VMEM((g, 2 * p), F32), pltpu.VMEM((g, 2 * p), F32)],
        compiler_params=_cparams(("arbitrary",), 48),
        name="s5_chunk_scan",
    )(hend3, hend3, a_re, a_im)
    yg = pl.pallas_call(
        _s5_out_kernel,
        grid=(g,),
        in_specs=[pl.BlockSpec((1, nc, th), lambda i: (i, 0, 0)),
                  pl.BlockSpec((1, th, th), lambda i: (i, 0, 0)),
                  pl.BlockSpec((nc, sw), lambda i: (0, i)),
                  pl.BlockSpec((nc, sw), lambda i: (0, i)),
                  pl.BlockSpec((1, sw, th), lambda i: (i, 0, 0))],
        out_specs=pl.BlockSpec((1, nc, th), lambda i: (i, 0, 0)),
        out_shape=jax.ShapeDtypeStruct((g, nc, th), BF16),
        compiler_params=_cparams(("arbitrary",), 32),
        name="s5_output",
    )(ug, m, hf.reshape(nc, g * sw), hb.reshape(nc, g * sw), emat)
    return yg.reshape(g, nc, t, hh).transpose(1, 2, 0, 3).reshape(seq, g * hh)


def _glu_kernel(y_ref, wv_ref, wg_ref, bv_ref, bg_ref, s_ref, o_ref):
    y = y_ref[...]
    val = jnp.dot(y, wv_ref[...], preferred_element_type=F32) + bv_ref[...]
    gt = jnp.dot(y, wg_ref[...], preferred_element_type=F32) + bg_ref[...]
    o_ref[...] = (val * _sigmoid(gt) * s_ref[...].astype(F32)).astype(o_ref.dtype)


def _glu(y, w_bf16, b, proj, *, s_col, tm=1024, tn=512):
    seq, kdim = y.shape
    width = w_bf16.shape[1] // 2
    assert s_col % tn == 0
    nj = width // tn
    b2 = b.reshape(1, 2 * width).astype(F32)
    return pl.pallas_call(
        _glu_kernel,
        grid=(seq // tm, nj),
        in_specs=[pl.BlockSpec((tm, kdim), lambda i, j: (i, 0)),
                  pl.BlockSpec((kdim, tn), lambda i, j: (0, j)),
                  pl.BlockSpec((kdim, tn), lambda i, j: (0, nj + j)),
                  pl.BlockSpec((1, tn), lambda i, j: (0, j)),
                  pl.BlockSpec((1, tn), lambda i, j: (0, nj + j)),
                  pl.BlockSpec((tm, tn), lambda i, j: (i, s_col // tn + j))],
        out_specs=pl.BlockSpec((tm, tn), lambda i, j: (i, j)),
        out_shape=jax.ShapeDtypeStruct((seq, width), BF16),
        compiler_params=_cparams(("arbitrary", "arbitrary"), 40),
        name="s5_glu",
    )(y, w_bf16, w_bf16, b2, b2, proj)


def _out_proj_kernel(a_ref, b_ref, wa_ref, wb_ref, x_ref, gate_ref, o_ref):
    acc = jnp.dot(a_ref[...], wa_ref[...], preferred_element_type=F32)
    acc = acc + jnp.dot(b_ref[...], wb_ref[...], preferred_element_type=F32)
    o_ref[...] = x_ref[...] + gate_ref[...] * acc


def _out_proj(oa, ob, w_bf16, x, gate, *, tm=1024, tn=512):
    seq, half = oa.shape
    d = w_bf16.shape[1]
    return pl.pallas_call(
        _out_proj_kernel,
        grid=(seq // tm, d // tn),
        in_specs=[pl.BlockSpec((tm, half), lambda i, j: (i, 0)),
                  pl.BlockSpec((tm, half), lambda i, j: (i, 0)),
                  pl.BlockSpec((half, tn), lambda i, j: (0, j)),
                  pl.BlockSpec((half, tn), lambda i, j: (1, j)),
                  pl.BlockSpec((tm, tn), lambda i, j: (i, j)),
                  pl.BlockSpec((1, tn), lambda i, j: (0, j))],
        out_specs=pl.BlockSpec((tm, tn), lambda i, j: (i, j)),
        out_shape=jax.ShapeDtypeStruct((seq, d), F32),
        compiler_params=_cparams(("arbitrary", "arbitrary"), 40),
        name="out_proj",
    )(oa, ob, w_bf16, w_bf16, x, gate)


def _na_bias_tables(rpb, rows):
    w = GRID_W
    nrb = rows // NA_QROWS
    assert rows >= NA_KROWS + NA_QROWS
    col = np.arange(w)
    col_start = np.clip(col - NA_COLS // 2, 0, w - NA_COLS)
    col_ok = (col[None, :] >= col_start[:, None]) & (col[None, :] < col_start[:, None] + NA_COLS)
    dc = np.clip(col[None, :] - col[:, None], -(NA_COLS - 1), NA_COLS - 1) + NA_COLS - 1
    onehot = (dc[None] == np.arange(2 * NA_COLS - 1)[:, None, None]).astype(np.float32)
    tt = jnp.einsum('hrd,dqk->hrqk', rpb.astype(F32) * math.log2(math.e), jnp.asarray(onehot),
                    precision=HIGHEST)
    tt = jnp.where(jnp.asarray(col_ok)[None, None], tt, NEG_INF)
    n_dr = 2 * NA_ROWS - 1
    tt = jnp.concatenate([tt, jnp.full((rpb.shape[0], 1, w, w), NEG_INF, F32)], axis=1)
    sel = np.zeros((3, NA_QROWS, NA_KROWS, n_dr + 1), np.float32)
    for ti, rb in enumerate((0, 1, nrb - 1)):
        ks = min(max(rb * NA_QROWS - NA_ROWS // 2, 0), rows - NA_KROWS)
        for rl in range(NA_QROWS):
            r = rb * NA_QROWS + rl
            rs = min(max(r - NA_ROWS // 2, 0), rows - NA_ROWS)
            for kl in range(NA_KROWS):
                kr = ks + kl
                sel[ti, rl, kl, kr - r + NA_ROWS - 1 if rs <= kr < rs + NA_ROWS else n_dr] = 1.0
    tab = jnp.einsum('trkd,hdqc->thrqkc', jnp.asarray(sel), tt, precision=HIGHEST)
    return tab.reshape(3, rpb.shape[0], NA_QROWS * w, NA_KROWS * w)


def _na_kernel(q_ref, k0_ref, k1_ref, k2_ref, v0_ref, v1_ref, v2_ref, b_ref, g_ref, o_ref):
    ones = jnp.ones((k0_ref.shape[0] * 3, HEAD_DIM), BF16)
    outs = []
    for h in range(C_HEADS):
        cols = slice(h * HEAD_DIM, (h + 1) * HEAD_DIM)
        k = jnp.concatenate([k0_ref[:, cols], k1_ref[:, cols], k2_ref[:, cols]], axis=0)
        v = jnp.concatenate([v0_ref[:, cols], v1_ref[:, cols], v2_ref[:, cols]], axis=0)
        s = lax.dot_general(q_ref[:, cols], k, (((1,), (1,)), ((), ())),
                            preferred_element_type=F32) + b_ref[0, h]
        m = jnp.max(s, axis=-1, keepdims=True)
        p = jnp.exp2(s - m).astype(BF16)
        pv = jnp.dot(p, jnp.concatenate([v, ones], axis=1), preferred_element_type=F32)
        o = pv[:, :HEAD_DIM] * (1.0 / pv[:, HEAD_DIM:])
        outs.append((o * g_ref[:, cols].astype(F32)).astype(o_ref.dtype))
    o_ref[...] = jnp.concatenate(outs, axis=1)


def _na_attention(proj, bias, *, q_col, k_col, v_col, g_col):
    seq = proj.shape[0]
    tq = NA_QROWS * GRID_W
    nrb = seq // tq
    nkb = NA_KROWS // NA_QROWS
    cw = C_HEADS * HEAD_DIM
    assert q_col % cw == 0 and k_col % cw == 0 and v_col % cw == 0 and g_col % cw == 0

    def kv_spec(col, off):
        return pl.BlockSpec((tq, cw), lambda rb: (jnp.clip(rb - 1, 0, nrb - nkb) + off, col // cw))

    btype = lambda rb: (jnp.where(rb == 0, 0, jnp.where(rb == nrb - 1, 2, 1)), 0, 0, 0)
    return pl.pallas_call(
        _na_kernel,
        grid=(nrb,),
        in_specs=[pl.BlockSpec((tq, cw), lambda rb: (rb, q_col // cw)),
                  kv_spec(k_col, 0), kv_spec(k_col, 1), kv_spec(k_col, 2),
                  kv_spec(v_col, 0), kv_spec(v_col, 1), kv_spec(v_col, 2),
                  pl.BlockSpec((1, C_HEADS, tq, nkb * tq), btype),
                  pl.BlockSpec((tq, cw), lambda rb: (rb, g_col // cw))],
        out_specs=pl.BlockSpec((tq, cw), lambda rb: (rb, 0)),
        out_shape=jax.ShapeDtypeStruct((seq, cw), BF16),
        compiler_params=_cparams(("arbitrary",), 48),
        name="na_attention",
    )(proj, proj, proj, proj, proj, proj, proj, bias, proj)


CONV_HALO = 8


def _conv_kernel(prev_ref, cur_ref, next_ref, w_ref, b_ref, o_ref):
    i = pl.program_id(0)
    tm = cur_ref.shape[0]
    prev = jnp.where(i == 0, 0.0, prev_ref[...].astype(F32))
    nxt = jnp.where(i == pl.num_programs(0) - 1, 0.0, next_ref[...].astype(F32))
    ext = jnp.concatenate([prev, cur_ref[...].astype(F32), nxt], axis=0)
    acc = jnp.zeros(cur_ref.shape, F32) + b_ref[...]
    for kk in range(SSD_CONV):
        start = CONV_HALO - SSD_CONV // 2 + kk
        acc = acc + ext[start:start + tm, :] * w_ref[kk:kk + 1, :]
    o_ref[...] = _silu(acc).astype(o_ref.dtype)


def _ssd_conv(proj, conv_w, conv_b, *, col, tm=256, tc=512):
    seq = proj.shape[0]
    ch = conv_w.shape[1]
    assert col % tc == 0 and ch % tc == 0
    nh = tm // CONV_HALO
    nblk = seq // CONV_HALO
    cb = col // tc
    return pl.pallas_call(
        _conv_kernel,
        grid=(seq // tm, ch // tc),
        in_specs=[pl.BlockSpec((CONV_HALO, tc), lambda i, j: (jnp.maximum(i * nh - 1, 0), cb + j)),
                  pl.BlockSpec((tm, tc), lambda i, j: (i, cb + j)),
                  pl.BlockSpec((CONV_HALO, tc),
                               lambda i, j: (jnp.minimum((i + 1) * nh, nblk - 1), cb + j)),
                  pl.BlockSpec((SSD_CONV, tc), lambda i, j: (0, j)),
                  pl.BlockSpec((1, tc), lambda i, j: (0, j))],
        out_specs=pl.BlockSpec((tm, tc), lambda i, j: (i, j)),
        out_shape=jax.ShapeDtypeStruct((seq, ch), BF16),
        compiler_params=_cparams(("arbitrary", "arbitrary"), 32),
        name="ssd_conv",
    )(proj, proj, proj, conv_w.astype(F32), conv_b.reshape(1, ch).astype(F32))


def _split_dot(a, b_bf16):
    hi = a.astype(BF16)
    lo = (a - hi.astype(F32)).astype(BF16)
    return (jnp.dot(hi, b_bf16, preferred_element_type=F32)
            + jnp.dot(lo, b_bf16, preferred_element_type=F32))


def _ssd_scan_kernel(xs_ref, b_ref, c_ref, dt_ref, bias_ref, a_ref, ex_ref, o_ref, st_scr):
    d = pl.program_id(0)
    t = xs_ref.shape[0]
    gw = SSD_WIDTH // SSD_GROUPS
    hpg = SSD_HEADS // SSD_GROUPS

    @pl.when(pl.program_id(1) == 0)
    def _():
        st_scr[...] = jnp.zeros(st_scr.shape, F32)

    row = lax.broadcasted_iota(jnp.int32, (t, t), 0)
    colm = lax.broadcasted_iota(jnp.int32, (t, t), 1)
    tri = (row - colm) * (1 - 2 * d) >= 0
    tri_f = tri.astype(F32)

    z = dt_ref[...] + bias_ref[0]
    dt = jnp.maximum(z, 0.0) + jnp.log(1.0 + jnp.exp(-jnp.abs(z)))
    adt = dt * a_ref[0]
    r = jnp.dot(tri_f, adt, precision=HIGHEST, preferred_element_type=F32)
    tot = jnp.where(d == 0, r[t - 1:t, :], r[0:1, :])
    ex = ex_ref[...]
    dt_x = _split_dot(dt, ex)
    er_x = _split_dot(jnp.exp(r), ex)
    sd_x = _split_dot(jnp.exp(tot - r), ex)
    et_x = _split_dot(jnp.exp(tot), ex)
    r_t = r.T

    xs = xs_ref[...].astype(F32)
    xd = xs * dt_x
    xd_b = xd.astype(BF16)
    xdd_b = (xd * sd_x).astype(BF16)

    y_parts = []
    for g in range(SSD_GROUPS):
        bg = b_ref[:, g * SSD_STATE:(g + 1) * SSD_STATE]
        cg = c_ref[:, g * SSD_STATE:(g + 1) * SSD_STATE]
        cb = lax.dot_general(cg, bg, (((1,), (1,)), ((), ())), preferred_element_type=F32)
        lanes = slice(g * gw, (g + 1) * gw)
        s_prev = st_scr[g]
        y_off = jnp.dot(cg, s_prev.astype(BF16), preferred_element_type=F32) * er_x[:, lanes]
        s_loc = lax.dot_general(bg, xdd_b[:, lanes], (((0,), (0,)), ((), ())),
                                preferred_element_type=F32)
        st_scr[g] = s_prev * et_x[:, lanes] + s_loc
        for hh in range(hpg):
            h = g * hpg + hh
            decay = jnp.exp(jnp.where(tri, r[:, h:h + 1] - r_t[h:h + 1, :], NEG_INF))
            sc = (cb * decay).astype(BF16)
            hl = slice(h * SSD_HEAD_DIM, (h + 1) * SSD_HEAD_DIM)
            y_parts.append(jnp.dot(sc, xd_b[:, hl], preferred_element_type=F32)
                           + y_off[:, hh * SSD_HEAD_DIM:(hh + 1) * SSD_HEAD_DIM])
    o_ref[0] = jnp.concatenate(y_parts, axis=-1).astype(o_ref.dtype)


def _ssd_scan(conv, dt_raw, dt_bias, a_log):
    seq = conv.shape[0]
    t = SSD_CHUNK
    nc = seq // t
    a = -jnp.exp(a_log.astype(F32)).reshape(2, 1, SSD_HEADS)
    bias = dt_bias.astype(F32).reshape(2, 1, SSD_HEADS)
    ex = jnp.repeat(jnp.eye(SSD_HEADS, dtype=BF16), SSD_HEAD_DIM, axis=1)
    dt2 = dt_raw[:, :2 * SSD_HEADS].reshape(seq, 2, SSD_HEADS).transpose(1, 0, 2)
    flip = lambda d, c: c + d * (nc - 1 - 2 * c)
    nxb = SSD_WIDTH // SSD_BC
    return pl.pallas_call(
        _ssd_scan_kernel,
        grid=(2, nc),
        in_specs=[pl.BlockSpec((t, SSD_WIDTH), lambda d, c: (flip(d, c), 0)),
                  pl.BlockSpec((t, SSD_BC), lambda d, c: (flip(d, c), nxb)),
                  pl.BlockSpec((t, SSD_BC), lambda d, c: (flip(d, c), nxb + 1)),
                  pl.BlockSpec((None, t, SSD_HEADS), lambda d, c: (d, flip(d, c), 0)),
                  pl.BlockSpec((1, 1, SSD_HEADS), lambda d, c: (d, 0, 0)),
                  pl.BlockSpec((1, 1, SSD_HEADS), lambda d, c: (d, 0, 0)),
                  pl.BlockSpec((SSD_HEADS, SSD_WIDTH), lambda d, c: (0, 0))],
        out_specs=pl.BlockSpec((1, t, SSD_WIDTH), lambda d, c: (d, flip(d, c), 0)),
        out_shape=jax.ShapeDtypeStruct((2, seq, SSD_WIDTH), BF16),
        scratch_shapes=[pltpu.VMEM((SSD_GROUPS, SSD_STATE, SSD_WIDTH // SSD_GROUPS), F32)],
        compiler_params=_cparams(("arbitrary", "arbitrary"), 32),
        name="ssd_scan",
    )(conv, conv, conv, dt2, bias, a, ex)


def _gated_norm_kernel(y_ref, xs_ref, z_ref, d_ref, w_ref, o_ref):
    y = y_ref[0].astype(F32) + y_ref[1].astype(F32) + d_ref[...] * xs_ref[...].astype(F32)
    y = y * z_ref[...].astype(F32)
    ms = jnp.mean(y * y, axis=-1, keepdims=True)
    o_ref[...] = (y * lax.rsqrt(ms + EPS) * w_ref[...]).astype(o_ref.dtype)


def _gated_norm(y2, conv, proj, d_x, norm_w, *, z_col, tm=512):
    seq = conv.shape[0]
    w = SSD_WIDTH
    assert z_col % w == 0
    return pl.pallas_call(
        _gated_norm_kernel,
        grid=(seq // tm,),
        in_specs=[pl.BlockSpec((2, tm, w), lambda i: (0, i, 0)),
                  pl.BlockSpec((tm, w), lambda i: (i, 0)),
                  pl.BlockSpec((tm, w), lambda i: (i, z_col // w)),
                  pl.BlockSpec((1, w), lambda i: (0, 0)),
                  pl.BlockSpec((1, w), lambda i: (0, 0))],
        out_specs=pl.BlockSpec((tm, w), lambda i: (i, 0)),
        out_shape=jax.ShapeDtypeStruct((seq, w), BF16),
        compiler_params=_cparams(("arbitrary",), 32),
        name="ssd_gated_norm",
    )(y2, conv, proj, d_x, norm_w.reshape(1, w).astype(F32))


def _rope_tables(seq):
    rows = seq // GRID_W
    n_axis = HEAD_DIM // 4
    inv = ROPE_THETA ** (-np.arange(n_axis, dtype=np.float64) / n_axis)
    ang_r = np.arange(rows, dtype=np.float64)[:, None] * inv
    ang_c = np.arange(GRID_W, dtype=np.float64)[:, None] * inv

    def expand(fr, fc):
        fr = jnp.asarray(fr.astype(np.float32))
        fc = jnp.asarray(fc.astype(np.float32))
        tab = jnp.concatenate(
            [jnp.broadcast_to(fr[:, None, :], (rows, GRID_W, n_axis)),
             jnp.broadcast_to(fc[None, :, :], (rows, GRID_W, n_axis))], axis=-1)
        return tab.reshape(seq, 2 * n_axis)

    cos = expand(np.cos(ang_r), np.cos(ang_c))
    sin = expand(np.sin(ang_r), np.sin(ang_c))
    return jnp.concatenate([cos, cos], axis=-1), jnp.concatenate([-sin, sin], axis=-1)


def _deinterleave_perm():
    return np.concatenate([np.arange(0, HEAD_DIM, 2), np.arange(1, HEAD_DIM, 2)])


def _layer_attn_s5(x, c, norm_g, ada_w, ada_b, w_in, q_norm, k_norm, lam_re, lam_im, log_step,
                   b_re, b_im, c_re, c_im, s5_d, w_glu, b_glu, w_out):
    seq, d = x.shape
    shift, scale1p, gate = _ada_mod(c, ada_w, ada_b)
    aw = A_HEADS * HEAD_DIM
    akw = A_KV_HEADS * HEAD_DIM
    perm = _deinterleave_perm()
    nqk = (aw + akw) // HEAD_DIM
    colperm = (np.arange(nqk)[:, None] * HEAD_DIM + perm[None, :]).reshape(-1)
    w = jnp.concatenate([w_in[:, colperm], w_in[:, aw + akw:]], axis=1).astype(BF16)
    cos2, sin2 = _rope_tables(seq)
    t = IN_TN
    q_col, k_col, v_col = 0, aw, aw + akw
    g_col = aw + 2 * akw
    u_col = g_col + aw
    gb_col = u_col + d // 2
    roles = ((q_col // t, k_col // t, "q"), (k_col // t, v_col // t, "k"),
             (v_col // t, g_col // t, "plain"), (g_col // t, u_col // t, "silu"),
             (u_col // t, gb_col // t, "plain"), (gb_col // t, (gb_col + d // 2) // t, "silu"))
    (proj,) = _in_proj(x, norm_g, scale1p, shift, w, q_norm[perm], k_norm[perm], cos2, sin2,
                       roles=roles, rope=True, q_scale=HEAD_DIM ** -0.5 * math.log2(math.e),
                       has_aux=False)
    o_a = _gqa_attention(proj, q_col=q_col, k_col=k_col, v_col=v_col, g_col=g_col)
    tables = _s5_tables(lam_re, lam_im, log_step, b_re, b_im, c_re, c_im, s5_d)
    y = _s5_mixer(proj[:, u_col:gb_col], tables)
    o_b = _glu(y, w_glu.astype(BF16), b_glu, proj, s_col=gb_col)
    return _out_proj(o_a, o_b, w_out.astype(BF16), x, gate)


def _layer_na_ssd(x, c, norm_g, ada_w, ada_b, w_in, q_norm, k_norm, rpb, conv_w, conv_b,
                  dt_bias, a_log, ssd_d, norm_w, w_out):
    seq, d = x.shape
    shift, scale1p, gate = _ada_mod(c, ada_w, ada_b)
    cw = C_HEADS * HEAD_DIM
    n_in = w_in.shape[1]
    t = IN_TN
    n_pad = -(-n_in // t) * t
    w = jnp.pad(w_in, ((0, 0), (0, n_pad - n_in))).astype(BF16)
    q_col, k_col, v_col, g_col, z_col = 0, cw, 2 * cw, 3 * cw, 4 * cw
    xbc_col = z_col + SSD_WIDTH
    dt_col = xbc_col + SSD_WIDTH + 2 * SSD_BC
    roles = ((q_col // t, k_col // t, "q"), (k_col // t, v_col // t, "k"),
             (v_col // t, g_col // t, "plain"), (g_col // t, xbc_col // t, "silu"),
             (xbc_col // t, dt_col // t, "plain"), (dt_col // t, n_pad // t, "aux"))
    dummy = jnp.zeros((seq, HEAD_DIM), F32)
    proj, dt_raw = _in_proj(x, norm_g, scale1p, shift, w, q_norm, k_norm, dummy, dummy,
                            roles=roles, rope=False, q_scale=HEAD_DIM ** -0.5 * math.log2(math.e),
                            has_aux=True)
    bias = _na_bias_tables(rpb, seq // GRID_W)
    o_c = _na_attention(proj, bias, q_col=q_col, k_col=k_col, v_col=v_col, g_col=g_col)
    conv = _ssd_conv(proj, conv_w, conv_b, col=xbc_col)
    y2 = _ssd_scan(conv, dt_raw, dt_bias, a_log)
    d_x = jnp.repeat(ssd_d.astype(F32), SSD_HEAD_DIM).reshape(1, SSD_WIDTH)
    o_d = _gated_norm(y2, conv, proj, d_x, norm_w, z_col=z_col)
    return _out_proj(o_c, o_d, w_out.astype(BF16), x, gate)


def kernel(x, c, e_norm_g, e_ada_w, e_ada_b, e_w_in, e_q_norm, e_k_norm, s5_lam_re, s5_lam_im,
           s5_log_step, s5_b_re, s5_b_im, s5_c_re, s5_c_im, s5_d, s5_w_glu, s5_b_glu, e_w_out,
           o_norm_g, o_ada_w, o_ada_b, o_w_in, o_q_norm, o_k_norm, na_rpb, ssd_conv_w, ssd_conv_b,
           ssd_dt_bias, ssd_a_log, ssd_d, ssd_norm_w, o_w_out):
    assert x.shape[0] == 1
    h = x[0]
    h = _layer_attn_s5(h, c, e_norm_g[0], e_ada_w[0], e_ada_b[0], e_w_in[0], e_q_norm[0],
                       e_k_norm[0], s5_lam_re[0], s5_lam_im[0], s5_log_step[0], s5_b_re[0],
                       s5_b_im[0], s5_c_re[0], s5_c_im[0], s5_d[0], s5_w_glu[0], s5_b_glu[0],
                       e_w_out[0])
    h = _layer_na_ssd(h, c, o_norm_g[0], o_ada_w[0], o_ada_b[0], o_w_in[0], o_q_norm[0],
                      o_k_norm[0], na_rpb[0], ssd_conv_w[0], ssd_conv_b[0], ssd_dt_bias[0],
                      ssd_a_log[0], ssd_d[0], ssd_norm_w[0], o_w_out[0])
    return h[None]
```

```python
import functools
import math

import jax
import jax.numpy as jnp
import numpy as np
from jax import lax
from jax.experimental import pallas as pl
from jax.experimental.pallas import tpu as pltpu

F32 = jnp.float32
BF16 = jnp.bfloat16
HIGHEST = lax.Precision.HIGHEST

GRID_W = 64
HEAD_DIM = 128
EPS = 1e-6
NEG_INF = -1e30
ROPE_THETA = 10000.0

A_HEADS = 8
A_KV_HEADS = 2
A_GROUP = A_HEADS // A_KV_HEADS
S5_GROUP = 16
S5_GROUPS = 64
S5_STATE = 64
S5_CHUNK = 32
C_HEADS = 8
NA_ROWS = 8
NA_COLS = 16
NA_QROWS = 4
NA_KROWS = 12
SSD_HEADS = 16
SSD_HEAD_DIM = 64
SSD_GROUPS = 2
SSD_STATE = 128
SSD_CONV = 5
SSD_CHUNK = 128
SSD_WIDTH = SSD_HEADS * SSD_HEAD_DIM
SSD_BC = SSD_GROUPS * SSD_STATE

V7X_VMEM_BYTES = 64 * 1024 * 1024
MiB = 1024 * 1024


def _cparams(semantics, vmem_mib):
    assert vmem_mib * MiB < V7X_VMEM_BYTES
    return pltpu.CompilerParams(dimension_semantics=semantics, vmem_limit_bytes=vmem_mib * MiB)


def _silu(x):
    return x * (1.0 / (1.0 + jnp.exp(-x)))


def _sigmoid(x):
    return 1.0 / (1.0 + jnp.exp(-x))


def _ada_kernel(c_ref, w_ref, b_ref, o_ref):
    c = c_ref[...]
    o_ref[...] = jnp.dot(_silu(c), w_ref[...], precision=HIGHEST,
                         preferred_element_type=F32) + b_ref[...]


def _ada_mod(c, w, b):
    d, n = w.shape
    tn = 512
    c8 = jnp.broadcast_to(c.astype(F32), (8, d))
    out = pl.pallas_call(
        _ada_kernel,
        grid=(n // tn,),
        in_specs=[pl.BlockSpec((8, d), lambda j: (0, 0)),
                  pl.BlockSpec((d, tn), lambda j: (0, j)),
                  pl.BlockSpec((1, tn), lambda j: (0, j))],
        out_specs=pl.BlockSpec((8, tn), lambda j: (0, j)),
        out_shape=jax.ShapeDtypeStruct((8, n), F32),
        compiler_params=_cparams(("arbitrary",), 24),
        name="ada_mod",
    )(c8, w, b.reshape(1, n))
    shift, scale, gate = jnp.split(out[0:1], 3, axis=-1)
    return shift, 1.0 + scale, gate


IN_TN = 256
IN_ROW_CHUNK = 64


def _in_proj_kernel(x_ref, g_ref, sc_ref, sh_ref, w_ref, qn_ref, kn_ref, cos_ref, sin_ref,
                    *out_and_scratch, roles, rope, q_scale, has_aux):
    if has_aux:
        o_ref, aux_ref, h_scr = out_and_scratch
    else:
        o_ref, h_scr = out_and_scratch
        aux_ref = None
    j = pl.program_id(1)
    tm = x_ref.shape[0]

    @pl.when(j == 0)
    def _():
        def body(r, carry):
            rows = pl.ds(pl.multiple_of(r * IN_ROW_CHUNK, IN_ROW_CHUNK), IN_ROW_CHUNK)
            xf = x_ref[rows, :]
            ms = jnp.mean(xf * xf, axis=-1, keepdims=True)
            y = xf * lax.rsqrt(ms + EPS) * g_ref[...]
            h_scr[rows, :] = (y * sc_ref[...] + sh_ref[...]).astype(BF16)
            return carry
        lax.fori_loop(0, tm // IN_ROW_CHUNK, body, 0, unroll=2)

    acc = jnp.dot(h_scr[...], w_ref[...], preferred_element_type=F32)

    def head_norm(a, gain):
        ms = jnp.mean(a * a, axis=-1, keepdims=True)
        return a * lax.rsqrt(ms + EPS) * gain

    def qk_epilogue(gain_ref, scale):
        outs = []
        for h in range(IN_TN // HEAD_DIM):
            a = head_norm(acc[:, h * HEAD_DIM:(h + 1) * HEAD_DIM], gain_ref[...])
            if rope:
                a = a * cos_ref[...] + pltpu.roll(a, HEAD_DIM // 2, axis=1) * sin_ref[...]
            if scale != 1.0:
                a = a * scale
            outs.append(a)
        o_ref[...] = jnp.concatenate(outs, axis=-1).astype(o_ref.dtype)

    for lo, hi, role in roles:
        @pl.when((j >= lo) & (j < hi))
        def _(role=role):
            if role == "q":
                qk_epilogue(qn_ref, q_scale)
            elif role == "k":
                qk_epilogue(kn_ref, 1.0)
            elif role == "silu":
                o_ref[...] = _silu(acc).astype(o_ref.dtype)
            elif role == "plain":
                o_ref[...] = acc.astype(o_ref.dtype)
            elif role == "aux":
                o_ref[...] = acc.astype(o_ref.dtype)
                aux_ref[...] = acc
            else:
                raise ValueError(role)


def _in_proj(x, norm_g, scale1p, shift, w_bf16, q_gain, k_gain, cos2, sin2, *, roles, rope,
             q_scale, has_aux, tm=1024):
    seq, d = x.shape
    n = w_bf16.shape[1]
    assert seq % tm == 0 and n % IN_TN == 0
    assert roles[-1][1] == n // IN_TN
    row = lambda i, j: (i, 0)
    const = lambda i, j: (0, 0)
    out_shape = [jax.ShapeDtypeStruct((seq, n), BF16)]
    out_specs = [pl.BlockSpec((tm, IN_TN), lambda i, j: (i, j))]
    if has_aux:
        out_shape.append(jax.ShapeDtypeStruct((seq, IN_TN), F32))
        out_specs.append(pl.BlockSpec((tm, IN_TN), row))
    kern = functools.partial(_in_proj_kernel, roles=roles, rope=rope, q_scale=q_scale,
                             has_aux=has_aux)
    return pl.pallas_call(
        kern,
        grid=(seq // tm, n // IN_TN),
        in_specs=[pl.BlockSpec((tm, d), row),
                  pl.BlockSpec((1, d), const), pl.BlockSpec((1, d), const),
                  pl.BlockSpec((1, d), const),
                  pl.BlockSpec((d, IN_TN), lambda i, j: (0, j)),
                  pl.BlockSpec((1, HEAD_DIM), const), pl.BlockSpec((1, HEAD_DIM), const),
                  pl.BlockSpec((tm, HEAD_DIM), row), pl.BlockSpec((tm, HEAD_DIM), row)],
        out_specs=out_specs,
        out_shape=out_shape,
        scratch_shapes=[pltpu.VMEM((tm, d), BF16)],
        compiler_params=_cparams(("arbitrary", "arbitrary"), 48),
        name="in_proj_rope" if rope else "in_proj",
    )(x, norm_g.reshape(1, d), scale1p, shift, w_bf16, q_gain.reshape(1, HEAD_DIM),
      k_gain.reshape(1, HEAD_DIM), cos2, sin2)


GQA_TK = 1024


def _gqa_kernel(q_ref, k_ref, v_ref, g_ref, o_ref, acc_scr, m_scr, s_scr):
    tq = q_ref.shape[0]
    nk = k_ref.shape[0] // GQA_TK
    q_all = jnp.concatenate(
        [q_ref[:, h * HEAD_DIM:(h + 1) * HEAD_DIM] for h in range(A_GROUP)], axis=0)
    acc_scr[...] = jnp.zeros(acc_scr.shape, F32)
    m_scr[...] = jnp.full(m_scr.shape, -jnp.inf, F32)
    ones = jnp.ones((GQA_TK, HEAD_DIM), BF16)

    def key_rows(kc):
        return pl.ds(pl.multiple_of(kc * GQA_TK, GQA_TK), GQA_TK)

    def scores(kc):
        return lax.dot_general(q_all, k_ref[key_rows(kc), :], (((1,), (1,)), ((), ())),
                               preferred_element_type=F32)

    def softmax_pv(slot, kc):
        s = s_scr[slot]
        v1 = jnp.concatenate([v_ref[key_rows(kc), :], ones], axis=1)
        m_prev = m_scr[...]
        m_new = jnp.maximum(m_prev, jnp.max(s, axis=-1, keepdims=True))
        alpha = jnp.exp2(m_prev - m_new)
        p = jnp.concatenate(
            [jnp.exp2(s[:, j * HEAD_DIM:(j + 1) * HEAD_DIM] - m_new).astype(BF16)
             for j in range(GQA_TK // HEAD_DIM)], axis=1)
        pv = jnp.dot(p, v1, preferred_element_type=F32)
        acc_scr[...] = jnp.concatenate([alpha, alpha], axis=1) * acc_scr[...] + pv
        m_scr[...] = m_new

    s_scr[0] = scores(0)

    def pair(i, carry):
        kc = 2 * i
        s_scr[1] = scores(kc + 1)
        softmax_pv(0, kc)
        s_scr[0] = scores(kc + 2)
        softmax_pv(1, kc + 1)
        return carry

    lax.fori_loop(0, nk // 2 - 1, pair, 0)
    s_scr[1] = scores(nk - 1)
    softmax_pv(0, nk - 2)
    softmax_pv(1, nk - 1)
    for h in range(A_GROUP):
        cols = slice(h * HEAD_DIM, (h + 1) * HEAD_DIM)
        a = acc_scr[h * tq:(h + 1) * tq, :]
        o = a[:, :HEAD_DIM] * (1.0 / a[:, HEAD_DIM:])
        o_ref[:, cols] = (o * g_ref[:, cols].astype(F32)).astype(o_ref.dtype)


def _gqa_attention(proj, *, q_col, k_col, v_col, g_col, tq=256):
    seq = proj.shape[0]
    gw = A_GROUP * HEAD_DIM
    assert q_col % gw == 0 and g_col % gw == 0 and k_col % HEAD_DIM == 0 and v_col % HEAD_DIM == 0
    assert seq % (2 * GQA_TK) == 0 and seq % tq == 0
    return pl.pallas_call(
        _gqa_kernel,
        grid=(A_KV_HEADS, seq // tq),
        in_specs=[pl.BlockSpec((tq, gw), lambda kh, qi: (qi, q_col // gw + kh)),
                  pl.BlockSpec((seq, HEAD_DIM), lambda kh, qi: (0, k_col // HEAD_DIM + kh)),
                  pl.BlockSpec((seq, HEAD_DIM), lambda kh, qi: (0, v_col // HEAD_DIM + kh)),
                  pl.BlockSpec((tq, gw), lambda kh, qi: (qi, g_col // gw + kh))],
        out_specs=pl.BlockSpec((tq, gw), lambda kh, qi: (qi, kh)),
        out_shape=jax.ShapeDtypeStruct((seq, A_HEADS * HEAD_DIM), BF16),
        scratch_shapes=[pltpu.VMEM((A_GROUP * tq, 2 * HEAD_DIM), F32),
                        pltpu.VMEM((A_GROUP * tq, HEAD_DIM), F32),
                        pltpu.VMEM((2, A_GROUP * tq, GQA_TK), F32)],
        compiler_params=_cparams(("arbitrary", "arbitrary"), 56),
        name="gqa_attention",
    )(proj, proj, proj, proj)


def _s5_tables(lam_re, lam_im, log_step, b_re, b_im, c_re, c_im, s5_d):
    t = S5_CHUNK
    g, p, hh = S5_GROUPS, S5_STATE, S5_GROUP
    lr = lam_re.astype(F32)
    li = lam_im.astype(F32)
    dt = jnp.exp(log_step.astype(F32))[..., None]
    mag = jnp.exp(lr * dt)
    ab_re = mag * jnp.cos(li * dt)
    ab_im = mag * jnp.sin(li * dt)
    den = lr * lr + li * li
    num_re = ab_re - 1.0
    f_re = (num_re * lr + ab_im * li) / den
    f_im = (ab_im * lr - num_re * li) / den
    br = b_re.astype(F32)
    bi = b_im.astype(F32)
    bb_re = f_re[..., None] * br - f_im[..., None] * bi
    bb_im = f_re[..., None] * bi + f_im[..., None] * br
    kk = jnp.arange(t + 1, dtype=F32)[None, None, :, None]
    pmag = jnp.exp(kk * (lr * dt)[:, :, None, :])
    ang = kk * (li * dt)[:, :, None, :]
    pw_re = pmag * jnp.cos(ang)
    pw_im = pmag * jnp.sin(ang)
    bt_re = jnp.swapaxes(bb_re, -1, -2)[:, :, None]
    bt_im = jnp.swapaxes(bb_im, -1, -2)[:, :, None]
    pr = pw_re[:, :, :, None, :]
    pi = pw_im[:, :, :, None, :]
    wt_re = pr * bt_re - pi * bt_im
    wt_im = pr * bt_im + pi * bt_re
    cr = c_re.astype(F32)
    ci = c_im.astype(F32)
    cc_re = cr[:, :, None]
    cc_im = ci[:, :, None]
    cl_re = cc_re * pr - cc_im * pi
    cl_im = cc_re * pi + cc_im * pr
    bt_cat = jnp.concatenate([bt_re[:, :, 0], -bt_im[:, :, 0]], axis=-1)
    cl_cat = jnp.concatenate([cl_re, cl_im], axis=-1).reshape(2, g, (t + 1) * hh, 2 * p)
    kj = jnp.einsum('dgjq,dgnq->dgjn', bt_cat, cl_cat, precision=HIGHEST)
    kj = kj.reshape(2, g, hh, t + 1, hh)
    dmat = s5_d.astype(F32).reshape(g, hh)[:, :, None] * jnp.eye(hh, dtype=F32)[None]
    k0 = kj[0, :, :, 0] + kj[1, :, :, 0] + dmat
    kb = kj[1, :, :, 1:t][:, :, ::-1]
    kf = kj[0, :, :, 1:t]
    kflat = jnp.concatenate([kb, k0[:, :, None], kf], axis=2).reshape(g, hh, (2 * t - 1) * hh)
    m = jnp.stack([kflat[:, :, (t - 1 - tp) * hh:(2 * t - 1 - tp) * hh] for tp in range(t)],
                  axis=1).reshape(g, t * hh, t * hh)
    flat = lambda a: a.reshape(g, t * hh, p)
    fmat = jnp.concatenate([flat(wt_re[0, :, :t][:, ::-1]), flat(wt_re[1, :, :t]),
                            flat(wt_im[0, :, :t][:, ::-1]), flat(wt_im[1, :, :t])], axis=-1)
    e_t = lambda a: jnp.swapaxes(flat(a), 1, 2)
    emat = jnp.concatenate([e_t(cl_re[0, :, 1:t + 1]), e_t(cl_re[1, :, 1:t + 1][:, ::-1]),
                            -e_t(cl_im[0, :, 1:t + 1]), -e_t(cl_im[1, :, 1:t + 1][:, ::-1])],
                           axis=1)
    a_re = jnp.concatenate([pw_re[0, :, t], pw_re[1, :, t]], axis=-1)
    a_im = jnp.concatenate([pw_im[0, :, t], pw_im[1, :, t]], axis=-1)
    return m.astype(BF16), fmat.astype(BF16), emat.astype(BF16), a_re, a_im


def _s5_state_kernel(u_ref, f_ref, o_ref):
    o_ref[...] = jnp.dot(u_ref[0], f_ref[0], preferred_element_type=F32)


def _s5_fwd_lanes(shape):
    lane = lax.broadcasted_iota(jnp.int32, shape, len(shape) - 1)
    return (lane % (2 * S5_STATE)) < S5_STATE


def _s5_scan_kernel(xf_ref, xb_ref, ar_ref, ai_ref, hf_ref, hb_ref, re_scr, im_scr):
    cb = xf_ref.shape[0]
    w = 2 * S5_STATE

    @pl.when(pl.program_id(0) == 0)
    def _():
        re_scr[...] = jnp.zeros(re_scr.shape, F32)
        im_scr[...] = jnp.zeros(im_scr.shape, F32)

    ar = ar_ref[...]
    ai = ai_ref[...]
    fwd = _s5_fwd_lanes(ar.shape)

    def body(i, carry):
        re, im = carry
        j = cb - 1 - i
        st = jnp.concatenate([re, im], axis=-1).astype(hf_ref.dtype)
        hf_ref[i] = st
        hb_ref[j] = st
        xf = xf_ref[i]
        xb = xb_ref[j]
        x_re = jnp.where(fwd, xf[:, :w], xb[:, :w])
        x_im = jnp.where(fwd, xf[:, w:], xb[:, w:])
        return ar * re - ai * im + x_re, ar * im + ai * re + x_im

    re, im = lax.fori_loop(0, cb, body, (re_scr[...], im_scr[...]), unroll=2)
    re_scr[...] = re
    im_scr[...] = im


def _s5_out_kernel(u_ref, m_ref, hf_ref, hb_ref, e_ref, o_ref):
    y = jnp.dot(u_ref[0], m_ref[0], preferred_element_type=F32)
    h = jnp.where(_s5_fwd_lanes(hf_ref.shape), hf_ref[...], hb_ref[...])
    y = y + jnp.dot(h, e_ref[0], preferred_element_type=F32)
    c0 = math.sqrt(2.0 / math.pi)
    y = 0.5 * y * (1.0 + jnp.tanh(c0 * (y + 0.044715 * (y * y * y))))
    o_ref[0] = y.astype(o_ref.dtype)


S5_SUPER = 128 // S5_GROUP


def _block_swap_matrix():
    n = S5_SUPER
    idx = np.arange(n * n * S5_GROUP)
    b, a, j = idx // (n * S5_GROUP), (idx // S5_GROUP) % n, idx % S5_GROUP
    sel = np.zeros((idx.size, idx.size), np.float32)
    sel[idx, a * n * S5_GROUP + b * S5_GROUP + j] = 1.0
    return jnp.asarray(sel, BF16)


def _s5_gather_kernel(*refs):
    ins, sel_ref, o_ref = refs[:S5_SUPER], refs[S5_SUPER], refs[S5_SUPER + 1]
    lhs = jnp.concatenate([r[...] for r in ins], axis=1)
    out = jnp.dot(lhs, sel_ref[...], preferred_element_type=F32).astype(o_ref.dtype)
    for a in range(S5_SUPER):
        o_ref[a] = out[:, a * 128:(a + 1) * 128]


def _s5_gather(proj, col):
    seq, n = proj.shape
    t, g = S5_CHUNK, S5_GROUPS
    nc = seq // t
    assert n % 128 == 0 and col % 128 == 0 and t % S5_SUPER == 0
    rows = proj.reshape(nc, t * n)
    nt, ct = n // 128, col // 128

    def spec(b):
        return pl.BlockSpec((nc, 128), lambda sg, q: (0, (S5_SUPER * q + b) * nt + ct + sg))

    return pl.pallas_call(
        _s5_gather_kernel,
        grid=(g // S5_SUPER, t // S5_SUPER),
        in_specs=[spec(b) for b in range(S5_SUPER)] + [
            pl.BlockSpec((1024, 1024), lambda sg, q: (0, 0))],
        out_specs=pl.BlockSpec((S5_SUPER, nc, 128), lambda sg, q: (sg, 0, q)),
        out_shape=jax.ShapeDtypeStruct((g, nc, t * S5_GROUP), BF16),
        compiler_params=_cparams(("arbitrary", "arbitrary"), 32),
        name="s5_gather",
    )(*([rows] * S5_SUPER), _block_swap_matrix())


def _s5_scatter_kernel(y_ref, sel_ref, *outs):
    lhs = jnp.concatenate([y_ref[a] for a in range(S5_SUPER)], axis=1)
    out = jnp.dot(lhs, sel_ref[...], preferred_element_type=F32).astype(outs[0].dtype)
    for b in range(S5_SUPER):
        outs[b][...] = out[:, b * 128:(b + 1) * 128]


def _s5_scatter(yg):
    g, nc, th = yg.shape
    t = S5_CHUNK
    nq = t // S5_SUPER
    width = g * S5_GROUP
    nt = width // 128
    parts = pl.pallas_call(
        _s5_scatter_kernel,
        grid=(g // S5_SUPER, nq),
        in_specs=[pl.BlockSpec((S5_SUPER, nc, 128), lambda sg, q: (sg, 0, q)),
                  pl.BlockSpec((1024, 1024), lambda sg, q: (0, 0))],
        out_specs=[pl.BlockSpec((nc, 128), lambda sg, q: (0, q * nt + sg))] * S5_SUPER,
        out_shape=[jax.ShapeDtypeStruct((nc, nq * width), BF16)] * S5_SUPER,
        compiler_params=_cparams(("arbitrary", "arbitrary"), 32),
        name="s5_scatter",
    )(yg, _block_swap_matrix())
    y = jnp.stack([pb.reshape(nc, nq, width) for pb in parts], axis=2)
    return y.reshape(nc * t, width)


def _s5_mixer(proj, u_col, tables):
    m, fmat, emat, a_re, a_im = tables
    seq = proj.shape[0]
    t, g, hh, p = S5_CHUNK, S5_GROUPS, S5_GROUP, S5_STATE
    nc = seq // t
    th = t * hh
    sw = 4 * p
    ug = _s5_gather(proj, u_col)
    hend = pl.pallas_call(
        _s5_state_kernel,
        grid=(g,),
        in_specs=[pl.BlockSpec((1, nc, th), lambda i: (i, 0, 0)),
                  pl.BlockSpec((1, th, sw), lambda i: (i, 0, 0))],
        out_specs=pl.BlockSpec((nc, sw), lambda i: (0, i)),
        out_shape=jax.ShapeDtypeStruct((nc, g * sw), F32),
        compiler_params=_cparams(("arbitrary",), 32),
        name="s5_chunk_state",
    )(ug, fmat)
    hend3 = hend.reshape(nc, g, sw)
    cb = min(nc, 64)
    nb = nc // cb
    fwd_blk = lambda c: (c, 0, 0)
    bwd_blk = lambda c: (nb - 1 - c, 0, 0)
    hf, hb = pl.pallas_call(
        _s5_scan_kernel,
        grid=(nb,),
        in_specs=[pl.BlockSpec((cb, g, sw), fwd_blk), pl.BlockSpec((cb, g, sw), bwd_blk),
                  pl.BlockSpec((g, 2 * p), lambda c: (0, 0)),
                  pl.BlockSpec((g, 2 * p), lambda c: (0, 0))],
        out_specs=[pl.BlockSpec((cb, g, sw), fwd_blk), pl.BlockSpec((cb, g, sw), bwd_blk)],
        out_shape=[jax.ShapeDtypeStruct((nc, g, sw), BF16)] * 2,
        scratch_shapes=[pltpu.VMEM((g, 2 * p), F32), pltpu.VMEM((g, 2 * p), F32)],
        compiler_params=_cparams(("arbitrary",), 48),
        name="s5_chunk_scan",
    )(hend3, hend3, a_re, a_im)
    yg = pl.pallas_call(
        _s5_out_kernel,
        grid=(g,),
        in_specs=[pl.BlockSpec((1, nc, th), lambda i: (i, 0, 0)),
                  pl.BlockSpec((1, th, th), lambda i: (i, 0, 0)),
                  pl.BlockSpec((nc, sw), lambda i: (0, i)),
                  pl.BlockSpec((nc, sw), lambda i: (0, i)),
                  pl.BlockSpec((1, sw, th), lambda i: (i, 0, 0))],
        out_specs=pl.BlockSpec((1, nc, th), lambda i: (i, 0, 0)),
        out_shape=jax.ShapeDtypeStruct((g, nc, th), BF16),
        compiler_params=_cparams(("arbitrary",), 32),
        name="s5_output",
    )(ug, m, hf.reshape(nc, g * sw), hb.reshape(nc, g * sw), emat)
    return _s5_scatter(yg)


def _glu_kernel(y_ref, wv_ref, wg_ref, bv_ref, bg_ref, s_ref, o_ref):
    y = y_ref[...]
    val = jnp.dot(y, wv_ref[...], preferred_element_type=F32) + bv_ref[...]
    gt = jnp.dot(y, wg_ref[...], preferred_element_type=F32) + bg_ref[...]
    o_ref[...] = (val * _sigmoid(gt) * s_ref[...].astype(F32)).astype(o_ref.dtype)


def _glu(y, w_bf16, b, proj, *, s_col, tm=1024, tn=512):
    seq, kdim = y.shape
    width = w_bf16.shape[1] // 2
    assert s_col % tn == 0
    nj = width // tn
    b2 = b.reshape(1, 2 * width).astype(F32)
    return pl.pallas_call(
        _glu_kernel,
        grid=(seq // tm, nj),
        in_specs=[pl.BlockSpec((tm, kdim), lambda i, j: (i, 0)),
                  pl.BlockSpec((kdim, tn), lambda i, j: (0, j)),
                  pl.BlockSpec((kdim, tn), lambda i, j: (0, nj + j)),
                  pl.BlockSpec((1, tn), lambda i, j: (0, j)),
                  pl.BlockSpec((1, tn), lambda i, j: (0, nj + j)),
                  pl.BlockSpec((tm, tn), lambda i, j: (i, s_col // tn + j))],
        out_specs=pl.BlockSpec((tm, tn), lambda i, j: (i, j)),
        out_shape=jax.ShapeDtypeStruct((seq, width), BF16),
        compiler_params=_cparams(("arbitrary", "arbitrary"), 40),
        name="s5_glu",
    )(y, w_bf16, w_bf16, b2, b2, proj)


def _out_proj_kernel(a_ref, b_ref, wa_ref, wb_ref, x_ref, gate_ref, o_ref):
    acc = jnp.dot(a_ref[...], wa_ref[...], preferred_element_type=F32)
    acc = acc + jnp.dot(b_ref[...], wb_ref[...], preferred_element_type=F32)
    o_ref[...] = x_ref[...] + gate_ref[...] * acc


def _out_proj(oa, ob, w_bf16, x, gate, *, tm=1024, tn=512):
    seq, half = oa.shape
    d = w_bf16.shape[1]
    return pl.pallas_call(
        _out_proj_kernel,
        grid=(seq // tm, d // tn),
        in_specs=[pl.BlockSpec((tm, half), lambda i, j: (i, 0)),
                  pl.BlockSpec((tm, half), lambda i, j: (i, 0)),
                  pl.BlockSpec((half, tn), lambda i, j: (0, j)),
                  pl.BlockSpec((half, tn), lambda i, j: (1, j)),
                  pl.BlockSpec((tm, tn), lambda i, j: (i, j)),
                  pl.BlockSpec((1, tn), lambda i, j: (0, j))],
        out_specs=pl.BlockSpec((tm, tn), lambda i, j: (i, j)),
        out_shape=jax.ShapeDtypeStruct((seq, d), F32),
        compiler_params=_cparams(("arbitrary", "arbitrary"), 40),
        name="out_proj",
    )(oa, ob, w_bf16, w_bf16, x, gate)


def _na_bias_tables(rpb, rows):
    w = GRID_W
    nrb = rows // NA_QROWS
    assert rows >= NA_KROWS + NA_QROWS
    col = np.arange(w)
    col_start = np.clip(col - NA_COLS // 2, 0, w - NA_COLS)
    col_ok = (col[None, :] >= col_start[:, None]) & (col[None, :] < col_start[:, None] + NA_COLS)
    dc = np.clip(col[None, :] - col[:, None], -(NA_COLS - 1), NA_COLS - 1) + NA_COLS - 1
    onehot = (dc[None] == np.arange(2 * NA_COLS - 1)[:, None, None]).astype(np.float32)
    tt = jnp.einsum('hrd,dqk->hrqk', rpb.astype(F32) * math.log2(math.e), jnp.asarray(onehot),
                    precision=HIGHEST)
    tt = jnp.where(jnp.asarray(col_ok)[None, None], tt, NEG_INF)
    n_dr = 2 * NA_ROWS - 1
    tt = jnp.concatenate([tt, jnp.full((rpb.shape[0], 1, w, w), NEG_INF, F32)], axis=1)
    sel = np.zeros((3, NA_QROWS, NA_KROWS, n_dr + 1), np.float32)
    for ti, rb in enumerate((0, 1, nrb - 1)):
        ks = min(max(rb * NA_QROWS - NA_ROWS // 2, 0), rows - NA_KROWS)
        for rl in range(NA_QROWS):
            r = rb * NA_QROWS + rl
            rs = min(max(r - NA_ROWS // 2, 0), rows - NA_ROWS)
            for kl in range(NA_KROWS):
                kr = ks + kl
                sel[ti, rl, kl, kr - r + NA_ROWS - 1 if rs <= kr < rs + NA_ROWS else n_dr] = 1.0
    tab = jnp.einsum('trkd,hdqc->thrqkc', jnp.asarray(sel), tt, precision=HIGHEST)
    return tab.reshape(3, rpb.shape[0], NA_QROWS * w, NA_KROWS * w)


def _na_kernel(q_ref, k0_ref, k1_ref, k2_ref, v0_ref, v1_ref, v2_ref, b_ref, g_ref, o_ref):
    ones = jnp.ones((k0_ref.shape[0] * 3, HEAD_DIM), BF16)
    outs = []
    for h in range(C_HEADS):
        cols = slice(h * HEAD_DIM, (h + 1) * HEAD_DIM)
        k = jnp.concatenate([k0_ref[:, cols], k1_ref[:, cols], k2_ref[:, cols]], axis=0)
        v = jnp.concatenate([v0_ref[:, cols], v1_ref[:, cols], v2_ref[:, cols]], axis=0)
        s = lax.dot_general(q_ref[:, cols], k, (((1,), (1,)), ((), ())),
                            preferred_element_type=F32) + b_ref[0, h]
        m = jnp.max(s, axis=-1, keepdims=True)
        p = jnp.exp2(s - m).astype(BF16)
        pv = jnp.dot(p, jnp.concatenate([v, ones], axis=1), preferred_element_type=F32)
        o = pv[:, :HEAD_DIM] * (1.0 / pv[:, HEAD_DIM:])
        outs.append((o * g_ref[:, cols].astype(F32)).astype(o_ref.dtype))
    o_ref[...] = jnp.concatenate(outs, axis=1)


def _na_attention(proj, bias, *, q_col, k_col, v_col, g_col):
    seq = proj.shape[0]
    tq = NA_QROWS * GRID_W
    nrb = seq // tq
    nkb = NA_KROWS // NA_QROWS
    cw = C_HEADS * HEAD_DIM
    assert q_col % cw == 0 and k_col % cw == 0 and v_col % cw == 0 and g_col % cw == 0

    def kv_spec(col, off):
        return pl.BlockSpec((tq, cw), lambda rb: (jnp.clip(rb - 1, 0, nrb - nkb) + off, col // cw))

    btype = lambda rb: (jnp.where(rb == 0, 0, jnp.where(rb == nrb - 1, 2, 1)), 0, 0, 0)
    return pl.pallas_call(
        _na_kernel,
        grid=(nrb,),
        in_specs=[pl.BlockSpec((tq, cw), lambda rb: (rb, q_col // cw)),
                  kv_spec(k_col, 0), kv_spec(k_col, 1), kv_spec(k_col, 2),
                  kv_spec(v_col, 0), kv_spec(v_col, 1), kv_spec(v_col, 2),
                  pl.BlockSpec((1, C_HEADS, tq, nkb * tq), btype),
                  pl.BlockSpec((tq, cw), lambda rb: (rb, g_col // cw))],
        out_specs=pl.BlockSpec((tq, cw), lambda rb: (rb, 0)),
        out_shape=jax.ShapeDtypeStruct((seq, cw), BF16),
        compiler_params=_cparams(("arbitrary",), 48),
        name="na_attention",
    )(proj, proj, proj, proj, proj, proj, proj, bias, proj)


CONV_HALO = 8


def _conv_kernel(prev_ref, cur_ref, next_ref, w_ref, b_ref, o_ref):
    i = pl.program_id(0)
    tm = cur_ref.shape[0]
    prev = jnp.where(i == 0, 0.0, prev_ref[...].astype(F32))
    nxt = jnp.where(i == pl.num_programs(0) - 1, 0.0, next_ref[...].astype(F32))
    ext = jnp.concatenate([prev, cur_ref[...].astype(F32), nxt], axis=0)
    acc = jnp.zeros(cur_ref.shape, F32) + b_ref[...]
    for kk in range(SSD_CONV):
        start = CONV_HALO - SSD_CONV // 2 + kk
        acc = acc + ext[start:start + tm, :] * w_ref[kk:kk + 1, :]
    o_ref[...] = _silu(acc).astype(o_ref.dtype)


def _ssd_conv(proj, conv_w, conv_b, *, col, tm=256, tc=512):
    seq = proj.shape[0]
    ch = conv_w.shape[1]
    assert col % tc == 0 and ch % tc == 0
    nh = tm // CONV_HALO
    nblk = seq // CONV_HALO
    cb = col // tc
    return pl.pallas_call(
        _conv_kernel,
        grid=(seq // tm, ch // tc),
        in_specs=[pl.BlockSpec((CONV_HALO, tc), lambda i, j: (jnp.maximum(i * nh - 1, 0), cb + j)),
                  pl.BlockSpec((tm, tc), lambda i, j: (i, cb + j)),
                  pl.BlockSpec((CONV_HALO, tc),
                               lambda i, j: (jnp.minimum((i + 1) * nh, nblk - 1), cb + j)),
                  pl.BlockSpec((SSD_CONV, tc), lambda i, j: (0, j)),
                  pl.BlockSpec((1, tc), lambda i, j: (0, j))],
        out_specs=pl.BlockSpec((tm, tc), lambda i, j: (i, j)),
        out_shape=jax.ShapeDtypeStruct((seq, ch), BF16),
        compiler_params=_cparams(("arbitrary", "arbitrary"), 32),
        name="ssd_conv",
    )(proj, proj, proj, conv_w.astype(F32), conv_b.reshape(1, ch).astype(F32))


def _split_dot(a, b_bf16):
    hi = a.astype(BF16)
    lo = (a - hi.astype(F32)).astype(BF16)
    return (jnp.dot(hi, b_bf16, preferred_element_type=F32)
            + jnp.dot(lo, b_bf16, preferred_element_type=F32))


def _expand_heads(a, ex_bf16):
    return jnp.dot(a.astype(BF16), ex_bf16, preferred_element_type=F32)


def _ssd_direction(d, xs_ref, b_ref, c_ref, dt_ref, bias_ref, a_ref, ex, o_ref, st_scr):
    t = xs_ref.shape[0]
    gw = SSD_WIDTH // SSD_GROUPS
    hpg = SSD_HEADS // SSD_GROUPS
    row = lax.broadcasted_iota(jnp.int32, (t, t), 0)
    colm = lax.broadcasted_iota(jnp.int32, (t, t), 1)
    tri = (row >= colm) if d == 0 else (row <= colm)
    tri_b = tri.astype(BF16)

    z = dt_ref[...] + bias_ref[d]
    dt = jnp.maximum(z, 0.0) + jnp.log(1.0 + jnp.exp(-jnp.abs(z)))
    adt = dt * a_ref[d]
    a1 = adt.astype(BF16)
    r1 = adt - a1.astype(F32)
    a2 = r1.astype(BF16)
    a3 = (r1 - a2.astype(F32)).astype(BF16)
    r = (jnp.dot(tri_b, a1, preferred_element_type=F32)
         + jnp.dot(tri_b, a2, preferred_element_type=F32)
         + jnp.dot(tri_b, a3, preferred_element_type=F32))
    tot = r[t - 1:t, :] if d == 0 else r[0:1, :]
    dt_x = _expand_heads(dt, ex)
    er_x = _expand_heads(jnp.exp(r), ex)
    sd_x = _expand_heads(jnp.exp(tot - r), ex)
    et_x = _split_dot(jnp.exp(tot), ex)
    r_t = r.T

    xs = xs_ref[...].astype(F32)
    xd = xs * dt_x
    xd_b = xd.astype(BF16)
    xdd_b = (xd * sd_x).astype(BF16)

    y_parts = []
    for g in range(SSD_GROUPS):
        bg = b_ref[:, g * SSD_STATE:(g + 1) * SSD_STATE]
        cg = c_ref[:, g * SSD_STATE:(g + 1) * SSD_STATE]
        cb = lax.dot_general(cg, bg, (((1,), (1,)), ((), ())), preferred_element_type=F32)
        lanes = slice(g * gw, (g + 1) * gw)
        s_prev = st_scr[d, g]
        y_off = jnp.dot(cg, s_prev.astype(BF16), preferred_element_type=F32) * er_x[:, lanes]
        s_loc = lax.dot_general(bg, xdd_b[:, lanes], (((0,), (0,)), ((), ())),
                                preferred_element_type=F32)
        st_scr[d, g] = s_prev * et_x[:, lanes] + s_loc
        for hh in range(hpg):
            h = g * hpg + hh
            decay = jnp.exp(jnp.where(tri, r[:, h:h + 1] - r_t[h:h + 1, :], NEG_INF))
            sc = (cb * decay).astype(BF16)
            hl = slice(h * SSD_HEAD_DIM, (h + 1) * SSD_HEAD_DIM)
            y_parts.append(jnp.dot(sc, xd_b[:, hl], preferred_element_type=F32)
                           + y_off[:, hh * SSD_HEAD_DIM:(hh + 1) * SSD_HEAD_DIM])
    o_ref[...] = jnp.concatenate(y_parts, axis=-1).astype(o_ref.dtype)


def _ssd_scan_kernel(xf_ref, bf_ref, cf_ref, dtf_ref, xb_ref, bb_ref, cb_ref, dtb_ref,
                     bias_ref, a_ref, ex_ref, of_ref, ob_ref, st_scr):
    @pl.when(pl.program_id(0) == 0)
    def _():
        st_scr[...] = jnp.zeros(st_scr.shape, F32)

    ex = ex_ref[...]
    _ssd_direction(0, xf_ref, bf_ref, cf_ref, dtf_ref, bias_ref, a_ref, ex, of_ref, st_scr)
    _ssd_direction(1, xb_ref, bb_ref, cb_ref, dtb_ref, bias_ref, a_ref, ex, ob_ref, st_scr)


def _ssd_scan(conv, dt_raw, dt_bias, a_log):
    seq = conv.shape[0]
    t = SSD_CHUNK
    nc = seq // t
    a = -jnp.exp(a_log.astype(F32)).reshape(2, 1, SSD_HEADS)
    bias = dt_bias.astype(F32).reshape(2, 1, SSD_HEADS)
    ex = jnp.repeat(jnp.eye(SSD_HEADS, dtype=BF16), SSD_HEAD_DIM, axis=1)
    dt_f = dt_raw[:, :SSD_HEADS]
    dt_b = dt_raw[:, SSD_HEADS:2 * SSD_HEADS]
    nxb = SSD_WIDTH // SSD_BC
    fwd = lambda c: c
    bwd = lambda c: nc - 1 - c

    def chunk_specs(pos):
        return [pl.BlockSpec((t, SSD_WIDTH), lambda c: (pos(c), 0)),
                pl.BlockSpec((t, SSD_BC), lambda c: (pos(c), nxb)),
                pl.BlockSpec((t, SSD_BC), lambda c: (pos(c), nxb + 1)),
                pl.BlockSpec((t, SSD_HEADS), lambda c: (pos(c), 0))]

    const3 = pl.BlockSpec((2, 1, SSD_HEADS), lambda c: (0, 0, 0))
    return pl.pallas_call(
        _ssd_scan_kernel,
        grid=(nc,),
        in_specs=chunk_specs(fwd) + chunk_specs(bwd) + [
            const3, const3, pl.BlockSpec((SSD_HEADS, SSD_WIDTH), lambda c: (0, 0))],
        out_specs=[pl.BlockSpec((t, SSD_WIDTH), lambda c: (fwd(c), 0)),
                   pl.BlockSpec((t, SSD_WIDTH), lambda c: (bwd(c), 0))],
        out_shape=[jax.ShapeDtypeStruct((seq, SSD_WIDTH), BF16)] * 2,
        scratch_shapes=[pltpu.VMEM((2, SSD_GROUPS, SSD_STATE, SSD_WIDTH // SSD_GROUPS), F32)],
        compiler_params=_cparams(("arbitrary",), 40),
        name="ssd_scan",
    )(conv, conv, conv, dt_f, conv, conv, conv, dt_b, bias, a, ex)


def _gated_norm_kernel(yf_ref, yb_ref, xs_ref, z_ref, d_ref, w_ref, o_ref):
    y = (yf_ref[...].astype(F32) + yb_ref[...].astype(F32)
         + d_ref[...] * xs_ref[...].astype(F32))
    y = y * z_ref[...].astype(F32)
    ms = jnp.mean(y * y, axis=-1, keepdims=True)
    o_ref[...] = (y * lax.rsqrt(ms + EPS) * w_ref[...]).astype(o_ref.dtype)


def _gated_norm(y_f, y_b, conv, proj, d_x, norm_w, *, z_col, tm=512):
    seq = conv.shape[0]
    w = SSD_WIDTH
    assert z_col % w == 0
    return pl.pallas_call(
        _gated_norm_kernel,
        grid=(seq // tm,),
        in_specs=[pl.BlockSpec((tm, w), lambda i: (i, 0)),
                  pl.BlockSpec((tm, w), lambda i: (i, 0)),
                  pl.BlockSpec((tm, w), lambda i: (i, 0)),
                  pl.BlockSpec((tm, w), lambda i: (i, z_col // w)),
                  pl.BlockSpec((1, w), lambda i: (0, 0)),
                  pl.BlockSpec((1, w), lambda i: (0, 0))],
        out_specs=pl.BlockSpec((tm, w), lambda i: (i, 0)),
        out_shape=jax.ShapeDtypeStruct((seq, w), BF16),
        compiler_params=_cparams(("arbitrary",), 32),
        name="ssd_gated_norm",
    )(y_f, y_b, conv, proj, d_x, norm_w.reshape(1, w).astype(F32))


def _rope_tables(seq):
    rows = seq // GRID_W
    n_axis = HEAD_DIM // 4
    inv = ROPE_THETA ** (-np.arange(n_axis, dtype=np.float64) / n_axis)
    ang_r = np.arange(rows, dtype=np.float64)[:, None] * inv
    ang_c = np.arange(GRID_W, dtype=np.float64)[:, None] * inv

    def expand(fr, fc):
        fr = jnp.asarray(fr.astype(np.float32))
        fc = jnp.asarray(fc.astype(np.float32))
        tab = jnp.concatenate(
            [jnp.broadcast_to(fr[:, None, :], (rows, GRID_W, n_axis)),
             jnp.broadcast_to(fc[None, :, :], (rows, GRID_W, n_axis))], axis=-1)
        return tab.reshape(seq, 2 * n_axis)

    cos = expand(np.cos(ang_r), np.cos(ang_c))
    sin = expand(np.sin(ang_r), np.sin(ang_c))
    return jnp.concatenate([cos, cos], axis=-1), jnp.concatenate([-sin, sin], axis=-1)


def _deinterleave_perm():
    return np.concatenate([np.arange(0, HEAD_DIM, 2), np.arange(1, HEAD_DIM, 2)])


def _layer_attn_s5(x, c, norm_g, ada_w, ada_b, w_in, q_norm, k_norm, lam_re, lam_im, log_step,
                   b_re, b_im, c_re, c_im, s5_d, w_glu, b_glu, w_out):
    seq, d = x.shape
    shift, scale1p, gate = _ada_mod(c, ada_w, ada_b)
    aw = A_HEADS * HEAD_DIM
    akw = A_KV_HEADS * HEAD_DIM
    perm = _deinterleave_perm()
    nqk = (aw + akw) // HEAD_DIM
    colperm = (np.arange(nqk)[:, None] * HEAD_DIM + perm[None, :]).reshape(-1)
    w = jnp.concatenate([w_in[:, colperm], w_in[:, aw + akw:]], axis=1).astype(BF16)
    cos2, sin2 = _rope_tables(seq)
    t = IN_TN
    q_col, k_col, v_col = 0, aw, aw + akw
    g_col = aw + 2 * akw
    u_col = g_col + aw
    gb_col = u_col + d // 2
    roles = ((q_col // t, k_col // t, "q"), (k_col // t, v_col // t, "k"),
             (v_col // t, g_col // t, "plain"), (g_col // t, u_col // t, "silu"),
             (u_col // t, gb_col // t, "plain"), (gb_col // t, (gb_col + d // 2) // t, "silu"))
    (proj,) = _in_proj(x, norm_g, scale1p, shift, w, q_norm[perm], k_norm[perm], cos2, sin2,
                       roles=roles, rope=True, q_scale=HEAD_DIM ** -0.5 * math.log2(math.e),
                       has_aux=False)
    o_a = _gqa_attention(proj, q_col=q_col, k_col=k_col, v_col=v_col, g_col=g_col)
    tables = _s5_tables(lam_re, lam_im, log_step, b_re, b_im, c_re, c_im, s5_d)
    y = _s5_mixer(proj, u_col, tables)
    o_b = _glu(y, w_glu.astype(BF16), b_glu, proj, s_col=gb_col)
    return _out_proj(o_a, o_b, w_out.astype(BF16), x, gate)


def _layer_na_ssd(x, c, norm_g, ada_w, ada_b, w_in, q_norm, k_norm, rpb, conv_w, conv_b,
                  dt_bias, a_log, ssd_d, norm_w, w_out):
    seq, d = x.shape
    shift, scale1p, gate = _ada_mod(c, ada_w, ada_b)
    cw = C_HEADS * HEAD_DIM
    n_in = w_in.shape[1]
    t = IN_TN
    n_pad = -(-n_in // t) * t
    w = jnp.pad(w_in, ((0, 0), (0, n_pad - n_in))).astype(BF16)
    q_col, k_col, v_col, g_col, z_col = 0, cw, 2 * cw, 3 * cw, 4 * cw
    xbc_col = z_col + SSD_WIDTH
    dt_col = xbc_col + SSD_WIDTH + 2 * SSD_BC
    roles = ((q_col // t, k_col // t, "q"), (k_col // t, v_col // t, "k"),
             (v_col // t, g_col // t, "plain"), (g_col // t, xbc_col // t, "silu"),
             (xbc_col // t, dt_col // t, "plain"), (dt_col // t, n_pad // t, "aux"))
    dummy = jnp.zeros((seq, HEAD_DIM), F32)
    proj, dt_raw = _in_proj(x, norm_g, scale1p, shift, w, q_norm, k_norm, dummy, dummy,
                            roles=roles, rope=False, q_scale=HEAD_DIM ** -0.5 * math.log2(math.e),
                            has_aux=True)
    bias = _na_bias_tables(rpb, seq // GRID_W)
    o_c = _na_attention(proj, bias, q_col=q_col, k_col=k_col, v_col=v_col, g_col=g_col)
    conv = _ssd_conv(proj, conv_w, conv_b, col=xbc_col)
    y_f, y_b = _ssd_scan(conv, dt_raw, dt_bias, a_log)
    d_x = jnp.repeat(ssd_d.astype(F32), SSD_HEAD_DIM).reshape(1, SSD_WIDTH)
    o_d = _gated_norm(y_f, y_b, conv, proj, d_x, norm_w, z_col=z_col)
    return _out_proj(o_c, o_d, w_out.astype(BF16), x, gate)


def kernel(x, c, e_norm_g, e_ada_w, e_ada_b, e_w_in, e_q_norm, e_k_norm, s5_lam_re, s5_lam_im,
           s5_log_step, s5_b_re, s5_b_im, s5_c_re, s5_c_im, s5_d, s5_w_glu, s5_b_glu, e_w_out,
           o_norm_g, o_ada_w, o_ada_b, o_w_in, o_q_norm, o_k_norm, na_rpb, ssd_conv_w, ssd_conv_b,
           ssd_dt_bias, ssd_a_log, ssd_d, ssd_norm_w, o_w_out):
    assert x.shape[0] == 1
    h = x[0]
    h = _layer_attn_s5(h, c, e_norm_g[0], e_ada_w[0], e_ada_b[0], e_w_in[0], e_q_norm[0],
                       e_k_norm[0], s5_lam_re[0], s5_lam_im[0], s5_log_step[0], s5_b_re[0],
                       s5_b_im[0], s5_c_re[0], s5_c_im[0], s5_d[0], s5_w_glu[0], s5_b_glu[0],
                       e_w_out[0])
    h = _layer_na_ssd(h, c, o_norm_g[0], o_ada_w[0], o_ada_b[0], o_w_in[0], o_q_norm[0],
                      o_k_norm[0], na_rpb[0], ssd_conv_w[0], ssd_conv_b[0], ssd_dt_bias[0],
                      ssd_a_log[0], ssd_d[0], ssd_norm_w[0], o_w_out[0])
    return h[None]
```

```python
import functools
import math

import jax
import jax.numpy as jnp
import numpy as np
from jax import lax
from jax.experimental import pallas as pl
from jax.experimental.pallas import tpu as pltpu

F32 = jnp.float32
BF16 = jnp.bfloat16
HIGHEST = lax.Precision.HIGHEST

GRID_W = 64
HEAD_DIM = 128
EPS = 1e-6
NEG_INF = -1e30
ROPE_THETA = 10000.0

A_HEADS = 8
A_KV_HEADS = 2
A_GROUP = A_HEADS // A_KV_HEADS
S5_GROUP = 16
S5_GROUPS = 64
S5_STATE = 64
S5_CHUNK = 32
C_HEADS = 8
NA_ROWS = 8
NA_COLS = 16
NA_QROWS = 4
NA_KROWS = 12
SSD_HEADS = 16
SSD_HEAD_DIM = 64
SSD_GROUPS = 2
SSD_STATE = 128
SSD_CONV = 5
SSD_CHUNK = 128
SSD_WIDTH = SSD_HEADS * SSD_HEAD_DIM
SSD_BC = SSD_GROUPS * SSD_STATE

V7X_VMEM_BYTES = 64 * 1024 * 1024
MiB = 1024 * 1024


def _cparams(semantics, vmem_mib):
    assert vmem_mib * MiB < V7X_VMEM_BYTES
    return pltpu.CompilerParams(dimension_semantics=semantics, vmem_limit_bytes=vmem_mib * MiB)


def _silu(x):
    return x * (1.0 / (1.0 + jnp.exp(-x)))


def _sigmoid(x):
    return 1.0 / (1.0 + jnp.exp(-x))


def _ada_kernel(c_ref, w_ref, b_ref, o_ref):
    c = c_ref[...]
    o_ref[...] = jnp.dot(_silu(c), w_ref[...], precision=HIGHEST,
                         preferred_element_type=F32) + b_ref[...]


def _ada_mod(c, w, b):
    d, n = w.shape
    tn = 512
    c8 = jnp.broadcast_to(c.astype(F32), (8, d))
    out = pl.pallas_call(
        _ada_kernel,
        grid=(n // tn,),
        in_specs=[pl.BlockSpec((8, d), lambda j: (0, 0)),
                  pl.BlockSpec((d, tn), lambda j: (0, j)),
                  pl.BlockSpec((1, tn), lambda j: (0, j))],
        out_specs=pl.BlockSpec((8, tn), lambda j: (0, j)),
        out_shape=jax.ShapeDtypeStruct((8, n), F32),
        compiler_params=_cparams(("arbitrary",), 24),
        name="ada_mod",
    )(c8, w, b.reshape(1, n))
    shift, scale, gate = jnp.split(out[0:1], 3, axis=-1)
    return shift, 1.0 + scale, gate


IN_TN = 256
IN_ROW_CHUNK = 64


def _in_proj_kernel(x_ref, g_ref, sc_ref, sh_ref, w_ref, qn_ref, kn_ref, cos_ref, sin_ref,
                    *out_and_scratch, roles, rope, q_scale, has_aux):
    if has_aux:
        o_ref, aux_ref, h_scr = out_and_scratch
    else:
        o_ref, h_scr = out_and_scratch
        aux_ref = None
    j = pl.program_id(1)
    tm = x_ref.shape[0]

    @pl.when(j == 0)
    def _():
        def body(r, carry):
            rows = pl.ds(pl.multiple_of(r * IN_ROW_CHUNK, IN_ROW_CHUNK), IN_ROW_CHUNK)
            xf = x_ref[rows, :]
            ms = jnp.mean(xf * xf, axis=-1, keepdims=True)
            y = xf * lax.rsqrt(ms + EPS) * g_ref[...]
            h_scr[rows, :] = (y * sc_ref[...] + sh_ref[...]).astype(BF16)
            return carry
        lax.fori_loop(0, tm // IN_ROW_CHUNK, body, 0, unroll=2)

    acc = jnp.dot(h_scr[...], w_ref[...], preferred_element_type=F32)

    def head_norm(a, gain):
        ms = jnp.mean(a * a, axis=-1, keepdims=True)
        return a * lax.rsqrt(ms + EPS) * gain

    def qk_epilogue(gain_ref, scale):
        outs = []
        for h in range(IN_TN // HEAD_DIM):
            a = head_norm(acc[:, h * HEAD_DIM:(h + 1) * HEAD_DIM], gain_ref[...])
            if rope:
                a = a * cos_ref[...] + pltpu.roll(a, HEAD_DIM // 2, axis=1) * sin_ref[...]
            if scale != 1.0:
                a = a * scale
            outs.append(a)
        o_ref[...] = jnp.concatenate(outs, axis=-1).astype(o_ref.dtype)

    for lo, hi, role in roles:
        @pl.when((j >= lo) & (j < hi))
        def _(role=role):
            if role == "q":
                qk_epilogue(qn_ref, q_scale)
            elif role == "k":
                qk_epilogue(kn_ref, 1.0)
            elif role == "silu":
                o_ref[...] = _silu(acc).astype(o_ref.dtype)
            elif role == "plain":
                o_ref[...] = acc.astype(o_ref.dtype)
            elif role == "aux":
                o_ref[...] = acc.astype(o_ref.dtype)
                aux_ref[...] = acc
            else:
                raise ValueError(role)


def _in_proj(x, norm_g, scale1p, shift, w_bf16, q_gain, k_gain, cos2, sin2, *, roles, rope,
             q_scale, has_aux, tm=1024):
    seq, d = x.shape
    n = w_bf16.shape[1]
    assert seq % tm == 0 and n % IN_TN == 0
    assert roles[-1][1] == n // IN_TN
    row = lambda i, j: (i, 0)
    const = lambda i, j: (0, 0)
    out_shape = [jax.ShapeDtypeStruct((seq, n), BF16)]
    out_specs = [pl.BlockSpec((tm, IN_TN), lambda i, j: (i, j))]
    if has_aux:
        out_shape.append(jax.ShapeDtypeStruct((seq, IN_TN), F32))
        out_specs.append(pl.BlockSpec((tm, IN_TN), row))
    kern = functools.partial(_in_proj_kernel, roles=roles, rope=rope, q_scale=q_scale,
                             has_aux=has_aux)
    return pl.pallas_call(
        kern,
        grid=(seq // tm, n // IN_TN),
        in_specs=[pl.BlockSpec((tm, d), row),
                  pl.BlockSpec((1, d), const), pl.BlockSpec((1, d), const),
                  pl.BlockSpec((1, d), const),
                  pl.BlockSpec((d, IN_TN), lambda i, j: (0, j)),
                  pl.BlockSpec((1, HEAD_DIM), const), pl.BlockSpec((1, HEAD_DIM), const),
                  pl.BlockSpec((tm, HEAD_DIM), row), pl.BlockSpec((tm, HEAD_DIM), row)],
        out_specs=out_specs,
        out_shape=out_shape,
        scratch_shapes=[pltpu.VMEM((tm, d), BF16)],
        compiler_params=_cparams(("arbitrary", "arbitrary"), 48),
        name="in_proj_rope" if rope else "in_proj",
    )(x, norm_g.reshape(1, d), scale1p, shift, w_bf16, q_gain.reshape(1, HEAD_DIM),
      k_gain.reshape(1, HEAD_DIM), cos2, sin2)


GQA_TK = 1024


def _gqa_kernel(q_ref, k_ref, v_ref, g_ref, o_ref, acc_scr, m_scr, s_scr):
    tq = q_ref.shape[0]
    nk = k_ref.shape[0] // GQA_TK
    q_all = jnp.concatenate(
        [q_ref[:, h * HEAD_DIM:(h + 1) * HEAD_DIM] for h in range(A_GROUP)], axis=0)
    acc_scr[...] = jnp.zeros(acc_scr.shape, F32)
    m_scr[...] = jnp.full(m_scr.shape, -jnp.inf, F32)
    ones = jnp.ones((GQA_TK, HEAD_DIM), BF16)

    def key_rows(kc):
        return pl.ds(pl.multiple_of(kc * GQA_TK, GQA_TK), GQA_TK)

    def scores(kc):
        return lax.dot_general(q_all, k_ref[key_rows(kc), :], (((1,), (1,)), ((), ())),
                               preferred_element_type=F32)

    def softmax_pv(slot, kc):
        s = s_scr[slot]
        v1 = jnp.concatenate([v_ref[key_rows(kc), :], ones], axis=1)
        m_prev = m_scr[...]
        m_new = jnp.maximum(m_prev, jnp.max(s, axis=-1, keepdims=True))
        alpha = jnp.exp2(m_prev - m_new)
        p = jnp.concatenate(
            [jnp.exp2(s[:, j * HEAD_DIM:(j + 1) * HEAD_DIM] - m_new).astype(BF16)
             for j in range(GQA_TK // HEAD_DIM)], axis=1)
        pv = jnp.dot(p, v1, preferred_element_type=F32)
        acc_scr[...] = jnp.concatenate([alpha, alpha], axis=1) * acc_scr[...] + pv
        m_scr[...] = m_new

    s_scr[0] = scores(0)

    def pair(i, carry):
        kc = 2 * i
        s_scr[1] = scores(kc + 1)
        softmax_pv(0, kc)
        s_scr[0] = scores(kc + 2)
        softmax_pv(1, kc + 1)
        return carry

    lax.fori_loop(0, nk // 2 - 1, pair, 0)
    s_scr[1] = scores(nk - 1)
    softmax_pv(0, nk - 2)
    softmax_pv(1, nk - 1)
    for h in range(A_GROUP):
        cols = slice(h * HEAD_DIM, (h + 1) * HEAD_DIM)
        a = acc_scr[h * tq:(h + 1) * tq, :]
        o = a[:, :HEAD_DIM] * (1.0 / a[:, HEAD_DIM:])
        o_ref[:, cols] = (o * g_ref[:, cols].astype(F32)).astype(o_ref.dtype)


def _gqa_attention(proj, *, q_col, k_col, v_col, g_col, tq=256):
    seq = proj.shape[0]
    gw = A_GROUP * HEAD_DIM
    assert q_col % gw == 0 and g_col % gw == 0 and k_col % HEAD_DIM == 0 and v_col % HEAD_DIM == 0
    assert seq % (2 * GQA_TK) == 0 and seq % tq == 0
    return pl.pallas_call(
        _gqa_kernel,
        grid=(A_KV_HEADS, seq // tq),
        in_specs=[pl.BlockSpec((tq, gw), lambda kh, qi: (qi, q_col // gw + kh)),
                  pl.BlockSpec((seq, HEAD_DIM), lambda kh, qi: (0, k_col // HEAD_DIM + kh)),
                  pl.BlockSpec((seq, HEAD_DIM), lambda kh, qi: (0, v_col // HEAD_DIM + kh)),
                  pl.BlockSpec((tq, gw), lambda kh, qi: (qi, g_col // gw + kh))],
        out_specs=pl.BlockSpec((tq, gw), lambda kh, qi: (qi, kh)),
        out_shape=jax.ShapeDtypeStruct((seq, A_HEADS * HEAD_DIM), BF16),
        scratch_shapes=[pltpu.VMEM((A_GROUP * tq, 2 * HEAD_DIM), F32),
                        pltpu.VMEM((A_GROUP * tq, HEAD_DIM), F32),
                        pltpu.VMEM((2, A_GROUP * tq, GQA_TK), F32)],
        compiler_params=_cparams(("arbitrary", "arbitrary"), 56),
        name="gqa_attention",
    )(proj, proj, proj, proj)


def _s5_tables(lam_re, lam_im, log_step, b_re, b_im, c_re, c_im, s5_d):
    t = S5_CHUNK
    g, p, hh = S5_GROUPS, S5_STATE, S5_GROUP
    lr = lam_re.astype(F32)
    li = lam_im.astype(F32)
    dt = jnp.exp(log_step.astype(F32))[..., None]
    mag = jnp.exp(lr * dt)
    ab_re = mag * jnp.cos(li * dt)
    ab_im = mag * jnp.sin(li * dt)
    den = lr * lr + li * li
    num_re = ab_re - 1.0
    f_re = (num_re * lr + ab_im * li) / den
    f_im = (ab_im * lr - num_re * li) / den
    br = b_re.astype(F32)
    bi = b_im.astype(F32)
    bb_re = f_re[..., None] * br - f_im[..., None] * bi
    bb_im = f_re[..., None] * bi + f_im[..., None] * br
    kk = jnp.arange(t + 1, dtype=F32)[None, None, :, None]
    pmag = jnp.exp(kk * (lr * dt)[:, :, None, :])
    ang = kk * (li * dt)[:, :, None, :]
    pw_re = pmag * jnp.cos(ang)
    pw_im = pmag * jnp.sin(ang)
    bt_re = jnp.swapaxes(bb_re, -1, -2)[:, :, None]
    bt_im = jnp.swapaxes(bb_im, -1, -2)[:, :, None]
    pr = pw_re[:, :, :, None, :]
    pi = pw_im[:, :, :, None, :]
    wt_re = pr * bt_re - pi * bt_im
    wt_im = pr * bt_im + pi * bt_re
    cr = c_re.astype(F32)
    ci = c_im.astype(F32)
    cc_re = cr[:, :, None]
    cc_im = ci[:, :, None]
    cl_re = cc_re * pr - cc_im * pi
    cl_im = cc_re * pi + cc_im * pr
    bt_cat = jnp.concatenate([bt_re[:, :, 0], -bt_im[:, :, 0]], axis=-1)
    cl_cat = jnp.concatenate([cl_re, cl_im], axis=-1).reshape(2, g, (t + 1) * hh, 2 * p)
    kj = jnp.einsum('dgjq,dgnq->dgjn', bt_cat, cl_cat, precision=HIGHEST)
    kj = kj.reshape(2, g, hh, t + 1, hh)
    dmat = s5_d.astype(F32).reshape(g, hh)[:, :, None] * jnp.eye(hh, dtype=F32)[None]
    k0 = kj[0, :, :, 0] + kj[1, :, :, 0] + dmat
    kb = kj[1, :, :, 1:t][:, :, ::-1]
    kf = kj[0, :, :, 1:t]
    kflat = jnp.concatenate([kb, k0[:, :, None], kf], axis=2).reshape(g, hh, (2 * t - 1) * hh)
    m = jnp.stack([kflat[:, :, (t - 1 - tp) * hh:(2 * t - 1 - tp) * hh] for tp in range(t)],
                  axis=1).reshape(g, t * hh, t * hh)
    flat = lambda a: a.reshape(g, t * hh, p)
    fmat = jnp.concatenate([flat(wt_re[0, :, :t][:, ::-1]), flat(wt_re[1, :, :t]),
                            flat(wt_im[0, :, :t][:, ::-1]), flat(wt_im[1, :, :t])], axis=-1)
    e_t = lambda a: jnp.swapaxes(flat(a), 1, 2)
    emat = jnp.concatenate([e_t(cl_re[0, :, 1:t + 1]), e_t(cl_re[1, :, 1:t + 1][:, ::-1]),
                            -e_t(cl_im[0, :, 1:t + 1]), -e_t(cl_im[1, :, 1:t + 1][:, ::-1])],
                           axis=1)
    a_re = jnp.concatenate([pw_re[0, :, t], pw_re[1, :, t]], axis=-1)
    a_im = jnp.concatenate([pw_im[0, :, t], pw_im[1, :, t]], axis=-1)
    return m.astype(BF16), fmat.astype(BF16), emat.astype(BF16), a_re, a_im


def _s5_state_kernel(u_ref, f_ref, o_ref):
    o_ref[...] = jnp.dot(u_ref[0], f_ref[0], preferred_element_type=F32)


def _s5_fwd_lanes(shape):
    lane = lax.broadcasted_iota(jnp.int32, shape, len(shape) - 1)
    return (lane % (2 * S5_STATE)) < S5_STATE


def _s5_scan_kernel(xf_ref, xb_ref, ar_ref, ai_ref, hf_ref, hb_ref, re_scr, im_scr):
    cb = xf_ref.shape[0]
    w = 2 * S5_STATE

    @pl.when(pl.program_id(0) == 0)
    def _():
        re_scr[...] = jnp.zeros(re_scr.shape, F32)
        im_scr[...] = jnp.zeros(im_scr.shape, F32)

    ar = ar_ref[...]
    ai = ai_ref[...]
    fwd = _s5_fwd_lanes(ar.shape)

    def body(i, carry):
        re, im = carry
        j = cb - 1 - i
        st = jnp.concatenate([re, im], axis=-1).astype(hf_ref.dtype)
        hf_ref[i] = st
        hb_ref[j] = st
        xf = xf_ref[i]
        xb = xb_ref[j]
        x_re = jnp.where(fwd, xf[:, :w], xb[:, :w])
        x_im = jnp.where(fwd, xf[:, w:], xb[:, w:])
        return ar * re - ai * im + x_re, ar * im + ai * re + x_im

    re, im = lax.fori_loop(0, cb, body, (re_scr[...], im_scr[...]), unroll=2)
    re_scr[...] = re
    im_scr[...] = im


def _s5_out_kernel(u_ref, m_ref, hf_ref, hb_ref, e_ref, o_ref):
    y = jnp.dot(u_ref[0], m_ref[0], preferred_element_type=F32)
    h = jnp.where(_s5_fwd_lanes(hf_ref.shape), hf_ref[...], hb_ref[...])
    y = y + jnp.dot(h, e_ref[0], preferred_element_type=F32)
    c0 = math.sqrt(2.0 / math.pi)
    y = 0.5 * y * (1.0 + jnp.tanh(c0 * (y + 0.044715 * (y * y * y))))
    o_ref[0] = y.astype(o_ref.dtype)


S5_SUPER = 128 // S5_GROUP


def _block_swap_matrix():
    n = S5_SUPER
    idx = np.arange(n * n * S5_GROUP)
    b, a, j = idx // (n * S5_GROUP), (idx // S5_GROUP) % n, idx % S5_GROUP
    sel = np.zeros((idx.size, idx.size), np.float32)
    sel[idx, a * n * S5_GROUP + b * S5_GROUP + j] = 1.0
    return jnp.asarray(sel, BF16)


S5_TBLK = 16


def _s5_gather_kernel(x_ref, sel_ref, o_ref, rows_scr):
    nc = x_ref.shape[0]
    rows_scr[...] = x_ref[...].astype(F32).reshape(nc * S5_TBLK, 128)
    for q in range(S5_TBLK // S5_SUPER):
        lhs = jnp.concatenate(
            [rows_scr[pl.ds(S5_SUPER * q + b, nc, stride=S5_TBLK), :].astype(BF16)
             for b in range(S5_SUPER)], axis=1)
        out = jnp.dot(lhs, sel_ref[...], preferred_element_type=F32).astype(o_ref.dtype)
        for a in range(S5_SUPER):
            o_ref[a, :, q * 128:(q + 1) * 128] = out[:, a * 128:(a + 1) * 128]


def _s5_gather(proj, col):
    seq, n = proj.shape
    t, g = S5_CHUNK, S5_GROUPS
    nc = seq // t
    assert n % 128 == 0 and col % 128 == 0 and t % S5_TBLK == 0
    lanes = S5_TBLK * S5_GROUP
    return pl.pallas_call(
        _s5_gather_kernel,
        grid=(g // S5_SUPER, t // S5_TBLK),
        in_specs=[pl.BlockSpec((nc, S5_TBLK, 128), lambda sg, h: (0, h, col // 128 + sg)),
                  pl.BlockSpec((1024, 1024), lambda sg, h: (0, 0))],
        out_specs=pl.BlockSpec((S5_SUPER, nc, lanes), lambda sg, h: (sg, 0, h)),
        out_shape=jax.ShapeDtypeStruct((g, nc, t * S5_GROUP), BF16),
        scratch_shapes=[pltpu.VMEM((nc * S5_TBLK, 128), F32)],
        compiler_params=_cparams(("arbitrary", "arbitrary"), 40),
        name="s5_gather",
    )(proj.reshape(nc, t, n), _block_swap_matrix())


def _s5_scatter_kernel(y_ref, sel_ref, o_ref, rows_scr):
    nc = y_ref.shape[1]
    for q in range(S5_TBLK // S5_SUPER):
        lhs = jnp.concatenate([y_ref[a, :, q * 128:(q + 1) * 128] for a in range(S5_SUPER)],
                              axis=1)
        out = jnp.dot(lhs, sel_ref[...], preferred_element_type=F32)
        for b in range(S5_SUPER):
            rows_scr[pl.ds(S5_SUPER * q + b, nc, stride=S5_TBLK), :] = out[:, b * 128:(b + 1) * 128]
    o_ref[...] = rows_scr[...].reshape(nc, S5_TBLK, 128).astype(o_ref.dtype)


def _s5_scatter(yg):
    g, nc, th = yg.shape
    t = S5_CHUNK
    width = g * S5_GROUP
    lanes = S5_TBLK * S5_GROUP
    y3 = pl.pallas_call(
        _s5_scatter_kernel,
        grid=(g // S5_SUPER, t // S5_TBLK),
        in_specs=[pl.BlockSpec((S5_SUPER, nc, lanes), lambda sg, h: (sg, 0, h)),
                  pl.BlockSpec((1024, 1024), lambda sg, h: (0, 0))],
        out_specs=pl.BlockSpec((nc, S5_TBLK, 128), lambda sg, h: (0, h, sg)),
        out_shape=jax.ShapeDtypeStruct((nc, t, width), BF16),
        scratch_shapes=[pltpu.VMEM((nc * S5_TBLK, 128), F32)],
        compiler_params=_cparams(("arbitrary", "arbitrary"), 40),
        name="s5_scatter",
    )(yg, _block_swap_matrix())
    return y3.reshape(nc * t, width)


def _s5_mixer(proj, u_col, tables):
    m, fmat, emat, a_re, a_im = tables
    seq = proj.shape[0]
    t, g, hh, p = S5_CHUNK, S5_GROUPS, S5_GROUP, S5_STATE
    nc = seq // t
    th = t * hh
    sw = 4 * p
    ug = _s5_gather(proj, u_col)
    hend = pl.pallas_call(
        _s5_state_kernel,
        grid=(g,),
        in_specs=[pl.BlockSpec((1, nc, th), lambda i: (i, 0, 0)),
                  pl.BlockSpec((1, th, sw), lambda i: (i, 0, 0))],
        out_specs=pl.BlockSpec((nc, sw), lambda i: (0, i)),
        out_shape=jax.ShapeDtypeStruct((nc, g * sw), F32),
        compiler_params=_cparams(("arbitrary",), 32),
        name="s5_chunk_state",
    )(ug, fmat)
    hend3 = hend.reshape(nc, g, sw)
    cb = min(nc, 64)
    nb = nc // cb
    fwd_blk = lambda c: (c, 0, 0)
    bwd_blk = lambda c: (nb - 1 - c, 0, 0)
    hf, hb = pl.pallas_call(
        _s5_scan_kernel,
        grid=(nb,),
        in_specs=[pl.BlockSpec((cb, g, sw), fwd_blk), pl.BlockSpec((cb, g, sw), bwd_blk),
                  pl.BlockSpec((g, 2 * p), lambda c: (0, 0)),
                  pl.BlockSpec((g, 2 * p), lambda c: (0, 0))],
        out_specs=[pl.BlockSpec((cb, g, sw), fwd_blk), pl.BlockSpec((cb, g, sw), bwd_blk)],
        out_shape=[jax.ShapeDtypeStruct((nc, g, sw), BF16)] * 2,
        scratch_shapes=[pltpu.VMEM((g, 2 * p), F32), pltpu.VMEM((g, 2 * p), F32)],
        compiler_params=_cparams(("arbitrary",), 48),
        name="s5_chunk_scan",
    )(hend3, hend3, a_re, a_im)
    yg = pl.pallas_call(
        _s5_out_kernel,
        grid=(g,),
        in_specs=[pl.BlockSpec((1, nc, th), lambda i: (i, 0, 0)),
                  pl.BlockSpec((1, th, th), lambda i: (i, 0, 0)),
                  pl.BlockSpec((nc, sw), lambda i: (0, i)),
                  pl.BlockSpec((nc, sw), lambda i: (0, i)),
                  pl.BlockSpec((1, sw, th), lambda i: (i, 0, 0))],
        out_specs=pl.BlockSpec((1, nc, th), lambda i: (i, 0, 0)),
        out_shape=jax.ShapeDtypeStruct((g, nc, th), BF16),
        compiler_params=_cparams(("arbitrary",), 32),
        name="s5_output",
    )(ug, m, hf.reshape(nc, g * sw), hb.reshape(nc, g * sw), emat)
    return _s5_scatter(yg)


def _glu_kernel(y_ref, wv_ref, wg_ref, bv_ref, bg_ref, s_ref, o_ref):
    y = y_ref[...]
    val = jnp.dot(y, wv_ref[...], preferred_element_type=F32) + bv_ref[...]
    gt = jnp.dot(y, wg_ref[...], preferred_element_type=F32) + bg_ref[...]
    o_ref[...] = (val * _sigmoid(gt) * s_ref[...].astype(F32)).astype(o_ref.dtype)


def _glu(y, w_bf16, b, proj, *, s_col, tm=1024, tn=512):
    seq, kdim = y.shape
    width = w_bf16.shape[1] // 2
    assert s_col % tn == 0
    nj = width // tn
    b2 = b.reshape(1, 2 * width).astype(F32)
    return pl.pallas_call(
        _glu_kernel,
        grid=(seq // tm, nj),
        in_specs=[pl.BlockSpec((tm, kdim), lambda i, j: (i, 0)),
                  pl.BlockSpec((kdim, tn), lambda i, j: (0, j)),
                  pl.BlockSpec((kdim, tn), lambda i, j: (0, nj + j)),
                  pl.BlockSpec((1, tn), lambda i, j: (0, j)),
                  pl.BlockSpec((1, tn), lambda i, j: (0, nj + j)),
                  pl.BlockSpec((tm, tn), lambda i, j: (i, s_col // tn + j))],
        out_specs=pl.BlockSpec((tm, tn), lambda i, j: (i, j)),
        out_shape=jax.ShapeDtypeStruct((seq, width), BF16),
        compiler_params=_cparams(("arbitrary", "arbitrary"), 40),
        name="s5_glu",
    )(y, w_bf16, w_bf16, b2, b2, proj)


def _out_proj_kernel(a_ref, b_ref, wa_ref, wb_ref, x_ref, gate_ref, o_ref):
    acc = jnp.dot(a_ref[...], wa_ref[...], preferred_element_type=F32)
    acc = acc + jnp.dot(b_ref[...], wb_ref[...], preferred_element_type=F32)
    o_ref[...] = x_ref[...] + gate_ref[...] * acc


def _out_proj(oa, ob, w_bf16, x, gate, *, tm=1024, tn=512):
    seq, half = oa.shape
    d = w_bf16.shape[1]
    return pl.pallas_call(
        _out_proj_kernel,
        grid=(seq // tm, d // tn),
        in_specs=[pl.BlockSpec((tm, half), lambda i, j: (i, 0)),
                  pl.BlockSpec((tm, half), lambda i, j: (i, 0)),
                  pl.BlockSpec((half, tn), lambda i, j: (0, j)),
                  pl.BlockSpec((half, tn), lambda i, j: (1, j)),
                  pl.BlockSpec((tm, tn), lambda i, j: (i, j)),
                  pl.BlockSpec((1, tn), lambda i, j: (0, j))],
        out_specs=pl.BlockSpec((tm, tn), lambda i, j: (i, j)),
        out_shape=jax.ShapeDtypeStruct((seq, d), F32),
        compiler_params=_cparams(("arbitrary", "arbitrary"), 40),
        name="out_proj",
    )(oa, ob, w_bf16, w_bf16, x, gate)


def _na_bias_tables(rpb, rows):
    w = GRID_W
    nrb = rows // NA_QROWS
    assert rows >= NA_KROWS + NA_QROWS
    col = np.arange(w)
    col_start = np.clip(col - NA_COLS // 2, 0, w - NA_COLS)
    col_ok = (col[None, :] >= col_start[:, None]) & (col[None, :] < col_start[:, None] + NA_COLS)
    dc = np.clip(col[None, :] - col[:, None], -(NA_COLS - 1), NA_COLS - 1) + NA_COLS - 1
    onehot = (dc[None] == np.arange(2 * NA_COLS - 1)[:, None, None]).astype(np.float32)
    tt = jnp.einsum('hrd,dqk->hrqk', rpb.astype(F32) * math.log2(math.e), jnp.asarray(onehot),
                    precision=HIGHEST)
    tt = jnp.where(jnp.asarray(col_ok)[None, None], tt, NEG_INF)
    n_dr = 2 * NA_ROWS - 1
    tt = jnp.concatenate([tt, jnp.full((rpb.shape[0], 1, w, w), NEG_INF, F32)], axis=1)
    sel = np.zeros((3, NA_QROWS, NA_KROWS, n_dr + 1), np.float32)
    for ti, rb in enumerate((0, 1, nrb - 1)):
        ks = min(max(rb * NA_QROWS - NA_ROWS // 2, 0), rows - NA_KROWS)
        for rl in range(NA_QROWS):
            r = rb * NA_QROWS + rl
            rs = min(max(r - NA_ROWS // 2, 0), rows - NA_ROWS)
            for kl in range(NA_KROWS):
                kr = ks + kl
                sel[ti, rl, kl, kr - r + NA_ROWS - 1 if rs <= kr < rs + NA_ROWS else n_dr] = 1.0
    tab = jnp.einsum('trkd,hdqc->thrqkc', jnp.asarray(sel), tt, precision=HIGHEST)
    return tab.reshape(3, rpb.shape[0], NA_QROWS * w, NA_KROWS * w)


def _na_kernel(q_ref, k0_ref, k1_ref, k2_ref, v0_ref, v1_ref, v2_ref, b_ref, g_ref, o_ref):
    ones = jnp.ones((k0_ref.shape[0] * 3, HEAD_DIM), BF16)
    outs = []
    for h in range(C_HEADS):
        cols = slice(h * HEAD_DIM, (h + 1) * HEAD_DIM)
        k = jnp.concatenate([k0_ref[:, cols], k1_ref[:, cols], k2_ref[:, cols]], axis=0)
        v = jnp.concatenate([v0_ref[:, cols], v1_ref[:, cols], v2_ref[:, cols]], axis=0)
        s = lax.dot_general(q_ref[:, cols], k, (((1,), (1,)), ((), ())),
                            preferred_element_type=F32) + b_ref[0, h]
        m = jnp.max(s, axis=-1, keepdims=True)
        p = jnp.exp2(s - m).astype(BF16)
        pv = jnp.dot(p, jnp.concatenate([v, ones], axis=1), preferred_element_type=F32)
        o = pv[:, :HEAD_DIM] * (1.0 / pv[:, HEAD_DIM:])
        outs.append((o * g_ref[:, cols].astype(F32)).astype(o_ref.dtype))
    o_ref[...] = jnp.concatenate(outs, axis=1)


def _na_attention(proj, bias, *, q_col, k_col, v_col, g_col):
    seq = proj.shape[0]
    tq = NA_QROWS * GRID_W
    nrb = seq // tq
    nkb = NA_KROWS // NA_QROWS
    cw = C_HEADS * HEAD_DIM
    assert q_col % cw == 0 and k_col % cw == 0 and v_col % cw == 0 and g_col % cw == 0

    def kv_spec(col, off):
        return pl.BlockSpec((tq, cw), lambda rb: (jnp.clip(rb - 1, 0, nrb - nkb) + off, col // cw))

    btype = lambda rb: (jnp.where(rb == 0, 0, jnp.where(rb == nrb - 1, 2, 1)), 0, 0, 0)
    return pl.pallas_call(
        _na_kernel,
        grid=(nrb,),
        in_specs=[pl.BlockSpec((tq, cw), lambda rb: (rb, q_col // cw)),
                  kv_spec(k_col, 0), kv_spec(k_col, 1), kv_spec(k_col, 2),
                  kv_spec(v_col, 0), kv_spec(v_col, 1), kv_spec(v_col, 2),
                  pl.BlockSpec((1, C_HEADS, tq, nkb * tq), btype),
                  pl.BlockSpec((tq, cw), lambda rb: (rb, g_col // cw))],
        out_specs=pl.BlockSpec((tq, cw), lambda rb: (rb, 0)),
        out_shape=jax.ShapeDtypeStruct((seq, cw), BF16),
        compiler_params=_cparams(("arbitrary",), 48),
        name="na_attention",
    )(proj, proj, proj, proj, proj, proj, proj, bias, proj)


CONV_HALO = 8


def _conv_kernel(prev_ref, cur_ref, next_ref, w_ref, b_ref, o_ref):
    i = pl.program_id(0)
    tm = cur_ref.shape[0]
    prev = jnp.where(i == 0, 0.0, prev_ref[...].astype(F32))
    nxt = jnp.where(i == pl.num_programs(0) - 1, 0.0, next_ref[...].astype(F32))
    ext = jnp.concatenate([prev, cur_ref[...].astype(F32), nxt], axis=0)
    acc = jnp.zeros(cur_ref.shape, F32) + b_ref[...]
    for kk in range(SSD_CONV):
        start = CONV_HALO - SSD_CONV // 2 + kk
        acc = acc + ext[start:start + tm, :] * w_ref[kk:kk + 1, :]
    o_ref[...] = _silu(acc).astype(o_ref.dtype)


def _ssd_conv(proj, conv_w, conv_b, *, col, tm=256, tc=512):
    seq = proj.shape[0]
    ch = conv_w.shape[1]
    assert col % tc == 0 and ch % tc == 0
    nh = tm // CONV_HALO
    nblk = seq // CONV_HALO
    cb = col // tc
    return pl.pallas_call(
        _conv_kernel,
        grid=(seq // tm, ch // tc),
        in_specs=[pl.BlockSpec((CONV_HALO, tc), lambda i, j: (jnp.maximum(i * nh - 1, 0), cb + j)),
                  pl.BlockSpec((tm, tc), lambda i, j: (i, cb + j)),
                  pl.BlockSpec((CONV_HALO, tc),
                               lambda i, j: (jnp.minimum((i + 1) * nh, nblk - 1), cb + j)),
                  pl.BlockSpec((SSD_CONV, tc), lambda i, j: (0, j)),
                  pl.BlockSpec((1, tc), lambda i, j: (0, j))],
        out_specs=pl.BlockSpec((tm, tc), lambda i, j: (i, j)),
        out_shape=jax.ShapeDtypeStruct((seq, ch), BF16),
        compiler_params=_cparams(("arbitrary", "arbitrary"), 32),
        name="ssd_conv",
    )(proj, proj, proj, conv_w.astype(F32), conv_b.reshape(1, ch).astype(F32))


def _split_dot(a, b_bf16):
    hi = a.astype(BF16)
    lo = (a - hi.astype(F32)).astype(BF16)
    return (jnp.dot(hi, b_bf16, preferred_element_type=F32)
            + jnp.dot(lo, b_bf16, preferred_element_type=F32))


def _expand_heads(a, ex_bf16):
    return jnp.dot(a.astype(BF16), ex_bf16, preferred_element_type=F32)


def _ssd_direction(d, xs_ref, b_ref, c_ref, dt_ref, bias_ref, a_ref, ex, o_ref, st_scr):
    t = xs_ref.shape[0]
    gw = SSD_WIDTH // SSD_GROUPS
    hpg = SSD_HEADS // SSD_GROUPS
    row = lax.broadcasted_iota(jnp.int32, (t, t), 0)
    colm = lax.broadcasted_iota(jnp.int32, (t, t), 1)
    tri = (row >= colm) if d == 0 else (row <= colm)
    tri_b = tri.astype(BF16)

    z = dt_ref[...] + bias_ref[d]
    dt = jnp.maximum(z, 0.0) + jnp.log(1.0 + jnp.exp(-jnp.abs(z)))
    adt = dt * a_ref[d]
    a1 = adt.astype(BF16)
    r1 = adt - a1.astype(F32)
    a2 = r1.astype(BF16)
    a3 = (r1 - a2.astype(F32)).astype(BF16)
    r = (jnp.dot(tri_b, a1, preferred_element_type=F32)
         + jnp.dot(tri_b, a2, preferred_element_type=F32)
         + jnp.dot(tri_b, a3, preferred_element_type=F32))
    tot = r[t - 1:t, :] if d == 0 else r[0:1, :]
    dt_x = _expand_heads(dt, ex)
    er_x = _expand_heads(jnp.exp(r), ex)
    sd_x = _expand_heads(jnp.exp(tot - r), ex)
    et_x = _split_dot(jnp.exp(tot), ex)
    r_t = r.T

    xs = xs_ref[...].astype(F32)
    xd = xs * dt_x
    xd_b = xd.astype(BF16)
    xdd_b = (xd * sd_x).astype(BF16)

    y_parts = []
    for g in range(SSD_GROUPS):
        bg = b_ref[:, g * SSD_STATE:(g + 1) * SSD_STATE]
        cg = c_ref[:, g * SSD_STATE:(g + 1) * SSD_STATE]
        cb = lax.dot_general(cg, bg, (((1,), (1,)), ((), ())), preferred_element_type=F32)
        lanes = slice(g * gw, (g + 1) * gw)
        s_prev = st_scr[d, g]
        y_off = jnp.dot(cg, s_prev.astype(BF16), preferred_element_type=F32) * er_x[:, lanes]
        s_loc = lax.dot_general(bg, xdd_b[:, lanes], (((0,), (0,)), ((), ())),
                                preferred_element_type=F32)
        st_scr[d, g] = s_prev * et_x[:, lanes] + s_loc
        for hh in range(hpg):
            h = g * hpg + hh
            decay = jnp.exp(jnp.where(tri, r[:, h:h + 1] - r_t[h:h + 1, :], NEG_INF))
            sc = (cb * decay).astype(BF16)
            hl = slice(h * SSD_HEAD_DIM, (h + 1) * SSD_HEAD_DIM)
            y_parts.append(jnp.dot(sc, xd_b[:, hl], preferred_element_type=F32)
                           + y_off[:, hh * SSD_HEAD_DIM:(hh + 1) * SSD_HEAD_DIM])
    o_ref[...] = jnp.concatenate(y_parts, axis=-1).astype(o_ref.dtype)


def _ssd_scan_kernel(xf_ref, bf_ref, cf_ref, dtf_ref, xb_ref, bb_ref, cb_ref, dtb_ref,
                     bias_ref, a_ref, ex_ref, of_ref, ob_ref, st_scr):
    @pl.when(pl.program_id(0) == 0)
    def _():
        st_scr[...] = jnp.zeros(st_scr.shape, F32)

    ex = ex_ref[...]
    _ssd_direction(0, xf_ref, bf_ref, cf_ref, dtf_ref, bias_ref, a_ref, ex, of_ref, st_scr)
    _ssd_direction(1, xb_ref, bb_ref, cb_ref, dtb_ref, bias_ref, a_ref, ex, ob_ref, st_scr)


def _ssd_scan(conv, dt_raw, dt_bias, a_log):
    seq = conv.shape[0]
    t = SSD_CHUNK
    nc = seq // t
    a = -jnp.exp(a_log.astype(F32)).reshape(2, 1, SSD_HEADS)
    bias = dt_bias.astype(F32).reshape(2, 1, SSD_HEADS)
    ex = jnp.repeat(jnp.eye(SSD_HEADS, dtype=BF16), SSD_HEAD_DIM, axis=1)
    dt_f = dt_raw[:, :SSD_HEADS]
    dt_b = dt_raw[:, SSD_HEADS:2 * SSD_HEADS]
    nxb = SSD_WIDTH // SSD_BC
    fwd = lambda c: c
    bwd = lambda c: nc - 1 - c

    def chunk_specs(pos):
        return [pl.BlockSpec((t, SSD_WIDTH), lambda c: (pos(c), 0)),
                pl.BlockSpec((t, SSD_BC), lambda c: (pos(c), nxb)),
                pl.BlockSpec((t, SSD_BC), lambda c: (pos(c), nxb + 1)),
                pl.BlockSpec((t, SSD_HEADS), lambda c: (pos(c), 0))]

    const3 = pl.BlockSpec((2, 1, SSD_HEADS), lambda c: (0, 0, 0))
    return pl.pallas_call(
        _ssd_scan_kernel,
        grid=(nc,),
        in_specs=chunk_specs(fwd) + chunk_specs(bwd) + [
            const3, const3, pl.BlockSpec((SSD_HEADS, SSD_WIDTH), lambda c: (0, 0))],
        out_specs=[pl.BlockSpec((t, SSD_WIDTH), lambda c: (fwd(c), 0)),
                   pl.BlockSpec((t, SSD_WIDTH), lambda c: (bwd(c), 0))],
        out_shape=[jax.ShapeDtypeStruct((seq, SSD_WIDTH), BF16)] * 2,
        scratch_shapes=[pltpu.VMEM((2, SSD_GROUPS, SSD_STATE, SSD_WIDTH // SSD_GROUPS), F32)],
        compiler_params=_cparams(("arbitrary",), 40),
        name="ssd_scan",
    )(conv, conv, conv, dt_f, conv, conv, conv, dt_b, bias, a, ex)


def _gated_norm_kernel(yf_ref, yb_ref, xs_ref, z_ref, d_ref, w_ref, o_ref):
    y = (yf_ref[...].astype(F32) + yb_ref[...].astype(F32)
         + d_ref[...] * xs_ref[...].astype(F32))
    y = y * z_ref[...].astype(F32)
    ms = jnp.mean(y * y, axis=-1, keepdims=True)
    o_ref[...] = (y * lax.rsqrt(ms + EPS) * w_ref[...]).astype(o_ref.dtype)


def _gated_norm(y_f, y_b, conv, proj, d_x, norm_w, *, z_col, tm=512):
    seq = conv.shape[0]
    w = SSD_WIDTH
    assert z_col % w == 0
    return pl.pallas_call(
        _gated_norm_kernel,
        grid=(seq // tm,),
        in_specs=[pl.BlockSpec((tm, w), lambda i: (i, 0)),
                  pl.BlockSpec((tm, w), lambda i: (i, 0)),
                  pl.BlockSpec((tm, w), lambda i: (i, 0)),
                  pl.BlockSpec((tm, w), lambda i: (i, z_col // w)),
                  pl.BlockSpec((1, w), lambda i: (0, 0)),
                  pl.BlockSpec((1, w), lambda i: (0, 0))],
        out_specs=pl.BlockSpec((tm, w), lambda i: (i, 0)),
        out_shape=jax.ShapeDtypeStruct((seq, w), BF16),
        compiler_params=_cparams(("arbitrary",), 32),
        name="ssd_gated_norm",
    )(y_f, y_b, conv, proj, d_x, norm_w.reshape(1, w).astype(F32))


def _rope_tables(seq):
    rows = seq // GRID_W
    n_axis = HEAD_DIM // 4
    inv = ROPE_THETA ** (-np.arange(n_axis, dtype=np.float64) / n_axis)
    ang_r = np.arange(rows, dtype=np.float64)[:, None] * inv
    ang_c = np.arange(GRID_W, dtype=np.float64)[:, None] * inv

    def expand(fr, fc):
        fr = jnp.asarray(fr.astype(np.float32))
        fc = jnp.asarray(fc.astype(np.float32))
        tab = jnp.concatenate(
            [jnp.broadcast_to(fr[:, None, :], (rows, GRID_W, n_axis)),
             jnp.broadcast_to(fc[None, :, :], (rows, GRID_W, n_axis))], axis=-1)
        return tab.reshape(seq, 2 * n_axis)

    cos = expand(np.cos(ang_r), np.cos(ang_c))
    sin = expand(np.sin(ang_r), np.sin(ang_c))
    return jnp.concatenate([cos, cos], axis=-1), jnp.concatenate([-sin, sin], axis=-1)


def _deinterleave_perm():
    return np.concatenate([np.arange(0, HEAD_DIM, 2), np.arange(1, HEAD_DIM, 2)])


def _layer_attn_s5(x, c, norm_g, ada_w, ada_b, w_in, q_norm, k_norm, lam_re, lam_im, log_step,
                   b_re, b_im, c_re, c_im, s5_d, w_glu, b_glu, w_out):
    seq, d = x.shape
    shift, scale1p, gate = _ada_mod(c, ada_w, ada_b)
    aw = A_HEADS * HEAD_DIM
    akw = A_KV_HEADS * HEAD_DIM
    perm = _deinterleave_perm()
    nqk = (aw + akw) // HEAD_DIM
    colperm = (np.arange(nqk)[:, None] * HEAD_DIM + perm[None, :]).reshape(-1)
    w = jnp.concatenate([w_in[:, colperm], w_in[:, aw + akw:]], axis=1).astype(BF16)
    cos2, sin2 = _rope_tables(seq)
    t = IN_TN
    q_col, k_col, v_col = 0, aw, aw + akw
    g_col = aw + 2 * akw
    u_col = g_col + aw
    gb_col = u_col + d // 2
    roles = ((q_col // t, k_col // t, "q"), (k_col // t, v_col // t, "k"),
             (v_col // t, g_col // t, "plain"), (g_col // t, u_col // t, "silu"),
             (u_col // t, gb_col // t, "plain"), (gb_col // t, (gb_col + d // 2) // t, "silu"))
    (proj,) = _in_proj(x, norm_g, scale1p, shift, w, q_norm[perm], k_norm[perm], cos2, sin2,
                       roles=roles, rope=True, q_scale=HEAD_DIM ** -0.5 * math.log2(math.e),
                       has_aux=False)
    o_a = _gqa_attention(proj, q_col=q_col, k_col=k_col, v_col=v_col, g_col=g_col)
    tables = _s5_tables(lam_re, lam_im, log_step, b_re, b_im, c_re, c_im, s5_d)
    y = _s5_mixer(proj, u_col, tables)
    o_b = _glu(y, w_glu.astype(BF16), b_glu, proj, s_col=gb_col)
    return _out_proj(o_a, o_b, w_out.astype(BF16), x, gate)


def _layer_na_ssd(x, c, norm_g, ada_w, ada_b, w_in, q_norm, k_norm, rpb, conv_w, conv_b,
                  dt_bias, a_log, ssd_d, norm_w, w_out):
    seq, d = x.shape
    shift, scale1p, gate = _ada_mod(c, ada_w, ada_b)
    cw = C_HEADS * HEAD_DIM
    n_in = w_in.shape[1]
    t = IN_TN
    n_pad = -(-n_in // t) * t
    w = jnp.pad(w_in, ((0, 0), (0, n_pad - n_in))).astype(BF16)
    q_col, k_col, v_col, g_col, z_col = 0, cw, 2 * cw, 3 * cw, 4 * cw
    xbc_col = z_col + SSD_WIDTH
    dt_col = xbc_col + SSD_WIDTH + 2 * SSD_BC
    roles = ((q_col // t, k_col // t, "q"), (k_col // t, v_col // t, "k"),
             (v_col // t, g_col // t, "plain"), (g_col // t, xbc_col // t, "silu"),
             (xbc_col // t, dt_col // t, "plain"), (dt_col // t, n_pad // t, "aux"))
    dummy = jnp.zeros((seq, HEAD_DIM), F32)
    proj, dt_raw = _in_proj(x, norm_g, scale1p, shift, w, q_norm, k_norm, dummy, dummy,
                            roles=roles, rope=False, q_scale=HEAD_DIM ** -0.5 * math.log2(math.e),
                            has_aux=True)
    bias = _na_bias_tables(rpb, seq // GRID_W)
    o_c = _na_attention(proj, bias, q_col=q_col, k_col=k_col, v_col=v_col, g_col=g_col)
    conv = _ssd_conv(proj, conv_w, conv_b, col=xbc_col)
    y_f, y_b = _ssd_scan(conv, dt_raw, dt_bias, a_log)
    d_x = jnp.repeat(ssd_d.astype(F32), SSD_HEAD_DIM).reshape(1, SSD_WIDTH)
    o_d = _gated_norm(y_f, y_b, conv, proj, d_x, norm_w, z_col=z_col)
    return _out_proj(o_c, o_d, w_out.astype(BF16), x, gate)


def kernel(x, c, e_norm_g, e_ada_w, e_ada_b, e_w_in, e_q_norm, e_k_norm, s5_lam_re, s5_lam_im,
           s5_log_step, s5_b_re, s5_b_im, s5_c_re, s5_c_im, s5_d, s5_w_glu, s5_b_glu, e_w_out,
           o_norm_g, o_ada_w, o_ada_b, o_w_in, o_q_norm, o_k_norm, na_rpb, ssd_conv_w, ssd_conv_b,
           ssd_dt_bias, ssd_a_log, ssd_d, ssd_norm_w, o_w_out):
    assert x.shape[0] == 1
    h = x[0]
    h = _layer_attn_s5(h, c, e_norm_g[0], e_ada_w[0], e_ada_b[0], e_w_in[0], e_q_norm[0],
                       e_k_norm[0], s5_lam_re[0], s5_lam_im[0], s5_log_step[0], s5_b_re[0],
                       s5_b_im[0], s5_c_re[0], s5_c_im[0], s5_d[0], s5_w_glu[0], s5_b_glu[0],
                       e_w_out[0])
    h = _layer_na_ssd(h, c, o_norm_g[0], o_ada_w[0], o_ada_b[0], o_w_in[0], o_q_norm[0],
                      o_k_norm[0], na_rpb[0], ssd_conv_w[0], ssd_conv_b[0], ssd_dt_bias[0],
                      ssd_a_log[0], ssd_d[0], ssd_norm_w[0], o_w_out[0])
    return h[None]
```

```python
import functools
import math

import jax
import jax.numpy as jnp
import numpy as np
from jax import lax
from jax.experimental import pallas as pl
from jax.experimental.pallas import tpu as pltpu

F32 = jnp.float32
BF16 = jnp.bfloat16
HIGHEST = lax.Precision.HIGHEST

GRID_W = 64
HEAD_DIM = 128
EPS = 1e-6
NEG_INF = -1e30
ROPE_THETA = 10000.0

A_HEADS = 8
A_KV_HEADS = 2
A_GROUP = A_HEADS // A_KV_HEADS
S5_GROUP = 16
S5_GROUPS = 64
S5_STATE = 64
S5_CHUNK = 32
C_HEADS = 8
NA_ROWS = 8
NA_COLS = 16
NA_QROWS = 4
NA_KROWS = 12
SSD_HEADS = 16
SSD_HEAD_DIM = 64
SSD_GROUPS = 2
SSD_STATE = 128
SSD_CONV = 5
SSD_CHUNK = 128
SSD_WIDTH = SSD_HEADS * SSD_HEAD_DIM
SSD_BC = SSD_GROUPS * SSD_STATE

V7X_VMEM_BYTES = 64 * 1024 * 1024
MiB = 1024 * 1024


def _cparams(semantics, vmem_mib):
    assert vmem_mib * MiB < V7X_VMEM_BYTES
    return pltpu.CompilerParams(dimension_semantics=semantics, vmem_limit_bytes=vmem_mib * MiB)


def _silu(x):
    return x * (1.0 / (1.0 + jnp.exp(-x)))


def _sigmoid(x):
    return 1.0 / (1.0 + jnp.exp(-x))


def _ada_kernel(c_ref, w_ref, b_ref, o_ref):
    sc = _silu(c_ref[...])
    o_ref[...] = jnp.sum(w_ref[...] * sc, axis=0, keepdims=True) + b_ref[...]


def _ada_mod(c, w, b):
    d, n = w.shape
    tn = 512
    out = pl.pallas_call(
        _ada_kernel,
        grid=(n // tn,),
        in_specs=[pl.BlockSpec((d, 1), lambda j: (0, 0)),
                  pl.BlockSpec((d, tn), lambda j: (0, j)),
                  pl.BlockSpec((1, tn), lambda j: (0, j))],
        out_specs=pl.BlockSpec((1, tn), lambda j: (0, j)),
        out_shape=jax.ShapeDtypeStruct((1, n), F32),
        compiler_params=_cparams(("arbitrary",), 24),
        name="ada_mod",
    )(c.astype(F32).reshape(d, 1), w, b.reshape(1, n))
    shift, scale, gate = jnp.split(out, 3, axis=-1)
    return shift, 1.0 + scale, gate


IN_TN = 256
IN_ROW_CHUNK = 64


def _in_proj_kernel(x_ref, g_ref, sc_ref, sh_ref, w_ref, qn_ref, kn_ref, cos_ref, sin_ref,
                    *out_and_scratch, roles, rope, q_scale, has_aux):
    if has_aux:
        o_ref, aux_ref, h_scr = out_and_scratch
    else:
        o_ref, h_scr = out_and_scratch
        aux_ref = None
    j = pl.program_id(1)
    tm = x_ref.shape[0]

    @pl.when(j == 0)
    def _():
        def body(r, carry):
            rows = pl.ds(pl.multiple_of(r * IN_ROW_CHUNK, IN_ROW_CHUNK), IN_ROW_CHUNK)
            xf = x_ref[rows, :]
            ms = jnp.mean(xf * xf, axis=-1, keepdims=True)
            y = xf * lax.rsqrt(ms + EPS) * g_ref[...]
            h_scr[rows, :] = (y * sc_ref[...] + sh_ref[...]).astype(BF16)
            return carry
        lax.fori_loop(0, tm // IN_ROW_CHUNK, body, 0, unroll=2)

    acc = jnp.dot(h_scr[...], w_ref[...], preferred_element_type=F32)

    def head_norm(a, gain):
        ms = jnp.mean(a * a, axis=-1, keepdims=True)
        return a * lax.rsqrt(ms + EPS) * gain

    def qk_epilogue(gain_ref, scale):
        outs = []
        for h in range(IN_TN // HEAD_DIM):
            a = head_norm(acc[:, h * HEAD_DIM:(h + 1) * HEAD_DIM], gain_ref[...])
            if rope:
                a = a * cos_ref[...] + pltpu.roll(a, HEAD_DIM // 2, axis=1) * sin_ref[...]
            if scale != 1.0:
                a = a * scale
            outs.append(a)
        o_ref[...] = jnp.concatenate(outs, axis=-1).astype(o_ref.dtype)

    for lo, hi, role in roles:
        @pl.when((j >= lo) & (j < hi))
        def _(role=role):
            if role == "q":
                qk_epilogue(qn_ref, q_scale)
            elif role == "k":
                qk_epilogue(kn_ref, 1.0)
            elif role == "silu":
                o_ref[...] = _silu(acc).astype(o_ref.dtype)
            elif role == "plain":
                o_ref[...] = acc.astype(o_ref.dtype)
            elif role == "aux":
                o_ref[...] = acc.astype(o_ref.dtype)
                aux_ref[...] = acc
            else:
                raise ValueError(role)


def _in_proj(x, norm_g, scale1p, shift, w_bf16, q_gain, k_gain, cos2, sin2, *, roles, rope,
             q_scale, has_aux, tm=1024):
    seq, d = x.shape
    n = w_bf16.shape[1]
    assert seq % tm == 0 and n % IN_TN == 0
    assert roles[-1][1] == n // IN_TN
    row = lambda i, j: (i, 0)
    const = lambda i, j: (0, 0)
    out_shape = [jax.ShapeDtypeStruct((seq, n), BF16)]
    out_specs = [pl.BlockSpec((tm, IN_TN), lambda i, j: (i, j))]
    if has_aux:
        out_shape.append(jax.ShapeDtypeStruct((seq, IN_TN), F32))
        out_specs.append(pl.BlockSpec((tm, IN_TN), row))
    kern = functools.partial(_in_proj_kernel, roles=roles, rope=rope, q_scale=q_scale,
                             has_aux=has_aux)
    return pl.pallas_call(
        kern,
        grid=(seq // tm, n // IN_TN),
        in_specs=[pl.BlockSpec((tm, d), row),
                  pl.BlockSpec((1, d), const), pl.BlockSpec((1, d), const),
                  pl.BlockSpec((1, d), const),
                  pl.BlockSpec((d, IN_TN), lambda i, j: (0, j)),
                  pl.BlockSpec((1, HEAD_DIM), const), pl.BlockSpec((1, HEAD_DIM), const),
                  pl.BlockSpec((tm, HEAD_DIM), row), pl.BlockSpec((tm, HEAD_DIM), row)],
        out_specs=out_specs,
        out_shape=out_shape,
        scratch_shapes=[pltpu.VMEM((tm, d), BF16)],
        compiler_params=_cparams(("arbitrary", "arbitrary"), 48),
        name="in_proj_rope" if rope else "in_proj",
    )(x, norm_g.reshape(1, d), scale1p, shift, w_bf16, q_gain.reshape(1, HEAD_DIM),
      k_gain.reshape(1, HEAD_DIM), cos2, sin2)


GQA_TK = 1024


def _gqa_kernel(q_ref, k_ref, v_ref, g_ref, o_ref, acc_scr, m_scr, s_scr):
    tq = q_ref.shape[0]
    nk = k_ref.shape[0] // GQA_TK
    q_all = jnp.concatenate(
        [q_ref[:, h * HEAD_DIM:(h + 1) * HEAD_DIM] for h in range(A_GROUP)], axis=0)
    acc_scr[...] = jnp.zeros(acc_scr.shape, F32)
    m_scr[...] = jnp.full(m_scr.shape, -jnp.inf, F32)
    ones = jnp.ones((GQA_TK, HEAD_DIM), BF16)

    def key_rows(kc):
        return pl.ds(pl.multiple_of(kc * GQA_TK, GQA_TK), GQA_TK)

    def scores(kc):
        return lax.dot_general(q_all, k_ref[key_rows(kc), :], (((1,), (1,)), ((), ())),
                               preferred_element_type=F32)

    def softmax_pv(slot, kc):
        s = s_scr[slot]
        v1 = jnp.concatenate([v_ref[key_rows(kc), :], ones], axis=1)
        m_prev = m_scr[...]
        m_new = jnp.maximum(m_prev, jnp.max(s, axis=-1, keepdims=True))
        alpha = jnp.exp2(m_prev - m_new)
        p = jnp.concatenate(
            [jnp.exp2(s[:, j * HEAD_DIM:(j + 1) * HEAD_DIM] - m_new).astype(BF16)
             for j in range(GQA_TK // HEAD_DIM)], axis=1)
        pv = jnp.dot(p, v1, preferred_element_type=F32)
        acc_scr[...] = jnp.concatenate([alpha, alpha], axis=1) * acc_scr[...] + pv
        m_scr[...] = m_new

    s_scr[0] = scores(0)

    def pair(i, carry):
        kc = 2 * i
        s_scr[1] = scores(kc + 1)
        softmax_pv(0, kc)
        s_scr[0] = scores(kc + 2)
        softmax_pv(1, kc + 1)
        return carry

    lax.fori_loop(0, nk // 2 - 1, pair, 0)
    s_scr[1] = scores(nk - 1)
    softmax_pv(0, nk - 2)
    softmax_pv(1, nk - 1)
    for h in range(A_GROUP):
        cols = slice(h * HEAD_DIM, (h + 1) * HEAD_DIM)
        a = acc_scr[h * tq:(h + 1) * tq, :]
        o = a[:, :HEAD_DIM] * (1.0 / a[:, HEAD_DIM:])
        o_ref[:, cols] = (o * g_ref[:, cols].astype(F32)).astype(o_ref.dtype)


def _gqa_attention(proj, *, q_col, k_col, v_col, g_col, tq=256):
    seq = proj.shape[0]
    gw = A_GROUP * HEAD_DIM
    assert q_col % gw == 0 and g_col % gw == 0 and k_col % HEAD_DIM == 0 and v_col % HEAD_DIM == 0
    assert seq % (2 * GQA_TK) == 0 and seq % tq == 0
    return pl.pallas_call(
        _gqa_kernel,
        grid=(A_KV_HEADS, seq // tq),
        in_specs=[pl.BlockSpec((tq, gw), lambda kh, qi: (qi, q_col // gw + kh)),
                  pl.BlockSpec((seq, HEAD_DIM), lambda kh, qi: (0, k_col // HEAD_DIM + kh)),
                  pl.BlockSpec((seq, HEAD_DIM), lambda kh, qi: (0, v_col // HEAD_DIM + kh)),
                  pl.BlockSpec((tq, gw), lambda kh, qi: (qi, g_col // gw + kh))],
        out_specs=pl.BlockSpec((tq, gw), lambda kh, qi: (qi, kh)),
        out_shape=jax.ShapeDtypeStruct((seq, A_HEADS * HEAD_DIM), BF16),
        scratch_shapes=[pltpu.VMEM((A_GROUP * tq, 2 * HEAD_DIM), F32),
                        pltpu.VMEM((A_GROUP * tq, HEAD_DIM), F32),
                        pltpu.VMEM((2, A_GROUP * tq, GQA_TK), F32)],
        compiler_params=_cparams(("arbitrary", "arbitrary"), 56),
        name="gqa_attention",
    )(proj, proj, proj, proj)


def _s5_tables(lam_re, lam_im, log_step, b_re, b_im, c_re, c_im, s5_d):
    t = S5_CHUNK
    g, p, hh = S5_GROUPS, S5_STATE, S5_GROUP
    lr = lam_re.astype(F32)
    li = lam_im.astype(F32)
    dt = jnp.exp(log_step.astype(F32))[..., None]
    mag = jnp.exp(lr * dt)
    ab_re = mag * jnp.cos(li * dt)
    ab_im = mag * jnp.sin(li * dt)
    den = lr * lr + li * li
    num_re = ab_re - 1.0
    f_re = (num_re * lr + ab_im * li) / den
    f_im = (ab_im * lr - num_re * li) / den
    br = b_re.astype(F32)
    bi = b_im.astype(F32)
    bb_re = f_re[..., None] * br - f_im[..., None] * bi
    bb_im = f_re[..., None] * bi + f_im[..., None] * br
    kk = jnp.arange(t + 1, dtype=F32)[None, None, :, None]
    pmag = jnp.exp(kk * (lr * dt)[:, :, None, :])
    ang = kk * (li * dt)[:, :, None, :]
    pw_re = pmag * jnp.cos(ang)
    pw_im = pmag * jnp.sin(ang)
    bt_re = jnp.swapaxes(bb_re, -1, -2)[:, :, None]
    bt_im = jnp.swapaxes(bb_im, -1, -2)[:, :, None]
    pr = pw_re[:, :, :, None, :]
    pi = pw_im[:, :, :, None, :]
    wt_re = pr * bt_re - pi * bt_im
    wt_im = pr * bt_im + pi * bt_re
    cr = c_re.astype(F32)
    ci = c_im.astype(F32)
    cc_re = cr[:, :, None]
    cc_im = ci[:, :, None]
    cl_re = cc_re * pr - cc_im * pi
    cl_im = cc_re * pi + cc_im * pr
    bt_cat = jnp.concatenate([bt_re[:, :, 0], -bt_im[:, :, 0]], axis=-1)
    cl_cat = jnp.concatenate([cl_re, cl_im], axis=-1).reshape(2, g, (t + 1) * hh, 2 * p)
    kj = jnp.einsum('dgjq,dgnq->dgjn', bt_cat, cl_cat, precision=HIGHEST)
    kj = kj.reshape(2, g, hh, t + 1, hh)
    dmat = s5_d.astype(F32).reshape(g, hh)[:, :, None] * jnp.eye(hh, dtype=F32)[None]
    k0 = kj[0, :, :, 0] + kj[1, :, :, 0] + dmat
    kb = kj[1, :, :, 1:t][:, :, ::-1]
    kf = kj[0, :, :, 1:t]
    kflat = jnp.concatenate([kb, k0[:, :, None], kf], axis=2).reshape(g, hh, (2 * t - 1) * hh)
    m = jnp.stack([kflat[:, :, (t - 1 - tp) * hh:(2 * t - 1 - tp) * hh] for tp in range(t)],
                  axis=1).reshape(g, t * hh, t * hh)
    flat = lambda a: a.reshape(g, t * hh, p)
    fmat = jnp.concatenate([flat(wt_re[0, :, :t][:, ::-1]), flat(wt_re[1, :, :t]),
                            flat(wt_im[0, :, :t][:, ::-1]), flat(wt_im[1, :, :t])], axis=-1)
    e_t = lambda a: jnp.swapaxes(flat(a), 1, 2)
    emat = jnp.concatenate([e_t(cl_re[0, :, 1:t + 1]), e_t(cl_re[1, :, 1:t + 1][:, ::-1]),
                            -e_t(cl_im[0, :, 1:t + 1]), -e_t(cl_im[1, :, 1:t + 1][:, ::-1])],
                           axis=1)
    a_re = jnp.concatenate([pw_re[0, :, t], pw_re[1, :, t]], axis=-1)
    a_im = jnp.concatenate([pw_im[0, :, t], pw_im[1, :, t]], axis=-1)
    return m.astype(BF16), fmat.astype(BF16), emat.astype(BF16), a_re, a_im


S5_GPS = 4


def _s5_state_kernel(u_ref, f_ref, o_ref):
    sw = f_ref.shape[2]
    for a in range(S5_GPS):
        o_ref[:, a * sw:(a + 1) * sw] = jnp.dot(u_ref[a], f_ref[a], preferred_element_type=F32)


def _s5_fwd_lanes(shape):
    lane = lax.broadcasted_iota(jnp.int32, shape, len(shape) - 1)
    return (lane % (2 * S5_STATE)) < S5_STATE


def _s5_scan_kernel(xf_ref, xb_ref, ar_ref, ai_ref, hf_ref, hb_ref, re_scr, im_scr):
    cb = xf_ref.shape[0]
    w = 2 * S5_STATE

    @pl.when(pl.program_id(0) == 0)
    def _():
        re_scr[...] = jnp.zeros(re_scr.shape, F32)
        im_scr[...] = jnp.zeros(im_scr.shape, F32)

    ar = ar_ref[...]
    ai = ai_ref[...]
    fwd = _s5_fwd_lanes(ar.shape)

    def body(i, carry):
        re, im = carry
        j = cb - 1 - i
        st = jnp.concatenate([re, im], axis=-1).astype(hf_ref.dtype)
        hf_ref[i] = st
        hb_ref[j] = st
        xf = xf_ref[i]
        xb = xb_ref[j]
        x_re = jnp.where(fwd, xf[:, :w], xb[:, :w])
        x_im = jnp.where(fwd, xf[:, w:], xb[:, w:])
        return ar * re - ai * im + x_re, ar * im + ai * re + x_im

    re, im = lax.fori_loop(0, cb, body, (re_scr[...], im_scr[...]), unroll=2)
    re_scr[...] = re
    im_scr[...] = im


def _s5_out_kernel(u_ref, m_ref, hf_ref, hb_ref, e_ref, o_ref):
    sw = e_ref.shape[1]
    c0 = math.sqrt(2.0 / math.pi)
    for a in range(S5_GPS):
        lanes = slice(a * sw, (a + 1) * sw)
        hf = hf_ref[:, lanes]
        y = jnp.dot(u_ref[a], m_ref[a], preferred_element_type=F32)
        h = jnp.where(_s5_fwd_lanes(hf.shape), hf, hb_ref[:, lanes])
        y = y + jnp.dot(h, e_ref[a], preferred_element_type=F32)
        y = 0.5 * y * (1.0 + jnp.tanh(c0 * (y + 0.044715 * (y * y * y))))
        o_ref[a] = y.astype(o_ref.dtype)


S5_SUPER = 128 // S5_GROUP


def _block_swap_matrix():
    n = S5_SUPER
    idx = np.arange(n * n * S5_GROUP)
    b, a, j = idx // (n * S5_GROUP), (idx // S5_GROUP) % n, idx % S5_GROUP
    sel = np.zeros((idx.size, idx.size), np.float32)
    sel[idx, a * n * S5_GROUP + b * S5_GROUP + j] = 1.0
    return jnp.asarray(sel, BF16)


S5_TBLK = 16


def _s5_gather_kernel(x_ref, sel_ref, o_ref, rows_scr):
    nc = x_ref.shape[0]
    rows_scr[...] = x_ref[...].astype(F32).reshape(nc * S5_TBLK, 128)
    for q in range(S5_TBLK // S5_SUPER):
        lhs = jnp.concatenate(
            [rows_scr[pl.ds(S5_SUPER * q + b, nc, stride=S5_TBLK), :].astype(BF16)
             for b in range(S5_SUPER)], axis=1)
        out = jnp.dot(lhs, sel_ref[...], preferred_element_type=F32).astype(o_ref.dtype)
        for a in range(S5_SUPER):
            o_ref[a, :, q * 128:(q + 1) * 128] = out[:, a * 128:(a + 1) * 128]


def _s5_gather(proj, col):
    seq, n = proj.shape
    t, g = S5_CHUNK, S5_GROUPS
    nc = seq // t
    assert n % 128 == 0 and col % 128 == 0 and t % S5_TBLK == 0
    lanes = S5_TBLK * S5_GROUP
    return pl.pallas_call(
        _s5_gather_kernel,
        grid=(g // S5_SUPER, t // S5_TBLK),
        in_specs=[pl.BlockSpec((nc, S5_TBLK, 128), lambda sg, h: (0, h, col // 128 + sg)),
                  pl.BlockSpec((1024, 1024), lambda sg, h: (0, 0))],
        out_specs=pl.BlockSpec((S5_SUPER, nc, lanes), lambda sg, h: (sg, 0, h)),
        out_shape=jax.ShapeDtypeStruct((g, nc, t * S5_GROUP), BF16),
        scratch_shapes=[pltpu.VMEM((nc * S5_TBLK, 128), F32)],
        compiler_params=_cparams(("arbitrary", "arbitrary"), 40),
        name="s5_gather",
    )(proj.reshape(nc, t, n), _block_swap_matrix())


def _s5_scatter_kernel(y_ref, sel_ref, o_ref, rows_scr):
    nc = y_ref.shape[1]
    for q in range(S5_TBLK // S5_SUPER):
        lhs = jnp.concatenate([y_ref[a, :, q * 128:(q + 1) * 128] for a in range(S5_SUPER)],
                              axis=1)
        out = jnp.dot(lhs, sel_ref[...], preferred_element_type=F32)
        for b in range(S5_SUPER):
            rows_scr[pl.ds(S5_SUPER * q + b, nc, stride=S5_TBLK), :] = out[:, b * 128:(b + 1) * 128]
    o_ref[...] = rows_scr[...].reshape(nc, S5_TBLK, 128).astype(o_ref.dtype)


def _s5_scatter(yg):
    g, nc, th = yg.shape
    t = S5_CHUNK
    width = g * S5_GROUP
    lanes = S5_TBLK * S5_GROUP
    y3 = pl.pallas_call(
        _s5_scatter_kernel,
        grid=(g // S5_SUPER, t // S5_TBLK),
        in_specs=[pl.BlockSpec((S5_SUPER, nc, lanes), lambda sg, h: (sg, 0, h)),
                  pl.BlockSpec((1024, 1024), lambda sg, h: (0, 0))],
        out_specs=pl.BlockSpec((nc, S5_TBLK, 128), lambda sg, h: (0, h, sg)),
        out_shape=jax.ShapeDtypeStruct((nc, t, width), BF16),
        scratch_shapes=[pltpu.VMEM((nc * S5_TBLK, 128), F32)],
        compiler_params=_cparams(("arbitrary", "arbitrary"), 40),
        name="s5_scatter",
    )(yg, _block_swap_matrix())
    return y3.reshape(nc * t, width)


def _s5_mixer(proj, u_col, tables):
    m, fmat, emat, a_re, a_im = tables
    seq = proj.shape[0]
    t, g, hh, p = S5_CHUNK, S5_GROUPS, S5_GROUP, S5_STATE
    nc = seq // t
    th = t * hh
    sw = 4 * p
    ug = _s5_gather(proj, u_col)
    hend = pl.pallas_call(
        _s5_state_kernel,
        grid=(g // S5_GPS,),
        in_specs=[pl.BlockSpec((S5_GPS, nc, th), lambda i: (i, 0, 0)),
                  pl.BlockSpec((S5_GPS, th, sw), lambda i: (i, 0, 0))],
        out_specs=pl.BlockSpec((nc, S5_GPS * sw), lambda i: (0, i)),
        out_shape=jax.ShapeDtypeStruct((nc, g * sw), F32),
        compiler_params=_cparams(("arbitrary",), 32),
        name="s5_chunk_state",
    )(ug, fmat)
    hend3 = hend.reshape(nc, g, sw)
    cb = min(nc, 64)
    nb = nc // cb
    fwd_blk = lambda c: (c, 0, 0)
    bwd_blk = lambda c: (nb - 1 - c, 0, 0)
    hf, hb = pl.pallas_call(
        _s5_scan_kernel,
        grid=(nb,),
        in_specs=[pl.BlockSpec((cb, g, sw), fwd_blk), pl.BlockSpec((cb, g, sw), bwd_blk),
                  pl.BlockSpec((g, 2 * p), lambda c: (0, 0)),
                  pl.BlockSpec((g, 2 * p), lambda c: (0, 0))],
        out_specs=[pl.BlockSpec((cb, g, sw), fwd_blk), pl.BlockSpec((cb, g, sw), bwd_blk)],
        out_shape=[jax.ShapeDtypeStruct((nc, g, sw), BF16)] * 2,
        scratch_shapes=[pltpu.VMEM((g, 2 * p), F32), pltpu.VMEM((g, 2 * p), F32)],
        compiler_params=_cparams(("arbitrary",), 48),
        name="s5_chunk_scan",
    )(hend3, hend3, a_re, a_im)
    yg = pl.pallas_call(
        _s5_out_kernel,
        grid=(g // S5_GPS,),
        in_specs=[pl.BlockSpec((S5_GPS, nc, th), lambda i: (i, 0, 0)),
                  pl.BlockSpec((S5_GPS, th, th), lambda i: (i, 0, 0)),
                  pl.BlockSpec((nc, S5_GPS * sw), lambda i: (0, i)),
                  pl.BlockSpec((nc, S5_GPS * sw), lambda i: (0, i)),
                  pl.BlockSpec((S5_GPS, sw, th), lambda i: (i, 0, 0))],
        out_specs=pl.BlockSpec((S5_GPS, nc, th), lambda i: (i, 0, 0)),
        out_shape=jax.ShapeDtypeStruct((g, nc, th), BF16),
        compiler_params=_cparams(("arbitrary",), 32),
        name="s5_output",
    )(ug, m, hf.reshape(nc, g * sw), hb.reshape(nc, g * sw), emat)
    return _s5_scatter(yg)


def _glu_kernel(y_ref, wv_ref, wg_ref, bv_ref, bg_ref, s_ref, o_ref):
    y = y_ref[...]
    val = jnp.dot(y, wv_ref[...], preferred_element_type=F32) + bv_ref[...]
    gt = jnp.dot(y, wg_ref[...], preferred_element_type=F32) + bg_ref[...]
    o_ref[...] = (val * _sigmoid(gt) * _silu(s_ref[...].astype(F32))).astype(o_ref.dtype)


def _glu(y, w_bf16, b, proj, *, s_col, tm=1024, tn=512):
    seq, kdim = y.shape
    width = w_bf16.shape[1] // 2
    assert s_col % tn == 0
    nj = width // tn
    b2 = b.reshape(1, 2 * width).astype(F32)
    return pl.pallas_call(
        _glu_kernel,
        grid=(seq // tm, nj),
        in_specs=[pl.BlockSpec((tm, kdim), lambda i, j: (i, 0)),
                  pl.BlockSpec((kdim, tn), lambda i, j: (0, j)),
                  pl.BlockSpec((kdim, tn), lambda i, j: (0, nj + j)),
                  pl.BlockSpec((1, tn), lambda i, j: (0, j)),
                  pl.BlockSpec((1, tn), lambda i, j: (0, nj + j)),
                  pl.BlockSpec((tm, tn), lambda i, j: (i, s_col // tn + j))],
        out_specs=pl.BlockSpec((tm, tn), lambda i, j: (i, j)),
        out_shape=jax.ShapeDtypeStruct((seq, width), BF16),
        compiler_params=_cparams(("arbitrary", "arbitrary"), 40),
        name="s5_glu",
    )(y, w_bf16, w_bf16, b2, b2, proj)


def _out_proj_kernel(a_ref, b_ref, wa_ref, wb_ref, x_ref, gate_ref, o_ref):
    acc = jnp.dot(a_ref[...], wa_ref[...], preferred_element_type=F32)
    acc = acc + jnp.dot(b_ref[...], wb_ref[...], preferred_element_type=F32)
    o_ref[...] = x_ref[...] + gate_ref[...] * acc


def _out_proj(oa, ob, w_bf16, x, gate, *, tm=1024, tn=512):
    seq, half = oa.shape
    d = w_bf16.shape[1]
    return pl.pallas_call(
        _out_proj_kernel,
        grid=(seq // tm, d // tn),
        in_specs=[pl.BlockSpec((tm, half), lambda i, j: (i, 0)),
                  pl.BlockSpec((tm, half), lambda i, j: (i, 0)),
                  pl.BlockSpec((half, tn), lambda i, j: (0, j)),
                  pl.BlockSpec((half, tn), lambda i, j: (1, j)),
                  pl.BlockSpec((tm, tn), lambda i, j: (i, j)),
                  pl.BlockSpec((1, tn), lambda i, j: (0, j))],
        out_specs=pl.BlockSpec((tm, tn), lambda i, j: (i, j)),
        out_shape=jax.ShapeDtypeStruct((seq, d), F32),
        compiler_params=_cparams(("arbitrary", "arbitrary"), 40),
        name="out_proj",
    )(oa, ob, w_bf16, w_bf16, x, gate)


def _na_bias_tables(rpb, rows):
    w = GRID_W
    nrb = rows // NA_QROWS
    assert rows >= NA_KROWS + NA_QROWS
    col = np.arange(w)
    col_start = np.clip(col - NA_COLS // 2, 0, w - NA_COLS)
    col_ok = (col[None, :] >= col_start[:, None]) & (col[None, :] < col_start[:, None] + NA_COLS)
    dc = np.clip(col[None, :] - col[:, None], -(NA_COLS - 1), NA_COLS - 1) + NA_COLS - 1
    onehot = (dc[None] == np.arange(2 * NA_COLS - 1)[:, None, None]).astype(np.float32)
    tt = jnp.einsum('hrd,dqk->hrqk', rpb.astype(F32) * math.log2(math.e), jnp.asarray(onehot),
                    precision=HIGHEST)
    tt = jnp.where(jnp.asarray(col_ok)[None, None], tt, NEG_INF)
    n_dr = 2 * NA_ROWS - 1
    tt = jnp.concatenate([tt, jnp.full((rpb.shape[0], 1, w, w), NEG_INF, F32)], axis=1)
    sel = np.zeros((3, NA_QROWS, NA_KROWS, n_dr + 1), np.float32)
    for ti, rb in enumerate((0, 1, nrb - 1)):
        ks = min(max(rb * NA_QROWS - NA_ROWS // 2, 0), rows - NA_KROWS)
        for rl in range(NA_QROWS):
            r = rb * NA_QROWS + rl
            rs = min(max(r - NA_ROWS // 2, 0), rows - NA_ROWS)
            for kl in range(NA_KROWS):
                kr = ks + kl
                sel[ti, rl, kl, kr - r + NA_ROWS - 1 if rs <= kr < rs + NA_ROWS else n_dr] = 1.0
    tab = jnp.einsum('trkd,hdqc->thrqkc', jnp.asarray(sel), tt, precision=HIGHEST)
    return tab.reshape(3, rpb.shape[0], NA_QROWS * w, NA_KROWS * w)


def _na_kernel(q_ref, k0_ref, k1_ref, k2_ref, v0_ref, v1_ref, v2_ref, b_ref, g_ref, o_ref):
    ones = jnp.ones((k0_ref.shape[0] * 3, HEAD_DIM), BF16)
    outs = []
    for h in range(C_HEADS):
        cols = slice(h * HEAD_DIM, (h + 1) * HEAD_DIM)
        k = jnp.concatenate([k0_ref[:, cols], k1_ref[:, cols], k2_ref[:, cols]], axis=0)
        v = jnp.concatenate([v0_ref[:, cols], v1_ref[:, cols], v2_ref[:, cols]], axis=0)
        s = lax.dot_general(q_ref[:, cols], k, (((1,), (1,)), ((), ())),
                            preferred_element_type=F32) + b_ref[0, h]
        m = jnp.max(s, axis=-1, keepdims=True)
        p = jnp.exp2(s - m).astype(BF16)
        pv = jnp.dot(p, jnp.concatenate([v, ones], axis=1), preferred_element_type=F32)
        o = pv[:, :HEAD_DIM] * (1.0 / pv[:, HEAD_DIM:])
        outs.append((o * _silu(g_ref[:, cols].astype(F32))).astype(o_ref.dtype))
    o_ref[...] = jnp.concatenate(outs, axis=1)


def _na_attention(proj, bias, *, q_col, k_col, v_col, g_col):
    seq = proj.shape[0]
    tq = NA_QROWS * GRID_W
    nrb = seq // tq
    nkb = NA_KROWS // NA_QROWS
    cw = C_HEADS * HEAD_DIM
    assert q_col % cw == 0 and k_col % cw == 0 and v_col % cw == 0 and g_col % cw == 0

    def kv_spec(col, off):
        return pl.BlockSpec((tq, cw), lambda rb: (jnp.clip(rb - 1, 0, nrb - nkb) + off, col // cw))

    btype = lambda rb: (jnp.where(rb == 0, 0, jnp.where(rb == nrb - 1, 2, 1)), 0, 0, 0)
    return pl.pallas_call(
        _na_kernel,
        grid=(nrb,),
        in_specs=[pl.BlockSpec((tq, cw), lambda rb: (rb, q_col // cw)),
                  kv_spec(k_col, 0), kv_spec(k_col, 1), kv_spec(k_col, 2),
                  kv_spec(v_col, 0), kv_spec(v_col, 1), kv_spec(v_col, 2),
                  pl.BlockSpec((1, C_HEADS, tq, nkb * tq), btype),
                  pl.BlockSpec((tq, cw), lambda rb: (rb, g_col // cw))],
        out_specs=pl.BlockSpec((tq, cw), lambda rb: (rb, 0)),
        out_shape=jax.ShapeDtypeStruct((seq, cw), BF16),
        compiler_params=_cparams(("arbitrary",), 48),
        name="na_attention",
    )(proj, proj, proj, proj, proj, proj, proj, bias, proj)


CONV_HALO = 8


def _conv_kernel(prev_ref, cur_ref, next_ref, w_ref, b_ref, o_ref):
    i = pl.program_id(0)
    tm = cur_ref.shape[0]
    prev = jnp.where(i == 0, 0.0, prev_ref[...].astype(F32))
    nxt = jnp.where(i == pl.num_programs(0) - 1, 0.0, next_ref[...].astype(F32))
    ext = jnp.concatenate([prev, cur_ref[...].astype(F32), nxt], axis=0)
    acc = jnp.zeros(cur_ref.shape, F32) + b_ref[...]
    for kk in range(SSD_CONV):
        start = CONV_HALO - SSD_CONV // 2 + kk
        acc = acc + ext[start:start + tm, :] * w_ref[kk:kk + 1, :]
    o_ref[...] = _silu(acc).astype(o_ref.dtype)


def _ssd_conv(proj, conv_w, conv_b, *, col, tm=512, tc=512):
    seq = proj.shape[0]
    ch = conv_w.shape[1]
    assert col % tc == 0 and ch % tc == 0
    nh = tm // CONV_HALO
    nblk = seq // CONV_HALO
    cb = col // tc
    return pl.pallas_call(
        _conv_kernel,
        grid=(seq // tm, ch // tc),
        in_specs=[pl.BlockSpec((CONV_HALO, tc), lambda i, j: (jnp.maximum(i * nh - 1, 0), cb + j)),
                  pl.BlockSpec((tm, tc), lambda i, j: (i, cb + j)),
                  pl.BlockSpec((CONV_HALO, tc),
                               lambda i, j: (jnp.minimum((i + 1) * nh, nblk - 1), cb + j)),
                  pl.BlockSpec((SSD_CONV, tc), lambda i, j: (0, j)),
                  pl.BlockSpec((1, tc), lambda i, j: (0, j))],
        out_specs=pl.BlockSpec((tm, tc), lambda i, j: (i, j)),
        out_shape=jax.ShapeDtypeStruct((seq, ch), BF16),
        compiler_params=_cparams(("arbitrary", "arbitrary"), 32),
        name="ssd_conv",
    )(proj, proj, proj, conv_w.astype(F32), conv_b.reshape(1, ch).astype(F32))


def _split_dot(a, b_bf16):
    hi = a.astype(BF16)
    lo = (a - hi.astype(F32)).astype(BF16)
    return (jnp.dot(hi, b_bf16, preferred_element_type=F32)
            + jnp.dot(lo, b_bf16, preferred_element_type=F32))


def _expand_heads(a, ex_bf16):
    return jnp.dot(a.astype(BF16), ex_bf16, preferred_element_type=F32)


def _ssd_direction(d, xs_ref, b_ref, c_ref, dt_ref, bias_ref, a_ref, ex, o_ref, st_scr):
    t = xs_ref.shape[0]
    gw = SSD_WIDTH // SSD_GROUPS
    hpg = SSD_HEADS // SSD_GROUPS
    row = lax.broadcasted_iota(jnp.int32, (t, t), 0)
    colm = lax.broadcasted_iota(jnp.int32, (t, t), 1)
    tri = (row >= colm) if d == 0 else (row <= colm)
    tri_b = tri.astype(BF16)

    z = dt_ref[...] + bias_ref[d]
    dt = jnp.maximum(z, 0.0) + jnp.log(1.0 + jnp.exp(-jnp.abs(z)))
    adt = dt * a_ref[d]
    a1 = adt.astype(BF16)
    r1 = adt - a1.astype(F32)
    a2 = r1.astype(BF16)
    a3 = (r1 - a2.astype(F32)).astype(BF16)
    r = (jnp.dot(tri_b, a1, preferred_element_type=F32)
         + jnp.dot(tri_b, a2, preferred_element_type=F32)
         + jnp.dot(tri_b, a3, preferred_element_type=F32))
    tot = r[t - 1:t, :] if d == 0 else r[0:1, :]
    dt_x = _expand_heads(dt, ex)
    er_x = _expand_heads(jnp.exp(r), ex)
    sd_x = _expand_heads(jnp.exp(tot - r), ex)
    et_x = _split_dot(jnp.exp(tot), ex)
    r_t = r.T

    xs = xs_ref[...].astype(F32)
    xd = xs * dt_x
    xd_b = xd.astype(BF16)
    xdd_b = (xd * sd_x).astype(BF16)

    y_parts = []
    for g in range(SSD_GROUPS):
        bg = b_ref[:, g * SSD_STATE:(g + 1) * SSD_STATE]
        cg = c_ref[:, g * SSD_STATE:(g + 1) * SSD_STATE]
        cb = lax.dot_general(cg, bg, (((1,), (1,)), ((), ())), preferred_element_type=F32)
        lanes = slice(g * gw, (g + 1) * gw)
        s_prev = st_scr[d, g]
        y_off = jnp.dot(cg, s_prev.astype(BF16), preferred_element_type=F32) * er_x[:, lanes]
        s_loc = lax.dot_general(bg, xdd_b[:, lanes], (((0,), (0,)), ((), ())),
                                preferred_element_type=F32)
        st_scr[d, g] = s_prev * et_x[:, lanes] + s_loc
        for hh in range(hpg):
            h = g * hpg + hh
            decay = jnp.exp(jnp.where(tri, r[:, h:h + 1] - r_t[h:h + 1, :], NEG_INF))
            sc = (cb * decay).astype(BF16)
            hl = slice(h * SSD_HEAD_DIM, (h + 1) * SSD_HEAD_DIM)
            y_parts.append(jnp.dot(sc, xd_b[:, hl], preferred_element_type=F32)
                           + y_off[:, hh * SSD_HEAD_DIM:(hh + 1) * SSD_HEAD_DIM])
    o_ref[...] = jnp.concatenate(y_parts, axis=-1).astype(o_ref.dtype)


def _ssd_scan_kernel(xf_ref, bf_ref, cf_ref, dtf_ref, xb_ref, bb_ref, cb_ref, dtb_ref,
                     bias_ref, a_ref, ex_ref, of_ref, ob_ref, st_scr):
    @pl.when(pl.program_id(0) == 0)
    def _():
        st_scr[...] = jnp.zeros(st_scr.shape, F32)

    ex = ex_ref[...]
    _ssd_direction(0, xf_ref, bf_ref, cf_ref, dtf_ref, bias_ref, a_ref, ex, of_ref, st_scr)
    _ssd_direction(1, xb_ref, bb_ref, cb_ref, dtb_ref, bias_ref, a_ref, ex, ob_ref, st_scr)


def _ssd_scan(conv, dt_raw, dt_bias, a_log):
    seq = conv.shape[0]
    t = SSD_CHUNK
    nc = seq // t
    a = -jnp.exp(a_log.astype(F32)).reshape(2, 1, SSD_HEADS)
    bias = dt_bias.astype(F32).reshape(2, 1, SSD_HEADS)
    ex = jnp.repeat(jnp.eye(SSD_HEADS, dtype=BF16), SSD_HEAD_DIM, axis=1)
    dt_f = dt_raw[:, :SSD_HEADS]
    dt_b = dt_raw[:, SSD_HEADS:2 * SSD_HEADS]
    nxb = SSD_WIDTH // SSD_BC
    fwd = lambda c: c
    bwd = lambda c: nc - 1 - c

    def chunk_specs(pos):
        return [pl.BlockSpec((t, SSD_WIDTH), lambda c: (pos(c), 0)),
                pl.BlockSpec((t, SSD_BC), lambda c: (pos(c), nxb)),
                pl.BlockSpec((t, SSD_BC), lambda c: (pos(c), nxb + 1)),
                pl.BlockSpec((t, SSD_HEADS), lambda c: (pos(c), 0))]

    const3 = pl.BlockSpec((2, 1, SSD_HEADS), lambda c: (0, 0, 0))
    return pl.pallas_call(
        _ssd_scan_kernel,
        grid=(nc,),
        in_specs=chunk_specs(fwd) + chunk_specs(bwd) + [
            const3, const3, pl.BlockSpec((SSD_HEADS, SSD_WIDTH), lambda c: (0, 0))],
        out_specs=[pl.BlockSpec((t, SSD_WIDTH), lambda c: (fwd(c), 0)),
                   pl.BlockSpec((t, SSD_WIDTH), lambda c: (bwd(c), 0))],
        out_shape=[jax.ShapeDtypeStruct((seq, SSD_WIDTH), BF16)] * 2,
        scratch_shapes=[pltpu.VMEM((2, SSD_GROUPS, SSD_STATE, SSD_WIDTH // SSD_GROUPS), F32)],
        compiler_params=_cparams(("arbitrary",), 40),
        name="ssd_scan",
    )(conv, conv, conv, dt_f, conv, conv, conv, dt_b, bias, a, ex)


def _gated_norm_kernel(yf_ref, yb_ref, xs_ref, z_ref, d_ref, w_ref, o_ref):
    y = (yf_ref[...].astype(F32) + yb_ref[...].astype(F32)
         + d_ref[...] * xs_ref[...].astype(F32))
    y = y * _silu(z_ref[...].astype(F32))
    ms = jnp.mean(y * y, axis=-1, keepdims=True)
    o_ref[...] = (y * lax.rsqrt(ms + EPS) * w_ref[...]).astype(o_ref.dtype)


def _gated_norm(y_f, y_b, conv, proj, d_x, norm_w, *, z_col, tm=512):
    seq = conv.shape[0]
    w = SSD_WIDTH
    assert z_col % w == 0
    return pl.pallas_call(
        _gated_norm_kernel,
        grid=(seq // tm,),
        in_specs=[pl.BlockSpec((tm, w), lambda i: (i, 0)),
                  pl.BlockSpec((tm, w), lambda i: (i, 0)),
                  pl.BlockSpec((tm, w), lambda i: (i, 0)),
                  pl.BlockSpec((tm, w), lambda i: (i, z_col // w)),
                  pl.BlockSpec((1, w), lambda i: (0, 0)),
                  pl.BlockSpec((1, w), lambda i: (0, 0))],
        out_specs=pl.BlockSpec((tm, w), lambda i: (i, 0)),
        out_shape=jax.ShapeDtypeStruct((seq, w), BF16),
        compiler_params=_cparams(("arbitrary",), 32),
        name="ssd_gated_norm",
    )(y_f, y_b, conv, proj, d_x, norm_w.reshape(1, w).astype(F32))


def _rope_tables(seq):
    rows = seq // GRID_W
    n_axis = HEAD_DIM // 4
    inv = ROPE_THETA ** (-np.arange(n_axis, dtype=np.float64) / n_axis)
    ang_r = np.arange(rows, dtype=np.float64)[:, None] * inv
    ang_c = np.arange(GRID_W, dtype=np.float64)[:, None] * inv

    def expand(fr, fc):
        fr = jnp.asarray(fr.astype(np.float32))
        fc = jnp.asarray(fc.astype(np.float32))
        tab = jnp.concatenate(
            [jnp.broadcast_to(fr[:, None, :], (rows, GRID_W, n_axis)),
             jnp.broadcast_to(fc[None, :, :], (rows, GRID_W, n_axis))], axis=-1)
        return tab.reshape(seq, 2 * n_axis)

    cos = expand(np.cos(ang_r), np.cos(ang_c))
    sin = expand(np.sin(ang_r), np.sin(ang_c))
    return jnp.concatenate([cos, cos], axis=-1), jnp.concatenate([-sin, sin], axis=-1)


def _deinterleave_perm():
    return np.concatenate([np.arange(0, HEAD_DIM, 2), np.arange(1, HEAD_DIM, 2)])


def _layer_attn_s5(x, c, norm_g, ada_w, ada_b, w_in, q_norm, k_norm, lam_re, lam_im, log_step,
                   b_re, b_im, c_re, c_im, s5_d, w_glu, b_glu, w_out):
    seq, d = x.shape
    shift, scale1p, gate = _ada_mod(c, ada_w, ada_b)
    aw = A_HEADS * HEAD_DIM
    akw = A_KV_HEADS * HEAD_DIM
    perm = _deinterleave_perm()
    nqk = (aw + akw) // HEAD_DIM
    colperm = (np.arange(nqk)[:, None] * HEAD_DIM + perm[None, :]).reshape(-1)
    w = jnp.concatenate([w_in[:, colperm], w_in[:, aw + akw:]], axis=1).astype(BF16)
    cos2, sin2 = _rope_tables(seq)
    t = IN_TN
    q_col, k_col, v_col = 0, aw, aw + akw
    g_col = aw + 2 * akw
    u_col = g_col + aw
    gb_col = u_col + d // 2
    roles = ((q_col // t, k_col // t, "q"), (k_col // t, v_col // t, "k"),
             (v_col // t, g_col // t, "plain"), (g_col // t, u_col // t, "silu"),
             (u_col // t, (gb_col + d // 2) // t, "plain"))
    (proj,) = _in_proj(x, norm_g, scale1p, shift, w, q_norm[perm], k_norm[perm], cos2, sin2,
                       roles=roles, rope=True, q_scale=HEAD_DIM ** -0.5 * math.log2(math.e),
                       has_aux=False)
    o_a = _gqa_attention(proj, q_col=q_col, k_col=k_col, v_col=v_col, g_col=g_col)
    tables = _s5_tables(lam_re, lam_im, log_step, b_re, b_im, c_re, c_im, s5_d)
    y = _s5_mixer(proj, u_col, tables)
    o_b = _glu(y, w_glu.astype(BF16), b_glu, proj, s_col=gb_col)
    return _out_proj(o_a, o_b, w_out.astype(BF16), x, gate)


def _layer_na_ssd(x, c, norm_g, ada_w, ada_b, w_in, q_norm, k_norm, rpb, conv_w, conv_b,
                  dt_bias, a_log, ssd_d, norm_w, w_out):
    seq, d = x.shape
    shift, scale1p, gate = _ada_mod(c, ada_w, ada_b)
    cw = C_HEADS * HEAD_DIM
    n_in = w_in.shape[1]
    t = IN_TN
    n_pad = -(-n_in // t) * t
    w = jnp.pad(w_in, ((0, 0), (0, n_pad - n_in))).astype(BF16)
    q_col, k_col, v_col, g_col, z_col = 0, cw, 2 * cw, 3 * cw, 4 * cw
    xbc_col = z_col + SSD_WIDTH
    dt_col = xbc_col + SSD_WIDTH + 2 * SSD_BC
    roles = ((q_col // t, k_col // t, "q"), (k_col // t, v_col // t, "k"),
             (v_col // t, dt_col // t, "plain"), (dt_col // t, n_pad // t, "aux"))
    dummy = jnp.zeros((seq, HEAD_DIM), F32)
    proj, dt_raw = _in_proj(x, norm_g, scale1p, shift, w, q_norm, k_norm, dummy, dummy,
                            roles=roles, rope=False, q_scale=HEAD_DIM ** -0.5 * math.log2(math.e),
                            has_aux=True)
    bias = _na_bias_tables(rpb, seq // GRID_W)
    o_c = _na_attention(proj, bias, q_col=q_col, k_col=k_col, v_col=v_col, g_col=g_col)
    conv = _ssd_conv(proj, conv_w, conv_b, col=xbc_col)
    y_f, y_b = _ssd_scan(conv, dt_raw, dt_bias, a_log)
    d_x = jnp.repeat(ssd_d.astype(F32), SSD_HEAD_DIM).reshape(1, SSD_WIDTH)
    o_d = _gated_norm(y_f, y_b, conv, proj, d_x, norm_w, z_col=z_col)
    return _out_proj(o_c, o_d, w_out.astype(BF16), x, gate)


def kernel(x, c, e_norm_g, e_ada_w, e_ada_b, e_w_in, e_q_norm, e_k_norm, s5_lam_re, s5_lam_im,
           s5_log_step, s5_b_re, s5_b_im, s5_c_re, s5_c_im, s5_d, s5_w_glu, s5_b_glu, e_w_out,
           o_norm_g, o_ada_w, o_ada_b, o_w_in, o_q_norm, o_k_norm, na_rpb, ssd_conv_w, ssd_conv_b,
           ssd_dt_bias, ssd_a_log, ssd_d, ssd_norm_w, o_w_out):
    assert x.shape[0] == 1
    h = x[0]
    h = _layer_attn_s5(h, c, e_norm_g[0], e_ada_w[0], e_ada_b[0], e_w_in[0], e_q_norm[0],
                       e_k_norm[0], s5_lam_re[0], s5_lam_im[0], s5_log_step[0], s5_b_re[0],
                       s5_b_im[0], s5_c_re[0], s5_c_im[0], s5_d[0], s5_w_glu[0], s5_b_glu[0],
                       e_w_out[0])
    h = _layer_na_ssd(h, c, o_norm_g[0], o_ada_w[0], o_ada_b[0], o_w_in[0], o_q_norm[0],
                      o_k_norm[0], na_rpb[0], ssd_conv_w[0], ssd_conv_b[0], ssd_dt_bias[0],
                      ssd_a_log[0], ssd_d[0], ssd_norm_w[0], o_w_out[0])
    return h[None]
```

```python
import functools
import math

import jax
import jax.numpy as jnp
import numpy as np
from jax import lax
from jax.experimental import pallas as pl
from jax.experimental.pallas import tpu as pltpu

F32 = jnp.float32
BF16 = jnp.bfloat16
HIGHEST = lax.Precision.HIGHEST

GRID_W = 64
HEAD_DIM = 128
EPS = 1e-6
NEG_INF = -1e30
ROPE_THETA = 10000.0

A_HEADS = 8
A_KV_HEADS = 2
A_GROUP = A_HEADS // A_KV_HEADS
S5_GROUP = 16
S5_GROUPS = 64
S5_STATE = 64
S5_CHUNK = 32
C_HEADS = 8
NA_ROWS = 8
NA_COLS = 16
NA_QROWS = 4
NA_KROWS = 12
SSD_HEADS = 16
SSD_HEAD_DIM = 64
SSD_GROUPS = 2
SSD_STATE = 128
SSD_CONV = 5
SSD_CHUNK = 128
SSD_WIDTH = SSD_HEADS * SSD_HEAD_DIM
SSD_BC = SSD_GROUPS * SSD_STATE

V7X_VMEM_BYTES = 64 * 1024 * 1024
MiB = 1024 * 1024


def _cparams(semantics, vmem_mib):
    assert vmem_mib * MiB < V7X_VMEM_BYTES
    return pltpu.CompilerParams(dimension_semantics=semantics, vmem_limit_bytes=vmem_mib * MiB)


def _silu(x):
    return x * (1.0 / (1.0 + jnp.exp(-x)))


def _sigmoid(x):
    return 1.0 / (1.0 + jnp.exp(-x))


def _ada_kernel(c_ref, w_ref, b_ref, o_ref):
    sc = _silu(c_ref[...])
    o_ref[...] = jnp.sum(w_ref[...] * sc, axis=0, keepdims=True) + b_ref[...]


def _ada_mod(c, w, b):
    d, n = w.shape
    tn = 512
    out = pl.pallas_call(
        _ada_kernel,
        grid=(n // tn,),
        in_specs=[pl.BlockSpec((d, 1), lambda j: (0, 0)),
                  pl.BlockSpec((d, tn), lambda j: (0, j)),
                  pl.BlockSpec((1, tn), lambda j: (0, j))],
        out_specs=pl.BlockSpec((1, tn), lambda j: (0, j)),
        out_shape=jax.ShapeDtypeStruct((1, n), F32),
        compiler_params=_cparams(("arbitrary",), 24),
        name="ada_mod",
    )(c.astype(F32).reshape(d, 1), w, b.reshape(1, n))
    shift, scale, gate = jnp.split(out, 3, axis=-1)
    return shift, 1.0 + scale, gate


IN_TN = 256
IN_ROW_CHUNK = 64


def _in_proj_kernel(x_ref, g_ref, sc_ref, sh_ref, w_ref, qn_ref, kn_ref, cos_ref, sin_ref,
                    *out_and_scratch, roles, rope, q_scale, has_aux):
    if has_aux:
        o_ref, aux_ref, h_scr = out_and_scratch
    else:
        o_ref, h_scr = out_and_scratch
        aux_ref = None
    j = pl.program_id(1)
    tm = x_ref.shape[0]

    @pl.when(j == 0)
    def _():
        def body(r, carry):
            rows = pl.ds(pl.multiple_of(r * IN_ROW_CHUNK, IN_ROW_CHUNK), IN_ROW_CHUNK)
            xf = x_ref[rows, :]
            ms = jnp.mean(xf * xf, axis=-1, keepdims=True)
            y = xf * lax.rsqrt(ms + EPS) * g_ref[...]
            h_scr[rows, :] = (y * sc_ref[...] + sh_ref[...]).astype(BF16)
            return carry
        lax.fori_loop(0, tm // IN_ROW_CHUNK, body, 0, unroll=2)

    acc = jnp.dot(h_scr[...], w_ref[...], preferred_element_type=F32)

    def head_norm(a, gain):
        ms = jnp.mean(a * a, axis=-1, keepdims=True)
        return a * lax.rsqrt(ms + EPS) * gain

    def qk_epilogue(gain_ref, scale):
        outs = []
        for h in range(IN_TN // HEAD_DIM):
            a = head_norm(acc[:, h * HEAD_DIM:(h + 1) * HEAD_DIM], gain_ref[...])
            if rope:
                a = a * cos_ref[...] + pltpu.roll(a, HEAD_DIM // 2, axis=1) * sin_ref[...]
            if scale != 1.0:
                a = a * scale
            outs.append(a)
        o_ref[...] = jnp.concatenate(outs, axis=-1).astype(o_ref.dtype)

    for lo, hi, role in roles:
        @pl.when((j >= lo) & (j < hi))
        def _(role=role):
            if role == "q":
                qk_epilogue(qn_ref, q_scale)
            elif role == "k":
                qk_epilogue(kn_ref, 1.0)
            elif role == "silu":
                o_ref[...] = _silu(acc).astype(o_ref.dtype)
            elif role == "plain":
                o_ref[...] = acc.astype(o_ref.dtype)
            elif role == "aux":
                o_ref[...] = acc.astype(o_ref.dtype)
                aux_ref[...] = acc
            else:
                raise ValueError(role)


def _in_proj(x, norm_g, scale1p, shift, w_bf16, q_gain, k_gain, cos2, sin2, *, roles, rope,
             q_scale, has_aux, tm=1024):
    seq, d = x.shape
    n = w_bf16.shape[1]
    assert seq % tm == 0 and n % IN_TN == 0
    assert roles[-1][1] == n // IN_TN
    row = lambda i, j: (i, 0)
    const = lambda i, j: (0, 0)
    out_shape = [jax.ShapeDtypeStruct((seq, n), BF16)]
    out_specs = [pl.BlockSpec((tm, IN_TN), lambda i, j: (i, j))]
    if has_aux:
        out_shape.append(jax.ShapeDtypeStruct((seq, IN_TN), F32))
        out_specs.append(pl.BlockSpec((tm, IN_TN), row))
    kern = functools.partial(_in_proj_kernel, roles=roles, rope=rope, q_scale=q_scale,
                             has_aux=has_aux)
    return pl.pallas_call(
        kern,
        grid=(seq // tm, n // IN_TN),
        in_specs=[pl.BlockSpec((tm, d), row),
                  pl.BlockSpec((1, d), const), pl.BlockSpec((1, d), const),
                  pl.BlockSpec((1, d), const),
                  pl.BlockSpec((d, IN_TN), lambda i, j: (0, j)),
                  pl.BlockSpec((1, HEAD_DIM), const), pl.BlockSpec((1, HEAD_DIM), const),
                  pl.BlockSpec((tm, HEAD_DIM), row), pl.BlockSpec((tm, HEAD_DIM), row)],
        out_specs=out_specs,
        out_shape=out_shape,
        scratch_shapes=[pltpu.VMEM((tm, d), BF16)],
        compiler_params=_cparams(("arbitrary", "arbitrary"), 48),
        name="in_proj_rope" if rope else "in_proj",
    )(x, norm_g.reshape(1, d), scale1p, shift, w_bf16, q_gain.reshape(1, HEAD_DIM),
      k_gain.reshape(1, HEAD_DIM), cos2, sin2)


GQA_TK = 1024


def _gqa_kernel(q_ref, k_ref, v_ref, g_ref, o_ref, acc_scr, m_scr, s_scr):
    tq = q_ref.shape[0]
    nk = k_ref.shape[0] // GQA_TK
    q_all = jnp.concatenate(
        [q_ref[:, h * HEAD_DIM:(h + 1) * HEAD_DIM] for h in range(A_GROUP)], axis=0)
    acc_scr[...] = jnp.zeros(acc_scr.shape, F32)
    m_scr[...] = jnp.full(m_scr.shape, -jnp.inf, F32)
    ones = jnp.ones((GQA_TK, HEAD_DIM), BF16)

    def key_rows(kc):
        return pl.ds(pl.multiple_of(kc * GQA_TK, GQA_TK), GQA_TK)

    def scores(kc):
        return lax.dot_general(q_all, k_ref[key_rows(kc), :], (((1,), (1,)), ((), ())),
                               preferred_element_type=F32)

    def softmax_pv(slot, kc):
        s = s_scr[slot]
        v1 = jnp.concatenate([v_ref[key_rows(kc), :], ones], axis=1)
        m_prev = m_scr[...]
        m_new = jnp.maximum(m_prev, jnp.max(s, axis=-1, keepdims=True))
        alpha = jnp.exp2(m_prev - m_new)
        p = jnp.concatenate(
            [jnp.exp2(s[:, j * HEAD_DIM:(j + 1) * HEAD_DIM] - m_new).astype(BF16)
             for j in range(GQA_TK // HEAD_DIM)], axis=1)
        pv = jnp.dot(p, v1, preferred_element_type=F32)
        acc_scr[...] = jnp.concatenate([alpha, alpha], axis=1) * acc_scr[...] + pv
        m_scr[...] = m_new

    s_scr[0] = scores(0)

    def pair(i, carry):
        kc = 2 * i
        s_scr[1] = scores(kc + 1)
        softmax_pv(0, kc)
        s_scr[0] = scores(kc + 2)
        softmax_pv(1, kc + 1)
        return carry

    lax.fori_loop(0, nk // 2 - 1, pair, 0)
    s_scr[1] = scores(nk - 1)
    softmax_pv(0, nk - 2)
    softmax_pv(1, nk - 1)
    for h in range(A_GROUP):
        cols = slice(h * HEAD_DIM, (h + 1) * HEAD_DIM)
        a = acc_scr[h * tq:(h + 1) * tq, :]
        o = a[:, :HEAD_DIM] * (1.0 / a[:, HEAD_DIM:])
        o_ref[:, cols] = (o * g_ref[:, cols].astype(F32)).astype(o_ref.dtype)


def _gqa_attention(proj, *, q_col, k_col, v_col, g_col, tq=256):
    seq = proj.shape[0]
    gw = A_GROUP * HEAD_DIM
    assert q_col % gw == 0 and g_col % gw == 0 and k_col % HEAD_DIM == 0 and v_col % HEAD_DIM == 0
    assert seq % (2 * GQA_TK) == 0 and seq % tq == 0
    return pl.pallas_call(
        _gqa_kernel,
        grid=(A_KV_HEADS, seq // tq),
        in_specs=[pl.BlockSpec((tq, gw), lambda kh, qi: (qi, q_col // gw + kh)),
                  pl.BlockSpec((seq, HEAD_DIM), lambda kh, qi: (0, k_col // HEAD_DIM + kh)),
                  pl.BlockSpec((seq, HEAD_DIM), lambda kh, qi: (0, v_col // HEAD_DIM + kh)),
                  pl.BlockSpec((tq, gw), lambda kh, qi: (qi, g_col // gw + kh))],
        out_specs=pl.BlockSpec((tq, gw), lambda kh, qi: (qi, kh)),
        out_shape=jax.ShapeDtypeStruct((seq, A_HEADS * HEAD_DIM), BF16),
        scratch_shapes=[pltpu.VMEM((A_GROUP * tq, 2 * HEAD_DIM), F32),
                        pltpu.VMEM((A_GROUP * tq, HEAD_DIM), F32),
                        pltpu.VMEM((2, A_GROUP * tq, GQA_TK), F32)],
        compiler_params=_cparams(("arbitrary", "arbitrary"), 56),
        name="gqa_attention",
    )(proj, proj, proj, proj)


def _s5_tables(lam_re, lam_im, log_step, b_re, b_im, c_re, c_im, s5_d):
    t = S5_CHUNK
    g, p, hh = S5_GROUPS, S5_STATE, S5_GROUP
    lr = lam_re.astype(F32)
    li = lam_im.astype(F32)
    dt = jnp.exp(log_step.astype(F32))[..., None]
    mag = jnp.exp(lr * dt)
    ab_re = mag * jnp.cos(li * dt)
    ab_im = mag * jnp.sin(li * dt)
    den = lr * lr + li * li
    num_re = ab_re - 1.0
    f_re = (num_re * lr + ab_im * li) / den
    f_im = (ab_im * lr - num_re * li) / den
    br = b_re.astype(F32)
    bi = b_im.astype(F32)
    bb_re = f_re[..., None] * br - f_im[..., None] * bi
    bb_im = f_re[..., None] * bi + f_im[..., None] * br
    kk = jnp.arange(t + 1, dtype=F32)[None, None, :, None]
    pmag = jnp.exp(kk * (lr * dt)[:, :, None, :])
    ang = kk * (li * dt)[:, :, None, :]
    pw_re = pmag * jnp.cos(ang)
    pw_im = pmag * jnp.sin(ang)
    bt_re = jnp.swapaxes(bb_re, -1, -2)[:, :, None]
    bt_im = jnp.swapaxes(bb_im, -1, -2)[:, :, None]
    pr = pw_re[:, :, :, None, :]
    pi = pw_im[:, :, :, None, :]
    wt_re = pr * bt_re - pi * bt_im
    wt_im = pr * bt_im + pi * bt_re
    cr = c_re.astype(F32)
    ci = c_im.astype(F32)
    cc_re = cr[:, :, None]
    cc_im = ci[:, :, None]
    cl_re = cc_re * pr - cc_im * pi
    cl_im = cc_re * pi + cc_im * pr
    bt_cat = jnp.concatenate([bt_re[:, :, 0], -bt_im[:, :, 0]], axis=-1)
    cl_cat = jnp.concatenate([cl_re, cl_im], axis=-1).reshape(2, g, (t + 1) * hh, 2 * p)
    kj = jnp.einsum('dgjq,dgnq->dgjn', bt_cat, cl_cat, precision=HIGHEST)
    kj = kj.reshape(2, g, hh, t + 1, hh)
    dmat = s5_d.astype(F32).reshape(g, hh)[:, :, None] * jnp.eye(hh, dtype=F32)[None]
    k0 = kj[0, :, :, 0] + kj[1, :, :, 0] + dmat
    kb = kj[1, :, :, 1:t][:, :, ::-1]
    kf = kj[0, :, :, 1:t]
    kflat = jnp.concatenate([kb, k0[:, :, None], kf], axis=2).reshape(g, hh, (2 * t - 1) * hh)
    m = jnp.stack([kflat[:, :, (t - 1 - tp) * hh:(2 * t - 1 - tp) * hh] for tp in range(t)],
                  axis=1).reshape(g, t * hh, t * hh)
    flat = lambda a: a.reshape(g, t * hh, p)
    fmat = jnp.concatenate([flat(wt_re[0, :, :t][:, ::-1]), flat(wt_re[1, :, :t]),
                            flat(wt_im[0, :, :t][:, ::-1]), flat(wt_im[1, :, :t])], axis=-1)
    e_t = lambda a: jnp.swapaxes(flat(a), 1, 2)
    emat = jnp.concatenate([e_t(cl_re[0, :, 1:t + 1]), e_t(cl_re[1, :, 1:t + 1][:, ::-1]),
                            -e_t(cl_im[0, :, 1:t + 1]), -e_t(cl_im[1, :, 1:t + 1][:, ::-1])],
                           axis=1)
    a_re = jnp.concatenate([pw_re[0, :, t], pw_re[1, :, t]], axis=-1)
    a_im = jnp.concatenate([pw_im[0, :, t], pw_im[1, :, t]], axis=-1)
    return m.astype(BF16), fmat.astype(BF16), emat.astype(BF16), a_re, a_im


S5_GPS = 4


def _s5_state_kernel(u_ref, f_ref, o_ref):
    sw = f_ref.shape[2]
    for a in range(S5_GPS):
        o_ref[:, a * sw:(a + 1) * sw] = jnp.dot(u_ref[a], f_ref[a], preferred_element_type=F32)


def _s5_fwd_lanes(shape):
    lane = lax.broadcasted_iota(jnp.int32, shape, len(shape) - 1)
    return (lane % (2 * S5_STATE)) < S5_STATE


def _s5_scan_kernel(xf_ref, xb_ref, ar_ref, ai_ref, hf_ref, hb_ref, re_scr, im_scr):
    cb = xf_ref.shape[0]
    w = 2 * S5_STATE

    @pl.when(pl.program_id(0) == 0)
    def _():
        re_scr[...] = jnp.zeros(re_scr.shape, F32)
        im_scr[...] = jnp.zeros(im_scr.shape, F32)

    ar = ar_ref[...]
    ai = ai_ref[...]
    fwd = _s5_fwd_lanes(ar.shape)

    def body(i, carry):
        re, im = carry
        j = cb - 1 - i
        st = jnp.concatenate([re, im], axis=-1).astype(hf_ref.dtype)
        hf_ref[i] = st
        hb_ref[j] = st
        xf = xf_ref[i]
        xb = xb_ref[j]
        x_re = jnp.where(fwd, xf[:, :w], xb[:, :w])
        x_im = jnp.where(fwd, xf[:, w:], xb[:, w:])
        return ar * re - ai * im + x_re, ar * im + ai * re + x_im

    re, im = lax.fori_loop(0, cb, body, (re_scr[...], im_scr[...]), unroll=2)
    re_scr[...] = re
    im_scr[...] = im


def _s5_out_kernel(u_ref, m_ref, hf_ref, hb_ref, e_ref, o_ref):
    sw = e_ref.shape[1]
    c0 = math.sqrt(2.0 / math.pi)
    for a in range(S5_GPS):
        lanes = slice(a * sw, (a + 1) * sw)
        hf = hf_ref[:, lanes]
        y = jnp.dot(u_ref[a], m_ref[a], preferred_element_type=F32)
        h = jnp.where(_s5_fwd_lanes(hf.shape), hf, hb_ref[:, lanes])
        y = y + jnp.dot(h, e_ref[a], preferred_element_type=F32)
        y = 0.5 * y * (1.0 + jnp.tanh(c0 * (y + 0.044715 * (y * y * y))))
        o_ref[a] = y.astype(o_ref.dtype)


S5_SUPER = 128 // S5_GROUP


def _block_swap_matrix():
    n = S5_SUPER
    idx = np.arange(n * n * S5_GROUP)
    b, a, j = idx // (n * S5_GROUP), (idx // S5_GROUP) % n, idx % S5_GROUP
    sel = np.zeros((idx.size, idx.size), np.float32)
    sel[idx, a * n * S5_GROUP + b * S5_GROUP + j] = 1.0
    return jnp.asarray(sel, BF16)


S5_TBLK = 16


def _s5_gather_kernel(x_ref, sel_ref, o_ref, rows_scr):
    nc = x_ref.shape[0]
    rows_scr[...] = x_ref[...].astype(F32).reshape(nc * S5_TBLK, 128)
    for q in range(S5_TBLK // S5_SUPER):
        lhs = jnp.concatenate(
            [rows_scr[pl.ds(S5_SUPER * q + b, nc, stride=S5_TBLK), :].astype(BF16)
             for b in range(S5_SUPER)], axis=1)
        out = jnp.dot(lhs, sel_ref[...], preferred_element_type=F32).astype(o_ref.dtype)
        for a in range(S5_SUPER):
            o_ref[a, :, q * 128:(q + 1) * 128] = out[:, a * 128:(a + 1) * 128]


def _s5_gather(proj, col):
    seq, n = proj.shape
    t, g = S5_CHUNK, S5_GROUPS
    nc = seq // t
    assert n % 128 == 0 and col % 128 == 0 and t % S5_TBLK == 0
    lanes = S5_TBLK * S5_GROUP
    return pl.pallas_call(
        _s5_gather_kernel,
        grid=(g // S5_SUPER, t // S5_TBLK),
        in_specs=[pl.BlockSpec((nc, S5_TBLK, 128), lambda sg, h: (0, h, col // 128 + sg)),
                  pl.BlockSpec((1024, 1024), lambda sg, h: (0, 0))],
        out_specs=pl.BlockSpec((S5_SUPER, nc, lanes), lambda sg, h: (sg, 0, h)),
        out_shape=jax.ShapeDtypeStruct((g, nc, t * S5_GROUP), BF16),
        scratch_shapes=[pltpu.VMEM((nc * S5_TBLK, 128), F32)],
        compiler_params=_cparams(("arbitrary", "arbitrary"), 40),
        name="s5_gather",
    )(proj.reshape(nc, t, n), _block_swap_matrix())


def _s5_scatter_kernel(y_ref, sel_ref, o_ref, rows_scr):
    nc = y_ref.shape[1]
    for q in range(S5_TBLK // S5_SUPER):
        lhs = jnp.concatenate([y_ref[a, :, q * 128:(q + 1) * 128] for a in range(S5_SUPER)],
                              axis=1)
        out = jnp.dot(lhs, sel_ref[...], preferred_element_type=F32)
        for b in range(S5_SUPER):
            rows_scr[pl.ds(S5_SUPER * q + b, nc, stride=S5_TBLK), :] = out[:, b * 128:(b + 1) * 128]
    o_ref[...] = rows_scr[...].reshape(nc, S5_TBLK, 128).astype(o_ref.dtype)


def _s5_scatter(yg):
    g, nc, th = yg.shape
    t = S5_CHUNK
    width = g * S5_GROUP
    lanes = S5_TBLK * S5_GROUP
    y3 = pl.pallas_call(
        _s5_scatter_kernel,
        grid=(g // S5_SUPER, t // S5_TBLK),
        in_specs=[pl.BlockSpec((S5_SUPER, nc, lanes), lambda sg, h: (sg, 0, h)),
                  pl.BlockSpec((1024, 1024), lambda sg, h: (0, 0))],
        out_specs=pl.BlockSpec((nc, S5_TBLK, 128), lambda sg, h: (0, h, sg)),
        out_shape=jax.ShapeDtypeStruct((nc, t, width), BF16),
        scratch_shapes=[pltpu.VMEM((nc * S5_TBLK, 128), F32)],
        compiler_params=_cparams(("arbitrary", "arbitrary"), 40),
        name="s5_scatter",
    )(yg, _block_swap_matrix())
    return y3.reshape(nc * t, width)


def _s5_mixer(proj, u_col, tables):
    m, fmat, emat, a_re, a_im = tables
    seq = proj.shape[0]
    t, g, hh, p = S5_CHUNK, S5_GROUPS, S5_GROUP, S5_STATE
    nc = seq // t
    th = t * hh
    sw = 4 * p
    ug = _s5_gather(proj, u_col)
    hend = pl.pallas_call(
        _s5_state_kernel,
        grid=(g // S5_GPS,),
        in_specs=[pl.BlockSpec((S5_GPS, nc, th), lambda i: (i, 0, 0)),
                  pl.BlockSpec((S5_GPS, th, sw), lambda i: (i, 0, 0))],
        out_specs=pl.BlockSpec((nc, S5_GPS * sw), lambda i: (0, i)),
        out_shape=jax.ShapeDtypeStruct((nc, g * sw), F32),
        compiler_params=_cparams(("arbitrary",), 32),
        name="s5_chunk_state",
    )(ug, fmat)
    hend3 = hend.reshape(nc, g, sw)
    cb = min(nc, 64)
    nb = nc // cb
    fwd_blk = lambda c: (c, 0, 0)
    bwd_blk = lambda c: (nb - 1 - c, 0, 0)
    hf, hb = pl.pallas_call(
        _s5_scan_kernel,
        grid=(nb,),
        in_specs=[pl.BlockSpec((cb, g, sw), fwd_blk), pl.BlockSpec((cb, g, sw), bwd_blk),
                  pl.BlockSpec((g, 2 * p), lambda c: (0, 0)),
                  pl.BlockSpec((g, 2 * p), lambda c: (0, 0))],
        out_specs=[pl.BlockSpec((cb, g, sw), fwd_blk), pl.BlockSpec((cb, g, sw), bwd_blk)],
        out_shape=[jax.ShapeDtypeStruct((nc, g, sw), BF16)] * 2,
        scratch_shapes=[pltpu.VMEM((g, 2 * p), F32), pltpu.VMEM((g, 2 * p), F32)],
        compiler_params=_cparams(("arbitrary",), 48),
        name="s5_chunk_scan",
    )(hend3, hend3, a_re, a_im)
    yg = pl.pallas_call(
        _s5_out_kernel,
        grid=(g // S5_GPS,),
        in_specs=[pl.BlockSpec((S5_GPS, nc, th), lambda i: (i, 0, 0)),
                  pl.BlockSpec((S5_GPS, th, th), lambda i: (i, 0, 0)),
                  pl.BlockSpec((nc, S5_GPS * sw), lambda i: (0, i)),
                  pl.BlockSpec((nc, S5_GPS * sw), lambda i: (0, i)),
                  pl.BlockSpec((S5_GPS, sw, th), lambda i: (i, 0, 0))],
        out_specs=pl.BlockSpec((S5_GPS, nc, th), lambda i: (i, 0, 0)),
        out_shape=jax.ShapeDtypeStruct((g, nc, th), BF16),
        compiler_params=_cparams(("arbitrary",), 32),
        name="s5_output",
    )(ug, m, hf.reshape(nc, g * sw), hb.reshape(nc, g * sw), emat)
    return _s5_scatter(yg)


def _glu_kernel(y_ref, wv_ref, wg_ref, bv_ref, bg_ref, s_ref, o_ref):
    y = y_ref[...]
    val = jnp.dot(y, wv_ref[...], preferred_element_type=F32) + bv_ref[...]
    gt = jnp.dot(y, wg_ref[...], preferred_element_type=F32) + bg_ref[...]
    o_ref[...] = (val * _sigmoid(gt) * _silu(s_ref[...].astype(F32))).astype(o_ref.dtype)


def _glu(y, w_bf16, b, proj, *, s_col, tm=1024, tn=512):
    seq, kdim = y.shape
    width = w_bf16.shape[1] // 2
    assert s_col % tn == 0
    nj = width // tn
    b2 = b.reshape(1, 2 * width).astype(F32)
    return pl.pallas_call(
        _glu_kernel,
        grid=(seq // tm, nj),
        in_specs=[pl.BlockSpec((tm, kdim), lambda i, j: (i, 0)),
                  pl.BlockSpec((kdim, tn), lambda i, j: (0, j)),
                  pl.BlockSpec((kdim, tn), lambda i, j: (0, nj + j)),
                  pl.BlockSpec((1, tn), lambda i, j: (0, j)),
                  pl.BlockSpec((1, tn), lambda i, j: (0, nj + j)),
                  pl.BlockSpec((tm, tn), lambda i, j: (i, s_col // tn + j))],
        out_specs=pl.BlockSpec((tm, tn), lambda i, j: (i, j)),
        out_shape=jax.ShapeDtypeStruct((seq, width), BF16),
        compiler_params=_cparams(("arbitrary", "arbitrary"), 40),
        name="s5_glu",
    )(y, w_bf16, w_bf16, b2, b2, proj)


def _out_proj_kernel(a_ref, b_ref, wa_ref, wb_ref, x_ref, gate_ref, o_ref):
    acc = jnp.dot(a_ref[...], wa_ref[...], preferred_element_type=F32)
    acc = acc + jnp.dot(b_ref[...], wb_ref[...], preferred_element_type=F32)
    o_ref[...] = x_ref[...] + gate_ref[...] * acc


def _out_proj(oa, ob, w_bf16, x, gate, *, tm=512, tn=2048):
    seq, half = oa.shape
    d = w_bf16.shape[1]
    tn = min(tn, d)
    return pl.pallas_call(
        _out_proj_kernel,
        grid=(seq // tm, d // tn),
        in_specs=[pl.BlockSpec((tm, half), lambda i, j: (i, 0)),
                  pl.BlockSpec((tm, half), lambda i, j: (i, 0)),
                  pl.BlockSpec((half, tn), lambda i, j: (0, j)),
                  pl.BlockSpec((half, tn), lambda i, j: (1, j)),
                  pl.BlockSpec((tm, tn), lambda i, j: (i, j)),
                  pl.BlockSpec((1, tn), lambda i, j: (0, j))],
        out_specs=pl.BlockSpec((tm, tn), lambda i, j: (i, j)),
        out_shape=jax.ShapeDtypeStruct((seq, d), F32),
        compiler_params=_cparams(("arbitrary", "arbitrary"), 48),
        name="out_proj",
    )(oa, ob, w_bf16, w_bf16, x, gate)


def _na_bias_tables(rpb, rows):
    w = GRID_W
    nrb = rows // NA_QROWS
    assert rows >= NA_KROWS + NA_QROWS
    col = np.arange(w)
    col_start = np.clip(col - NA_COLS // 2, 0, w - NA_COLS)
    col_ok = (col[None, :] >= col_start[:, None]) & (col[None, :] < col_start[:, None] + NA_COLS)
    dc = np.clip(col[None, :] - col[:, None], -(NA_COLS - 1), NA_COLS - 1) + NA_COLS - 1
    onehot = (dc[None] == np.arange(2 * NA_COLS - 1)[:, None, None]).astype(np.float32)
    tt = jnp.einsum('hrd,dqk->hrqk', rpb.astype(F32) * math.log2(math.e), jnp.asarray(onehot),
                    precision=HIGHEST)
    tt = jnp.where(jnp.asarray(col_ok)[None, None], tt, NEG_INF)
    n_dr = 2 * NA_ROWS - 1
    tt = jnp.concatenate([tt, jnp.full((rpb.shape[0], 1, w, w), NEG_INF, F32)], axis=1)
    sel = np.zeros((3, NA_QROWS, NA_KROWS, n_dr + 1), np.float32)
    for ti, rb in enumerate((0, 1, nrb - 1)):
        ks = min(max(rb * NA_QROWS - NA_ROWS // 2, 0), rows - NA_KROWS)
        for rl in range(NA_QROWS):
            r = rb * NA_QROWS + rl
            rs = min(max(r - NA_ROWS // 2, 0), rows - NA_ROWS)
            for kl in range(NA_KROWS):
                kr = ks + kl
                sel[ti, rl, kl, kr - r + NA_ROWS - 1 if rs <= kr < rs + NA_ROWS else n_dr] = 1.0
    tab = jnp.einsum('trkd,hdqc->thrqkc', jnp.asarray(sel), tt, precision=HIGHEST)
    return tab.reshape(3, rpb.shape[0], NA_QROWS * w, NA_KROWS * w)


def _na_kernel(q_ref, k0_ref, k1_ref, k2_ref, v0_ref, v1_ref, v2_ref, b_ref, g_ref, o_ref):
    ones = jnp.ones((k0_ref.shape[0] * 3, HEAD_DIM), BF16)
    outs = []
    for h in range(C_HEADS):
        cols = slice(h * HEAD_DIM, (h + 1) * HEAD_DIM)
        k = jnp.concatenate([k0_ref[:, cols], k1_ref[:, cols], k2_ref[:, cols]], axis=0)
        v = jnp.concatenate([v0_ref[:, cols], v1_ref[:, cols], v2_ref[:, cols]], axis=0)
        s = lax.dot_general(q_ref[:, cols], k, (((1,), (1,)), ((), ())),
                            preferred_element_type=F32) + b_ref[0, h]
        m = jnp.max(s, axis=-1, keepdims=True)
        p = jnp.exp2(s - m).astype(BF16)
        pv = jnp.dot(p, jnp.concatenate([v, ones], axis=1), preferred_element_type=F32)
        o = pv[:, :HEAD_DIM] * (1.0 / pv[:, HEAD_DIM:])
        outs.append((o * _silu(g_ref[:, cols].astype(F32))).astype(o_ref.dtype))
    o_ref[...] = jnp.concatenate(outs, axis=1)


def _na_attention(proj, bias, *, q_col, k_col, v_col, g_col):
    seq = proj.shape[0]
    tq = NA_QROWS * GRID_W
    nrb = seq // tq
    nkb = NA_KROWS // NA_QROWS
    cw = C_HEADS * HEAD_DIM
    assert q_col % cw == 0 and k_col % cw == 0 and v_col % cw == 0 and g_col % cw == 0

    def kv_spec(col, off):
        return pl.BlockSpec((tq, cw), lambda rb: (jnp.clip(rb - 1, 0, nrb - nkb) + off, col // cw))

    btype = lambda rb: (jnp.where(rb == 0, 0, jnp.where(rb == nrb - 1, 2, 1)), 0, 0, 0)
    return pl.pallas_call(
        _na_kernel,
        grid=(nrb,),
        in_specs=[pl.BlockSpec((tq, cw), lambda rb: (rb, q_col // cw)),
                  kv_spec(k_col, 0), kv_spec(k_col, 1), kv_spec(k_col, 2),
                  kv_spec(v_col, 0), kv_spec(v_col, 1), kv_spec(v_col, 2),
                  pl.BlockSpec((1, C_HEADS, tq, nkb * tq), btype),
                  pl.BlockSpec((tq, cw), lambda rb: (rb, g_col // cw))],
        out_specs=pl.BlockSpec((tq, cw), lambda rb: (rb, 0)),
        out_shape=jax.ShapeDtypeStruct((seq, cw), BF16),
        compiler_params=_cparams(("arbitrary",), 48),
        name="na_attention",
    )(proj, proj, proj, proj, proj, proj, proj, bias, proj)


CONV_HALO = 8


def _conv_kernel(prev_ref, cur_ref, next_ref, w_ref, b_ref, o_ref):
    i = pl.program_id(0)
    tm = cur_ref.shape[0]
    prev = jnp.where(i == 0, 0.0, prev_ref[...].astype(F32))
    nxt = jnp.where(i == pl.num_programs(0) - 1, 0.0, next_ref[...].astype(F32))
    ext = jnp.concatenate([prev, cur_ref[...].astype(F32), nxt], axis=0)
    acc = jnp.zeros(cur_ref.shape, F32) + b_ref[...]
    for kk in range(SSD_CONV):
        start = CONV_HALO - SSD_CONV // 2 + kk
        acc = acc + ext[start:start + tm, :] * w_ref[kk:kk + 1, :]
    o_ref[...] = _silu(acc).astype(o_ref.dtype)


def _ssd_conv(proj, conv_w, conv_b, *, col, tm=512, tc=512):
    seq = proj.shape[0]
    ch = conv_w.shape[1]
    assert col % tc == 0 and ch % tc == 0
    nh = tm // CONV_HALO
    nblk = seq // CONV_HALO
    cb = col // tc
    return pl.pallas_call(
        _conv_kernel,
        grid=(seq // tm, ch // tc),
        in_specs=[pl.BlockSpec((CONV_HALO, tc), lambda i, j: (jnp.maximum(i * nh - 1, 0), cb + j)),
                  pl.BlockSpec((tm, tc), lambda i, j: (i, cb + j)),
                  pl.BlockSpec((CONV_HALO, tc),
                               lambda i, j: (jnp.minimum((i + 1) * nh, nblk - 1), cb + j)),
                  pl.BlockSpec((SSD_CONV, tc), lambda i, j: (0, j)),
                  pl.BlockSpec((1, tc), lambda i, j: (0, j))],
        out_specs=pl.BlockSpec((tm, tc), lambda i, j: (i, j)),
        out_shape=jax.ShapeDtypeStruct((seq, ch), BF16),
        compiler_params=_cparams(("arbitrary", "arbitrary"), 32),
        name="ssd_conv",
    )(proj, proj, proj, conv_w.astype(F32), conv_b.reshape(1, ch).astype(F32))


def _split_dot(a, b_bf16):
    hi = a.astype(BF16)
    lo = (a - hi.astype(F32)).astype(BF16)
    return (jnp.dot(hi, b_bf16, preferred_element_type=F32)
            + jnp.dot(lo, b_bf16, preferred_element_type=F32))


def _expand_heads(a, ex_bf16):
    return jnp.dot(a.astype(BF16), ex_bf16, preferred_element_type=F32)


def _ssd_direction(d, xs_ref, b_ref, c_ref, dt_ref, bias_ref, a_ref, ex, o_ref, st_scr):
    t = xs_ref.shape[0]
    gw = SSD_WIDTH // SSD_GROUPS
    hpg = SSD_HEADS // SSD_GROUPS
    row = lax.broadcasted_iota(jnp.int32, (t, t), 0)
    colm = lax.broadcasted_iota(jnp.int32, (t, t), 1)
    tri = (row >= colm) if d == 0 else (row <= colm)
    tri_b = tri.astype(BF16)

    z = dt_ref[...] + bias_ref[d]
    dt = jnp.maximum(z, 0.0) + jnp.log(1.0 + jnp.exp(-jnp.abs(z)))
    adt = dt * a_ref[d]
    a1 = adt.astype(BF16)
    r1 = adt - a1.astype(F32)
    a2 = r1.astype(BF16)
    a3 = (r1 - a2.astype(F32)).astype(BF16)
    r = (jnp.dot(tri_b, a1, preferred_element_type=F32)
         + jnp.dot(tri_b, a2, preferred_element_type=F32)
         + jnp.dot(tri_b, a3, preferred_element_type=F32))
    tot = r[t - 1:t, :] if d == 0 else r[0:1, :]
    dt_x = _expand_heads(dt, ex)
    er_x = _expand_heads(jnp.exp(r), ex)
    sd_x = _expand_heads(jnp.exp(tot - r), ex)
    et_x = _split_dot(jnp.exp(tot), ex)
    r_t = r.T

    xs = xs_ref[...].astype(F32)
    xd = xs * dt_x
    xd_b = xd.astype(BF16)
    xdd_b = (xd * sd_x).astype(BF16)

    y_parts = []
    for g in range(SSD_GROUPS):
        bg = b_ref[:, g * SSD_STATE:(g + 1) * SSD_STATE]
        cg = c_ref[:, g * SSD_STATE:(g + 1) * SSD_STATE]
        cb = lax.dot_general(cg, bg, (((1,), (1,)), ((), ())), preferred_element_type=F32)
        lanes = slice(g * gw, (g + 1) * gw)
        s_prev = st_scr[d, g]
        y_off = jnp.dot(cg, s_prev.astype(BF16), preferred_element_type=F32) * er_x[:, lanes]
        s_loc = lax.dot_general(bg, xdd_b[:, lanes], (((0,), (0,)), ((), ())),
                                preferred_element_type=F32)
        st_scr[d, g] = s_prev * et_x[:, lanes] + s_loc
        for hh in range(hpg):
            h = g * hpg + hh
            decay = jnp.exp(jnp.where(tri, r[:, h:h + 1] - r_t[h:h + 1, :], NEG_INF))
            sc = (cb * decay).astype(BF16)
            hl = slice(h * SSD_HEAD_DIM, (h + 1) * SSD_HEAD_DIM)
            y_parts.append(jnp.dot(sc, xd_b[:, hl], preferred_element_type=F32)
                           + y_off[:, hh * SSD_HEAD_DIM:(hh + 1) * SSD_HEAD_DIM])
    o_ref[...] = jnp.concatenate(y_parts, axis=-1).astype(o_ref.dtype)


def _ssd_scan_kernel(xf_ref, bf_ref, cf_ref, dtf_ref, xb_ref, bb_ref, cb_ref, dtb_ref,
                     bias_ref, a_ref, ex_ref, of_ref, ob_ref, st_scr):
    @pl.when(pl.program_id(0) == 0)
    def _():
        st_scr[...] = jnp.zeros(st_scr.shape, F32)

    ex = ex_ref[...]
    _ssd_direction(0, xf_ref, bf_ref, cf_ref, dtf_ref, bias_ref, a_ref, ex, of_ref, st_scr)
    _ssd_direction(1, xb_ref, bb_ref, cb_ref, dtb_ref, bias_ref, a_ref, ex, ob_ref, st_scr)


def _ssd_scan(conv, dt_raw, dt_bias, a_log):
    seq = conv.shape[0]
    t = SSD_CHUNK
    nc = seq // t
    a = -jnp.exp(a_log.astype(F32)).reshape(2, 1, SSD_HEADS)
    bias = dt_bias.astype(F32).reshape(2, 1, SSD_HEADS)
    ex = jnp.repeat(jnp.eye(SSD_HEADS, dtype=BF16), SSD_HEAD_DIM, axis=1)
    dt_f = dt_raw[:, :SSD_HEADS]
    dt_b = dt_raw[:, SSD_HEADS:2 * SSD_HEADS]
    nxb = SSD_WIDTH // SSD_BC
    fwd = lambda c: c
    bwd = lambda c: nc - 1 - c

    def chunk_specs(pos):
        return [pl.BlockSpec((t, SSD_WIDTH), lambda c: (pos(c), 0)),
                pl.BlockSpec((t, SSD_BC), lambda c: (pos(c), nxb)),
                pl.BlockSpec((t, SSD_BC), lambda c: (pos(c), nxb + 1)),
                pl.BlockSpec((t, SSD_HEADS), lambda c: (pos(c), 0))]

    const3 = pl.BlockSpec((2, 1, SSD_HEADS), lambda c: (0, 0, 0))
    return pl.pallas_call(
        _ssd_scan_kernel,
        grid=(nc,),
        in_specs=chunk_specs(fwd) + chunk_specs(bwd) + [
            const3, const3, pl.BlockSpec((SSD_HEADS, SSD_WIDTH), lambda c: (0, 0))],
        out_specs=[pl.BlockSpec((t, SSD_WIDTH), lambda c: (fwd(c), 0)),
                   pl.BlockSpec((t, SSD_WIDTH), lambda c: (bwd(c), 0))],
        out_shape=[jax.ShapeDtypeStruct((seq, SSD_WIDTH), BF16)] * 2,
        scratch_shapes=[pltpu.VMEM((2, SSD_GROUPS, SSD_STATE, SSD_WIDTH // SSD_GROUPS), F32)],
        compiler_params=_cparams(("arbitrary",), 40),
        name="ssd_scan",
    )(conv, conv, conv, dt_f, conv, conv, conv, dt_b, bias, a, ex)


def _gated_norm_kernel(yf_ref, yb_ref, xs_ref, z_ref, d_ref, w_ref, o_ref):
    y = (yf_ref[...].astype(F32) + yb_ref[...].astype(F32)
         + d_ref[...] * xs_ref[...].astype(F32))
    y = y * _silu(z_ref[...].astype(F32))
    ms = jnp.mean(y * y, axis=-1, keepdims=True)
    o_ref[...] = (y * lax.rsqrt(ms + EPS) * w_ref[...]).astype(o_ref.dtype)


def _gated_norm(y_f, y_b, conv, proj, d_x, norm_w, *, z_col, tm=512):
    seq = conv.shape[0]
    w = SSD_WIDTH
    assert z_col % w == 0
    return pl.pallas_call(
        _gated_norm_kernel,
        grid=(seq // tm,),
        in_specs=[pl.BlockSpec((tm, w), lambda i: (i, 0)),
                  pl.BlockSpec((tm, w), lambda i: (i, 0)),
                  pl.BlockSpec((tm, w), lambda i: (i, 0)),
                  pl.BlockSpec((tm, w), lambda i: (i, z_col // w)),
                  pl.BlockSpec((1, w), lambda i: (0, 0)),
                  pl.BlockSpec((1, w), lambda i: (0, 0))],
        out_specs=pl.BlockSpec((tm, w), lambda i: (i, 0)),
        out_shape=jax.ShapeDtypeStruct((seq, w), BF16),
        compiler_params=_cparams(("arbitrary",), 32),
        name="ssd_gated_norm",
    )(y_f, y_b, conv, proj, d_x, norm_w.reshape(1, w).astype(F32))


def _rope_tables(seq):
    rows = seq // GRID_W
    n_axis = HEAD_DIM // 4
    inv = ROPE_THETA ** (-np.arange(n_axis, dtype=np.float64) / n_axis)
    ang_r = np.arange(rows, dtype=np.float64)[:, None] * inv
    ang_c = np.arange(GRID_W, dtype=np.float64)[:, None] * inv

    def expand(fr, fc):
        fr = jnp.asarray(fr.astype(np.float32))
        fc = jnp.asarray(fc.astype(np.float32))
        tab = jnp.concatenate(
            [jnp.broadcast_to(fr[:, None, :], (rows, GRID_W, n_axis)),
             jnp.broadcast_to(fc[None, :, :], (rows, GRID_W, n_axis))], axis=-1)
        return tab.reshape(seq, 2 * n_axis)

    cos = expand(np.cos(ang_r), np.cos(ang_c))
    sin = expand(np.sin(ang_r), np.sin(ang_c))
    return jnp.concatenate([cos, cos], axis=-1), jnp.concatenate([-sin, sin], axis=-1)


def _deinterleave_perm():
    return np.concatenate([np.arange(0, HEAD_DIM, 2), np.arange(1, HEAD_DIM, 2)])


def _layer_attn_s5(x, c, norm_g, ada_w, ada_b, w_in, q_norm, k_norm, lam_re, lam_im, log_step,
                   b_re, b_im, c_re, c_im, s5_d, w_glu, b_glu, w_out):
    seq, d = x.shape
    shift, scale1p, gate = _ada_mod(c, ada_w, ada_b)
    aw = A_HEADS * HEAD_DIM
    akw = A_KV_HEADS * HEAD_DIM
    perm = _deinterleave_perm()
    nqk = (aw + akw) // HEAD_DIM
    colperm = (np.arange(nqk)[:, None] * HEAD_DIM + perm[None, :]).reshape(-1)
    w = jnp.concatenate([w_in[:, colperm], w_in[:, aw + akw:]], axis=1).astype(BF16)
    cos2, sin2 = _rope_tables(seq)
    t = IN_TN
    q_col, k_col, v_col = 0, aw, aw + akw
    g_col = aw + 2 * akw
    u_col = g_col + aw
    gb_col = u_col + d // 2
    roles = ((q_col // t, k_col // t, "q"), (k_col // t, v_col // t, "k"),
             (v_col // t, g_col // t, "plain"), (g_col // t, u_col // t, "silu"),
             (u_col // t, (gb_col + d // 2) // t, "plain"))
    (proj,) = _in_proj(x, norm_g, scale1p, shift, w, q_norm[perm], k_norm[perm], cos2, sin2,
                       roles=roles, rope=True, q_scale=HEAD_DIM ** -0.5 * math.log2(math.e),
                       has_aux=False)
    o_a = _gqa_attention(proj, q_col=q_col, k_col=k_col, v_col=v_col, g_col=g_col)
    tables = _s5_tables(lam_re, lam_im, log_step, b_re, b_im, c_re, c_im, s5_d)
    y = _s5_mixer(proj, u_col, tables)
    o_b = _glu(y, w_glu.astype(BF16), b_glu, proj, s_col=gb_col)
    return _out_proj(o_a, o_b, w_out.astype(BF16), x, gate)


def _layer_na_ssd(x, c, norm_g, ada_w, ada_b, w_in, q_norm, k_norm, rpb, conv_w, conv_b,
                  dt_bias, a_log, ssd_d, norm_w, w_out):
    seq, d = x.shape
    shift, scale1p, gate = _ada_mod(c, ada_w, ada_b)
    cw = C_HEADS * HEAD_DIM
    n_in = w_in.shape[1]
    t = IN_TN
    n_pad = -(-n_in // t) * t
    w = jnp.pad(w_in, ((0, 0), (0, n_pad - n_in))).astype(BF16)
    q_col, k_col, v_col, g_col, z_col = 0, cw, 2 * cw, 3 * cw, 4 * cw
    xbc_col = z_col + SSD_WIDTH
    dt_col = xbc_col + SSD_WIDTH + 2 * SSD_BC
    roles = ((q_col // t, k_col // t, "q"), (k_col // t, v_col // t, "k"),
             (v_col // t, dt_col // t, "plain"), (dt_col // t, n_pad // t, "aux"))
    dummy = jnp.zeros((seq, HEAD_DIM), F32)
    proj, dt_raw = _in_proj(x, norm_g, scale1p, shift, w, q_norm, k_norm, dummy, dummy,
                            roles=roles, rope=False, q_scale=HEAD_DIM ** -0.5 * math.log2(math.e),
                            has_aux=True)
    bias = _na_bias_tables(rpb, seq // GRID_W)
    o_c = _na_attention(proj, bias, q_col=q_col, k_col=k_col, v_col=v_col, g_col=g_col)
    conv = _ssd_conv(proj, conv_w, conv_b, col=xbc_col)
    y_f, y_b = _ssd_scan(conv, dt_raw, dt_bias, a_log)
    d_x = jnp.repeat(ssd_d.astype(F32), SSD_HEAD_DIM).reshape(1, SSD_WIDTH)
    o_d = _gated_norm(y_f, y_b, conv, proj, d_x, norm_w, z_col=z_col)
    return _out_proj(o_c, o_d, w_out.astype(BF16), x, gate)


def kernel(x, c, e_norm_g, e_ada_w, e_ada_b, e_w_in, e_q_norm, e_k_norm, s5_lam_re, s5_lam_im,
           s5_log_step, s5_b_re, s5_b_im, s5_c_re, s5_c_im, s5_d, s5_w_glu, s5_b_glu, e_w_out,
           o_norm_g, o_ada_w, o_ada_b, o_w_in, o_q_norm, o_k_norm, na_rpb, ssd_conv_w, ssd_conv_b,
           ssd_dt_bias, ssd_a_log, ssd_d, ssd_norm_w, o_w_out):
    assert x.shape[0] == 1
    h = x[0]
    h = _layer_attn_s5(h, c, e_norm_g[0], e_ada_w[0], e_ada_b[0], e_w_in[0], e_q_norm[0],
                       e_k_norm[0], s5_lam_re[0], s5_lam_im[0], s5_log_step[0], s5_b_re[0],
                       s5_b_im[0], s5_c_re[0], s5_c_im[0], s5_d[0], s5_w_glu[0], s5_b_glu[0],
                       e_w_out[0])
    h = _layer_na_ssd(h, c, o_norm_g[0], o_ada_w[0], o_ada_b[0], o_w_in[0], o_q_norm[0],
                      o_k_norm[0], na_rpb[0], ssd_conv_w[0], ssd_conv_b[0], ssd_dt_bias[0],
                      ssd_a_log[0], ssd_d[0], ssd_norm_w[0], o_w_out[0])
    return h[None]
```

```python
import functools
import math

import jax
import jax.numpy as jnp
import numpy as np
from jax import lax
from jax.experimental import pallas as pl
from jax.experimental.pallas import tpu as pltpu

F32 = jnp.float32
BF16 = jnp.bfloat16
HIGHEST = lax.Precision.HIGHEST

GRID_W = 64
HEAD_DIM = 128
EPS = 1e-6
NEG_INF = -1e30
ROPE_THETA = 10000.0

A_HEADS = 8
A_KV_HEADS = 2
A_GROUP = A_HEADS // A_KV_HEADS
S5_GROUP = 16
S5_GROUPS = 64
S5_STATE = 64
S5_CHUNK = 32
C_HEADS = 8
NA_ROWS = 8
NA_COLS = 16
NA_QROWS = 4
NA_KROWS = 12
SSD_HEADS = 16
SSD_HEAD_DIM = 64
SSD_GROUPS = 2
SSD_STATE = 128
SSD_CONV = 5
SSD_CHUNK = 128
SSD_WIDTH = SSD_HEADS * SSD_HEAD_DIM
SSD_BC = SSD_GROUPS * SSD_STATE

V7X_VMEM_BYTES = 64 * 1024 * 1024
MiB = 1024 * 1024


def _cparams(semantics, vmem_mib):
    assert vmem_mib * MiB < V7X_VMEM_BYTES
    return pltpu.CompilerParams(dimension_semantics=semantics, vmem_limit_bytes=vmem_mib * MiB)


def _silu(x):
    return x * (1.0 / (1.0 + jnp.exp(-x)))


def _sigmoid(x):
    return 1.0 / (1.0 + jnp.exp(-x))


def _ada_kernel(c_ref, w_ref, b_ref, o_ref):
    sc = _silu(c_ref[...])
    o_ref[...] = jnp.sum(w_ref[...] * sc, axis=0, keepdims=True) + b_ref[...]


def _ada_mod(c, w, b):
    d, n = w.shape
    tn = 512
    out = pl.pallas_call(
        _ada_kernel,
        grid=(n // tn,),
        in_specs=[pl.BlockSpec((d, 1), lambda j: (0, 0)),
                  pl.BlockSpec((d, tn), lambda j: (0, j)),
                  pl.BlockSpec((1, tn), lambda j: (0, j))],
        out_specs=pl.BlockSpec((1, tn), lambda j: (0, j)),
        out_shape=jax.ShapeDtypeStruct((1, n), F32),
        compiler_params=_cparams(("arbitrary",), 24),
        name="ada_mod",
    )(c.astype(F32).reshape(d, 1), w, b.reshape(1, n))
    shift, scale, gate = jnp.split(out, 3, axis=-1)
    return shift, 1.0 + scale, gate


IN_TN = 256
IN_TW = 2304
IN_ROW_CHUNK = 64


def _norm_mod_kernel(x_ref, g_ref, sc_ref, sh_ref, h_ref):
    tm = x_ref.shape[0]

    def body(r, carry):
        rows = pl.ds(pl.multiple_of(r * IN_ROW_CHUNK, IN_ROW_CHUNK), IN_ROW_CHUNK)
        xf = x_ref[rows, :]
        ms = jnp.mean(xf * xf, axis=-1, keepdims=True)
        y = xf * lax.rsqrt(ms + EPS) * g_ref[...]
        h_ref[rows, :] = (y * sc_ref[...] + sh_ref[...]).astype(h_ref.dtype)
        return carry

    lax.fori_loop(0, tm // IN_ROW_CHUNK, body, 0, unroll=2)


def _norm_mod(x, norm_g, scale1p, shift, *, tm=512):
    seq, d = x.shape
    const = lambda i: (0, 0)
    return pl.pallas_call(
        _norm_mod_kernel,
        grid=(seq // tm,),
        in_specs=[pl.BlockSpec((tm, d), lambda i: (i, 0)),
                  pl.BlockSpec((1, d), const), pl.BlockSpec((1, d), const),
                  pl.BlockSpec((1, d), const)],
        out_specs=pl.BlockSpec((tm, d), lambda i: (i, 0)),
        out_shape=jax.ShapeDtypeStruct((seq, d), BF16),
        compiler_params=_cparams(("arbitrary",), 32),
        name="norm_mod",
    )(x, norm_g.reshape(1, d), scale1p, shift)


def _in_proj_kernel(h_ref, w_ref, qn_ref, kn_ref, cos_ref, sin_ref, *outs, roles, rope, q_scale,
                    has_aux):
    o_ref = outs[0]
    aux_ref = outs[1] if has_aux else None
    c = pl.program_id(0)
    acc = jnp.dot(h_ref[...], w_ref[...], preferred_element_type=F32)
    per_tile = IN_TW // IN_TN

    def role_of(granule):
        return next(role for lo, hi, role in roles if lo <= granule < hi)

    def qk_epilogue(a2, gain_ref, scale):
        heads = []
        for h in range(IN_TN // HEAD_DIM):
            a = a2[:, h * HEAD_DIM:(h + 1) * HEAD_DIM]
            ms = jnp.mean(a * a, axis=-1, keepdims=True)
            a = a * lax.rsqrt(ms + EPS) * gain_ref[...]
            if rope:
                a = a * cos_ref[...] + pltpu.roll(a, HEAD_DIM // 2, axis=1) * sin_ref[...]
            if scale != 1.0:
                a = a * scale
            heads.append(a)
        return jnp.concatenate(heads, axis=-1)

    for ct in range(roles[-1][1] // per_tile):
        @pl.when(c == ct)
        def _(ct=ct):
            for s in range(per_tile):
                role = role_of(ct * per_tile + s)
                cols = slice(s * IN_TN, (s + 1) * IN_TN)
                a = acc[:, cols]
                if role == "q":
                    a = qk_epilogue(a, qn_ref, q_scale)
                elif role == "k":
                    a = qk_epilogue(a, kn_ref, 1.0)
                elif role == "silu":
                    a = _silu(a)
                elif role == "aux":
                    aux_ref[...] = a
                elif role != "plain":
                    raise ValueError(role)
                o_ref[:, cols] = a.astype(o_ref.dtype)


def _in_proj(x, norm_g, scale1p, shift, w_bf16, q_gain, k_gain, cos2, sin2, *, roles, rope,
             q_scale, has_aux, tm=512):
    seq, d = x.shape
    n = w_bf16.shape[1]
    assert seq % tm == 0 and n % IN_TW == 0 and IN_TW % IN_TN == 0
    assert roles[-1][1] == n // IN_TN
    ncol = n // IN_TW
    h = _norm_mod(x, norm_g, scale1p, shift)
    row = lambda c, i: (i, 0)
    const = lambda c, i: (0, 0)
    out_shape = [jax.ShapeDtypeStruct((seq, n), BF16)]
    out_specs = [pl.BlockSpec((tm, IN_TW), lambda c, i: (i, c))]
    if has_aux:
        assert roles[-1][2] == "aux" and roles[-1][1] - roles[-1][0] == 1
        out_shape.append(jax.ShapeDtypeStruct((seq, IN_TN), F32))
        out_specs.append(
            pl.BlockSpec((tm, IN_TN), lambda c, i: (jnp.where(c == ncol - 1, i, 0), 0)))
    kern = functools.partial(_in_proj_kernel, roles=roles, rope=rope, q_scale=q_scale,
                             has_aux=has_aux)
    return pl.pallas_call(
        kern,
        grid=(ncol, seq // tm),
        in_specs=[pl.BlockSpec((tm, d), row),
                  pl.BlockSpec((d, IN_TW), lambda c, i: (0, c)),
                  pl.BlockSpec((1, HEAD_DIM), const), pl.BlockSpec((1, HEAD_DIM), const),
                  pl.BlockSpec((tm, HEAD_DIM), row), pl.BlockSpec((tm, HEAD_DIM), row)],
        out_specs=out_specs,
        out_shape=out_shape,
        compiler_params=_cparams(("arbitrary", "arbitrary"), 56),
        name="in_proj_rope" if rope else "in_proj",
    )(h, w_bf16, q_gain.reshape(1, HEAD_DIM), k_gain.reshape(1, HEAD_DIM), cos2, sin2)


GQA_TK = 1024


def _gqa_kernel(q_ref, k_ref, v_ref, g_ref, o_ref, acc_scr, m_scr, s_scr):
    tq = q_ref.shape[0]
    nk = k_ref.shape[0] // GQA_TK
    q_all = jnp.concatenate(
        [q_ref[:, h * HEAD_DIM:(h + 1) * HEAD_DIM] for h in range(A_GROUP)], axis=0)
    acc_scr[...] = jnp.zeros(acc_scr.shape, F32)
    m_scr[...] = jnp.full(m_scr.shape, -jnp.inf, F32)
    ones = jnp.ones((GQA_TK, HEAD_DIM), BF16)

    def key_rows(kc):
        return pl.ds(pl.multiple_of(kc * GQA_TK, GQA_TK), GQA_TK)

    def scores(kc):
        return lax.dot_general(q_all, k_ref[key_rows(kc), :], (((1,), (1,)), ((), ())),
                               preferred_element_type=F32)

    def softmax_pv(slot, kc):
        s = s_scr[slot]
        v1 = jnp.concatenate([v_ref[key_rows(kc), :], ones], axis=1)
        m_prev = m_scr[...]
        m_new = jnp.maximum(m_prev, jnp.max(s, axis=-1, keepdims=True))
        alpha = jnp.exp2(m_prev - m_new)
        p = jnp.concatenate(
            [jnp.exp2(s[:, j * HEAD_DIM:(j + 1) * HEAD_DIM] - m_new).astype(BF16)
             for j in range(GQA_TK // HEAD_DIM)], axis=1)
        pv = jnp.dot(p, v1, preferred_element_type=F32)
        acc_scr[...] = jnp.concatenate([alpha, alpha], axis=1) * acc_scr[...] + pv
        m_scr[...] = m_new

    s_scr[0] = scores(0)

    def pair(i, carry):
        kc = 2 * i
        s_scr[1] = scores(kc + 1)
        softmax_pv(0, kc)
        s_scr[0] = scores(kc + 2)
        softmax_pv(1, kc + 1)
        return carry

    lax.fori_loop(0, nk // 2 - 1, pair, 0)
    s_scr[1] = scores(nk - 1)
    softmax_pv(0, nk - 2)
    softmax_pv(1, nk - 1)
    for h in range(A_GROUP):
        cols = slice(h * HEAD_DIM, (h + 1) * HEAD_DIM)
        a = acc_scr[h * tq:(h + 1) * tq, :]
        o = a[:, :HEAD_DIM] * (1.0 / a[:, HEAD_DIM:])
        o_ref[:, cols] = (o * g_ref[:, cols].astype(F32)).astype(o_ref.dtype)


def _gqa_attention(proj, *, q_col, k_col, v_col, g_col, tq=256):
    seq = proj.shape[0]
    gw = A_GROUP * HEAD_DIM
    assert q_col % gw == 0 and g_col % gw == 0 and k_col % HEAD_DIM == 0 and v_col % HEAD_DIM == 0
    assert seq % (2 * GQA_TK) == 0 and seq % tq == 0
    return pl.pallas_call(
        _gqa_kernel,
        grid=(A_KV_HEADS, seq // tq),
        in_specs=[pl.BlockSpec((tq, gw), lambda kh, qi: (qi, q_col // gw + kh)),
                  pl.BlockSpec((seq, HEAD_DIM), lambda kh, qi: (0, k_col // HEAD_DIM + kh)),
                  pl.BlockSpec((seq, HEAD_DIM), lambda kh, qi: (0, v_col // HEAD_DIM + kh)),
                  pl.BlockSpec((tq, gw), lambda kh, qi: (qi, g_col // gw + kh))],
        out_specs=pl.BlockSpec((tq, gw), lambda kh, qi: (qi, kh)),
        out_shape=jax.ShapeDtypeStruct((seq, A_HEADS * HEAD_DIM), BF16),
        scratch_shapes=[pltpu.VMEM((A_GROUP * tq, 2 * HEAD_DIM), F32),
                        pltpu.VMEM((A_GROUP * tq, HEAD_DIM), F32),
                        pltpu.VMEM((2, A_GROUP * tq, GQA_TK), F32)],
        compiler_params=_cparams(("arbitrary", "arbitrary"), 56),
        name="gqa_attention",
    )(proj, proj, proj, proj)


def _s5_tables(lam_re, lam_im, log_step, b_re, b_im, c_re, c_im, s5_d):
    t = S5_CHUNK
    g, p, hh = S5_GROUPS, S5_STATE, S5_GROUP
    lr = lam_re.astype(F32)
    li = lam_im.astype(F32)
    dt = jnp.exp(log_step.astype(F32))[..., None]
    mag = jnp.exp(lr * dt)
    ab_re = mag * jnp.cos(li * dt)
    ab_im = mag * jnp.sin(li * dt)
    den = lr * lr + li * li
    num_re = ab_re - 1.0
    f_re = (num_re * lr + ab_im * li) / den
    f_im = (ab_im * lr - num_re * li) / den
    br = b_re.astype(F32)
    bi = b_im.astype(F32)
    bb_re = f_re[..., None] * br - f_im[..., None] * bi
    bb_im = f_re[..., None] * bi + f_im[..., None] * br
    kk = jnp.arange(t + 1, dtype=F32)[None, None, :, None]
    pmag = jnp.exp(kk * (lr * dt)[:, :, None, :])
    ang = kk * (li * dt)[:, :, None, :]
    pw_re = pmag * jnp.cos(ang)
    pw_im = pmag * jnp.sin(ang)
    bt_re = jnp.swapaxes(bb_re, -1, -2)[:, :, None]
    bt_im = jnp.swapaxes(bb_im, -1, -2)[:, :, None]
    pr = pw_re[:, :, :, None, :]
    pi = pw_im[:, :, :, None, :]
    wt_re = pr * bt_re - pi * bt_im
    wt_im = pr * bt_im + pi * bt_re
    cr = c_re.astype(F32)
    ci = c_im.astype(F32)
    cc_re = cr[:, :, None]
    cc_im = ci[:, :, None]
    cl_re = cc_re * pr - cc_im * pi
    cl_im = cc_re * pi + cc_im * pr
    bt_cat = jnp.concatenate([bt_re[:, :, 0], -bt_im[:, :, 0]], axis=-1)
    cl_cat = jnp.concatenate([cl_re, cl_im], axis=-1).reshape(2, g, (t + 1) * hh, 2 * p)
    kj = jnp.einsum('dgjq,dgnq->dgjn', bt_cat, cl_cat, precision=HIGHEST)
    kj = kj.reshape(2, g, hh, t + 1, hh)
    dmat = s5_d.astype(F32).reshape(g, hh)[:, :, None] * jnp.eye(hh, dtype=F32)[None]
    k0 = kj[0, :, :, 0] + kj[1, :, :, 0] + dmat
    kb = kj[1, :, :, 1:t][:, :, ::-1]
    kf = kj[0, :, :, 1:t]
    kflat = jnp.concatenate([kb, k0[:, :, None], kf], axis=2).reshape(g, hh, (2 * t - 1) * hh)
    m = jnp.stack([kflat[:, :, (t - 1 - tp) * hh:(2 * t - 1 - tp) * hh] for tp in range(t)],
                  axis=1).reshape(g, t * hh, t * hh)
    flat = lambda a: a.reshape(g, t * hh, p)
    fmat = jnp.concatenate([flat(wt_re[0, :, :t][:, ::-1]), flat(wt_re[1, :, :t]),
                            flat(wt_im[0, :, :t][:, ::-1]), flat(wt_im[1, :, :t])], axis=-1)
    e_t = lambda a: jnp.swapaxes(flat(a), 1, 2)
    emat = jnp.concatenate([e_t(cl_re[0, :, 1:t + 1]), e_t(cl_re[1, :, 1:t + 1][:, ::-1]),
                            -e_t(cl_im[0, :, 1:t + 1]), -e_t(cl_im[1, :, 1:t + 1][:, ::-1])],
                           axis=1)
    a_re = jnp.concatenate([pw_re[0, :, t], pw_re[1, :, t]], axis=-1)
    a_im = jnp.concatenate([pw_im[0, :, t], pw_im[1, :, t]], axis=-1)
    return m.astype(BF16), fmat.astype(BF16), emat.astype(BF16), a_re, a_im


S5_GPS = 4


def _s5_state_kernel(u_ref, f_ref, o_ref):
    sw = f_ref.shape[2]
    for a in range(S5_GPS):
        o_ref[:, a * sw:(a + 1) * sw] = jnp.dot(u_ref[a], f_ref[a], preferred_element_type=F32)


def _s5_fwd_lanes(shape):
    lane = lax.broadcasted_iota(jnp.int32, shape, len(shape) - 1)
    return (lane % (2 * S5_STATE)) < S5_STATE


def _s5_scan_kernel(xf_ref, xb_ref, ar_ref, ai_ref, hf_ref, hb_ref, re_scr, im_scr):
    cb = xf_ref.shape[0]
    w = 2 * S5_STATE

    @pl.when(pl.program_id(0) == 0)
    def _():
        re_scr[...] = jnp.zeros(re_scr.shape, F32)
        im_scr[...] = jnp.zeros(im_scr.shape, F32)

    ar = ar_ref[...]
    ai = ai_ref[...]
    fwd = _s5_fwd_lanes(ar.shape)

    def body(i, carry):
        re, im = carry
        j = cb - 1 - i
        st = jnp.concatenate([re, im], axis=-1).astype(hf_ref.dtype)
        hf_ref[i] = st
        hb_ref[j] = st
        xf = xf_ref[i]
        xb = xb_ref[j]
        x_re = jnp.where(fwd, xf[:, :w], xb[:, :w])
        x_im = jnp.where(fwd, xf[:, w:], xb[:, w:])
        return ar * re - ai * im + x_re, ar * im + ai * re + x_im

    re, im = lax.fori_loop(0, cb, body, (re_scr[...], im_scr[...]), unroll=2)
    re_scr[...] = re
    im_scr[...] = im


def _s5_out_kernel(u_ref, m_ref, hf_ref, hb_ref, e_ref, o_ref):
    sw = e_ref.shape[1]
    c0 = math.sqrt(2.0 / math.pi)
    for a in range(S5_GPS):
        lanes = slice(a * sw, (a + 1) * sw)
        hf = hf_ref[:, lanes]
        y = jnp.dot(u_ref[a], m_ref[a], preferred_element_type=F32)
        h = jnp.where(_s5_fwd_lanes(hf.shape), hf, hb_ref[:, lanes])
        y = y + jnp.dot(h, e_ref[a], preferred_element_type=F32)
        y = 0.5 * y * (1.0 + jnp.tanh(c0 * (y + 0.044715 * (y * y * y))))
        o_ref[a] = y.astype(o_ref.dtype)


S5_SUPER = 128 // S5_GROUP


def _block_swap_matrix():
    n = S5_SUPER
    idx = np.arange(n * n * S5_GROUP)
    b, a, j = idx // (n * S5_GROUP), (idx // S5_GROUP) % n, idx % S5_GROUP
    sel = np.zeros((idx.size, idx.size), np.float32)
    sel[idx, a * n * S5_GROUP + b * S5_GROUP + j] = 1.0
    return jnp.asarray(sel, BF16)


S5_TBLK = 16


def _s5_gather_kernel(x_ref, sel_ref, o_ref, rows_scr):
    nc = x_ref.shape[0]
    rows_scr[...] = x_ref[...].astype(F32).reshape(nc * S5_TBLK, 128)
    for q in range(S5_TBLK // S5_SUPER):
        lhs = jnp.concatenate(
            [rows_scr[pl.ds(S5_SUPER * q + b, nc, stride=S5_TBLK), :].astype(BF16)
             for b in range(S5_SUPER)], axis=1)
        out = jnp.dot(lhs, sel_ref[...], preferred_element_type=F32).astype(o_ref.dtype)
        for a in range(S5_SUPER):
            o_ref[a, :, q * 128:(q + 1) * 128] = out[:, a * 128:(a + 1) * 128]


def _s5_gather(proj, col):
    seq, n = proj.shape
    t, g = S5_CHUNK, S5_GROUPS
    nc = seq // t
    assert n % 128 == 0 and col % 128 == 0 and t % S5_TBLK == 0
    lanes = S5_TBLK * S5_GROUP
    return pl.pallas_call(
        _s5_gather_kernel,
        grid=(g // S5_SUPER, t // S5_TBLK),
        in_specs=[pl.BlockSpec((nc, S5_TBLK, 128), lambda sg, h: (0, h, col // 128 + sg)),
                  pl.BlockSpec((1024, 1024), lambda sg, h: (0, 0))],
        out_specs=pl.BlockSpec((S5_SUPER, nc, lanes), lambda sg, h: (sg, 0, h)),
        out_shape=jax.ShapeDtypeStruct((g, nc, t * S5_GROUP), BF16),
        scratch_shapes=[pltpu.VMEM((nc * S5_TBLK, 128), F32)],
        compiler_params=_cparams(("arbitrary", "arbitrary"), 40),
        name="s5_gather",
    )(proj.reshape(nc, t, n), _block_swap_matrix())


def _s5_scatter_kernel(y_ref, sel_ref, o_ref, rows_scr):
    nc = y_ref.shape[1]
    for q in range(S5_TBLK // S5_SUPER):
        lhs = jnp.concatenate([y_ref[a, :, q * 128:(q + 1) * 128] for a in range(S5_SUPER)],
                              axis=1)
        out = jnp.dot(lhs, sel_ref[...], preferred_element_type=F32)
        for b in range(S5_SUPER):
            rows_scr[pl.ds(S5_SUPER * q + b, nc, stride=S5_TBLK), :] = out[:, b * 128:(b + 1) * 128]
    o_ref[...] = rows_scr[...].reshape(nc, S5_TBLK, 128).astype(o_ref.dtype)


def _s5_scatter(yg):
    g, nc, th = yg.shape
    t = S5_CHUNK
    width = g * S5_GROUP
    lanes = S5_TBLK * S5_GROUP
    y3 = pl.pallas_call(
        _s5_scatter_kernel,
        grid=(g // S5_SUPER, t // S5_TBLK),
        in_specs=[pl.BlockSpec((S5_SUPER, nc, lanes), lambda sg, h: (sg, 0, h)),
                  pl.BlockSpec((1024, 1024), lambda sg, h: (0, 0))],
        out_specs=pl.BlockSpec((nc, S5_TBLK, 128), lambda sg, h: (0, h, sg)),
        out_shape=jax.ShapeDtypeStruct((nc, t, width), BF16),
        scratch_shapes=[pltpu.VMEM((nc * S5_TBLK, 128), F32)],
        compiler_params=_cparams(("arbitrary", "arbitrary"), 40),
        name="s5_scatter",
    )(yg, _block_swap_matrix())
    return y3.reshape(nc * t, width)


def _s5_mixer(proj, u_col, tables):
    m, fmat, emat, a_re, a_im = tables
    seq = proj.shape[0]
    t, g, hh, p = S5_CHUNK, S5_GROUPS, S5_GROUP, S5_STATE
    nc = seq // t
    th = t * hh
    sw = 4 * p
    ug = _s5_gather(proj, u_col)
    hend = pl.pallas_call(
        _s5_state_kernel,
        grid=(g // S5_GPS,),
        in_specs=[pl.BlockSpec((S5_GPS, nc, th), lambda i: (i, 0, 0)),
                  pl.BlockSpec((S5_GPS, th, sw), lambda i: (i, 0, 0))],
        out_specs=pl.BlockSpec((nc, S5_GPS * sw), lambda i: (0, i)),
        out_shape=jax.ShapeDtypeStruct((nc, g * sw), F32),
        compiler_params=_cparams(("arbitrary",), 32),
        name="s5_chunk_state",
    )(ug, fmat)
    hend3 = hend.reshape(nc, g, sw)
    cb = min(nc, 64)
    nb = nc // cb
    fwd_blk = lambda c: (c, 0, 0)
    bwd_blk = lambda c: (nb - 1 - c, 0, 0)
    hf, hb = pl.pallas_call(
        _s5_scan_kernel,
        grid=(nb,),
        in_specs=[pl.BlockSpec((cb, g, sw), fwd_blk), pl.BlockSpec((cb, g, sw), bwd_blk),
                  pl.BlockSpec((g, 2 * p), lambda c: (0, 0)),
                  pl.BlockSpec((g, 2 * p), lambda c: (0, 0))],
        out_specs=[pl.BlockSpec((cb, g, sw), fwd_blk), pl.BlockSpec((cb, g, sw), bwd_blk)],
        out_shape=[jax.ShapeDtypeStruct((nc, g, sw), BF16)] * 2,
        scratch_shapes=[pltpu.VMEM((g, 2 * p), F32), pltpu.VMEM((g, 2 * p), F32)],
        compiler_params=_cparams(("arbitrary",), 48),
        name="s5_chunk_scan",
    )(hend3, hend3, a_re, a_im)
    yg = pl.pallas_call(
        _s5_out_kernel,
        grid=(g // S5_GPS,),
        in_specs=[pl.BlockSpec((S5_GPS, nc, th), lambda i: (i, 0, 0)),
                  pl.BlockSpec((S5_GPS, th, th), lambda i: (i, 0, 0)),
                  pl.BlockSpec((nc, S5_GPS * sw), lambda i: (0, i)),
                  pl.BlockSpec((nc, S5_GPS * sw), lambda i: (0, i)),
                  pl.BlockSpec((S5_GPS, sw, th), lambda i: (i, 0, 0))],
        out_specs=pl.BlockSpec((S5_GPS, nc, th), lambda i: (i, 0, 0)),
        out_shape=jax.ShapeDtypeStruct((g, nc, th), BF16),
        compiler_params=_cparams(("arbitrary",), 32),
        name="s5_output",
    )(ug, m, hf.reshape(nc, g * sw), hb.reshape(nc, g * sw), emat)
    return _s5_scatter(yg)


def _glu_kernel(y_ref, wv_ref, wg_ref, bv_ref, bg_ref, s_ref, o_ref):
    y = y_ref[...]
    val = jnp.dot(y, wv_ref[...], preferred_element_type=F32) + bv_ref[...]
    gt = jnp.dot(y, wg_ref[...], preferred_element_type=F32) + bg_ref[...]
    o_ref[...] = (val * _sigmoid(gt) * _silu(s_ref[...].astype(F32))).astype(o_ref.dtype)


def _glu(y, w_bf16, b, proj, *, s_col, tm=1024, tn=512):
    seq, kdim = y.shape
    width = w_bf16.shape[1] // 2
    assert s_col % tn == 0
    nj = width // tn
    b2 = b.reshape(1, 2 * width).astype(F32)
    return pl.pallas_call(
        _glu_kernel,
        grid=(seq // tm, nj),
        in_specs=[pl.BlockSpec((tm, kdim), lambda i, j: (i, 0)),
                  pl.BlockSpec((kdim, tn), lambda i, j: (0, j)),
                  pl.BlockSpec((kdim, tn), lambda i, j: (0, nj + j)),
                  pl.BlockSpec((1, tn), lambda i, j: (0, j)),
                  pl.BlockSpec((1, tn), lambda i, j: (0, nj + j)),
                  pl.BlockSpec((tm, tn), lambda i, j: (i, s_col // tn + j))],
        out_specs=pl.BlockSpec((tm, tn), lambda i, j: (i, j)),
        out_shape=jax.ShapeDtypeStruct((seq, width), BF16),
        compiler_params=_cparams(("arbitrary", "arbitrary"), 40),
        name="s5_glu",
    )(y, w_bf16, w_bf16, b2, b2, proj)


def _out_proj_kernel(a_ref, b_ref, wa_ref, wb_ref, x_ref, gate_ref, o_ref):
    acc = jnp.dot(a_ref[...], wa_ref[...], preferred_element_type=F32)
    acc = acc + jnp.dot(b_ref[...], wb_ref[...], preferred_element_type=F32)
    o_ref[...] = x_ref[...] + gate_ref[...] * acc


def _out_proj(oa, ob, w_bf16, x, gate, *, tm=512, tn=2048):
    seq, half = oa.shape
    d = w_bf16.shape[1]
    tn = min(tn, d)
    return pl.pallas_call(
        _out_proj_kernel,
        grid=(seq // tm, d // tn),
        in_specs=[pl.BlockSpec((tm, half), lambda i, j: (i, 0)),
                  pl.BlockSpec((tm, half), lambda i, j: (i, 0)),
                  pl.BlockSpec((half, tn), lambda i, j: (0, j)),
                  pl.BlockSpec((half, tn), lambda i, j: (1, j)),
                  pl.BlockSpec((tm, tn), lambda i, j: (i, j)),
                  pl.BlockSpec((1, tn), lambda i, j: (0, j))],
        out_specs=pl.BlockSpec((tm, tn), lambda i, j: (i, j)),
        out_shape=jax.ShapeDtypeStruct((seq, d), F32),
        compiler_params=_cparams(("arbitrary", "arbitrary"), 48),
        name="out_proj",
    )(oa, ob, w_bf16, w_bf16, x, gate)


def _na_bias_tables(rpb, rows):
    w = GRID_W
    nrb = rows // NA_QROWS
    assert rows >= NA_KROWS + NA_QROWS
    col = np.arange(w)
    col_start = np.clip(col - NA_COLS // 2, 0, w - NA_COLS)
    col_ok = (col[None, :] >= col_start[:, None]) & (col[None, :] < col_start[:, None] + NA_COLS)
    dc = np.clip(col[None, :] - col[:, None], -(NA_COLS - 1), NA_COLS - 1) + NA_COLS - 1
    onehot = (dc[None] == np.arange(2 * NA_COLS - 1)[:, None, None]).astype(np.float32)
    tt = jnp.einsum('hrd,dqk->hrqk', rpb.astype(F32) * math.log2(math.e), jnp.asarray(onehot),
                    precision=HIGHEST)
    tt = jnp.where(jnp.asarray(col_ok)[None, None], tt, NEG_INF)
    n_dr = 2 * NA_ROWS - 1
    tt = jnp.concatenate([tt, jnp.full((rpb.shape[0], 1, w, w), NEG_INF, F32)], axis=1)
    sel = np.zeros((3, NA_QROWS, NA_KROWS, n_dr + 1), np.float32)
    for ti, rb in enumerate((0, 1, nrb - 1)):
        ks = min(max(rb * NA_QROWS - NA_ROWS // 2, 0), rows - NA_KROWS)
        for rl in range(NA_QROWS):
            r = rb * NA_QROWS + rl
            rs = min(max(r - NA_ROWS // 2, 0), rows - NA_ROWS)
            for kl in range(NA_KROWS):
                kr = ks + kl
                sel[ti, rl, kl, kr - r + NA_ROWS - 1 if rs <= kr < rs + NA_ROWS else n_dr] = 1.0
    tab = jnp.einsum('trkd,hdqc->thrqkc', jnp.asarray(sel), tt, precision=HIGHEST)
    return tab.reshape(3, rpb.shape[0], NA_QROWS * w, NA_KROWS * w)


def _na_kernel(q_ref, k0_ref, k1_ref, k2_ref, v0_ref, v1_ref, v2_ref, b_ref, g_ref, o_ref):
    ones = jnp.ones((k0_ref.shape[0] * 3, HEAD_DIM), BF16)
    outs = []
    for h in range(C_HEADS):
        cols = slice(h * HEAD_DIM, (h + 1) * HEAD_DIM)
        k = jnp.concatenate([k0_ref[:, cols], k1_ref[:, cols], k2_ref[:, cols]], axis=0)
        v = jnp.concatenate([v0_ref[:, cols], v1_ref[:, cols], v2_ref[:, cols]], axis=0)
        s = lax.dot_general(q_ref[:, cols], k, (((1,), (1,)), ((), ())),
                            preferred_element_type=F32) + b_ref[0, h]
        m = jnp.max(s, axis=-1, keepdims=True)
        p = jnp.exp2(s - m).astype(BF16)
        pv = jnp.dot(p, jnp.concatenate([v, ones], axis=1), preferred_element_type=F32)
        o = pv[:, :HEAD_DIM] * (1.0 / pv[:, HEAD_DIM:])
        outs.append((o * _silu(g_ref[:, cols].astype(F32))).astype(o_ref.dtype))
    o_ref[...] = jnp.concatenate(outs, axis=1)


def _na_attention(proj, bias, *, q_col, k_col, v_col, g_col):
    seq = proj.shape[0]
    tq = NA_QROWS * GRID_W
    nrb = seq // tq
    nkb = NA_KROWS // NA_QROWS
    cw = C_HEADS * HEAD_DIM
    assert q_col % cw == 0 and k_col % cw == 0 and v_col % cw == 0 and g_col % cw == 0

    def kv_spec(col, off):
        return pl.BlockSpec((tq, cw), lambda rb: (jnp.clip(rb - 1, 0, nrb - nkb) + off, col // cw))

    btype = lambda rb: (jnp.where(rb == 0, 0, jnp.where(rb == nrb - 1, 2, 1)), 0, 0, 0)
    return pl.pallas_call(
        _na_kernel,
        grid=(nrb,),
        in_specs=[pl.BlockSpec((tq, cw), lambda rb: (rb, q_col // cw)),
                  kv_spec(k_col, 0), kv_spec(k_col, 1), kv_spec(k_col, 2),
                  kv_spec(v_col, 0), kv_spec(v_col, 1), kv_spec(v_col, 2),
                  pl.BlockSpec((1, C_HEADS, tq, nkb * tq), btype),
                  pl.BlockSpec((tq, cw), lambda rb: (rb, g_col // cw))],
        out_specs=pl.BlockSpec((tq, cw), lambda rb: (rb, 0)),
        out_shape=jax.ShapeDtypeStruct((seq, cw), BF16),
        compiler_params=_cparams(("arbitrary",), 48),
        name="na_attention",
    )(proj, proj, proj, proj, proj, proj, proj, bias, proj)


CONV_HALO = 8


def _conv_kernel(prev_ref, cur_ref, next_ref, w_ref, b_ref, o_ref):
    i = pl.program_id(0)
    tm = cur_ref.shape[0]
    prev = jnp.where(i == 0, 0.0, prev_ref[...].astype(F32))
    nxt = jnp.where(i == pl.num_programs(0) - 1, 0.0, next_ref[...].astype(F32))
    ext = jnp.concatenate([prev, cur_ref[...].astype(F32), nxt], axis=0)
    acc = jnp.zeros(cur_ref.shape, F32) + b_ref[...]
    for kk in range(SSD_CONV):
        start = CONV_HALO - SSD_CONV // 2 + kk
        acc = acc + ext[start:start + tm, :] * w_ref[kk:kk + 1, :]
    o_ref[...] = _silu(acc).astype(o_ref.dtype)


def _ssd_conv(proj, conv_w, conv_b, *, col, tm=512, tc=512):
    seq = proj.shape[0]
    ch = conv_w.shape[1]
    assert col % tc == 0 and ch % tc == 0
    nh = tm // CONV_HALO
    nblk = seq // CONV_HALO
    cb = col // tc
    return pl.pallas_call(
        _conv_kernel,
        grid=(seq // tm, ch // tc),
        in_specs=[pl.BlockSpec((CONV_HALO, tc), lambda i, j: (jnp.maximum(i * nh - 1, 0), cb + j)),
                  pl.BlockSpec((tm, tc), lambda i, j: (i, cb + j)),
                  pl.BlockSpec((CONV_HALO, tc),
                               lambda i, j: (jnp.minimum((i + 1) * nh, nblk - 1), cb + j)),
                  pl.BlockSpec((SSD_CONV, tc), lambda i, j: (0, j)),
                  pl.BlockSpec((1, tc), lambda i, j: (0, j))],
        out_specs=pl.BlockSpec((tm, tc), lambda i, j: (i, j)),
        out_shape=jax.ShapeDtypeStruct((seq, ch), BF16),
        compiler_params=_cparams(("arbitrary", "arbitrary"), 32),
        name="ssd_conv",
    )(proj, proj, proj, conv_w.astype(F32), conv_b.reshape(1, ch).astype(F32))


def _split_dot(a, b_bf16):
    hi = a.astype(BF16)
    lo = (a - hi.astype(F32)).astype(BF16)
    return (jnp.dot(hi, b_bf16, preferred_element_type=F32)
            + jnp.dot(lo, b_bf16, preferred_element_type=F32))


def _expand_heads(a, ex_bf16):
    return jnp.dot(a.astype(BF16), ex_bf16, preferred_element_type=F32)


def _ssd_direction(d, xs_ref, b_ref, c_ref, dt_ref, bias_ref, a_ref, ex, o_ref, st_scr):
    t = xs_ref.shape[0]
    gw = SSD_WIDTH // SSD_GROUPS
    hpg = SSD_HEADS // SSD_GROUPS
    row = lax.broadcasted_iota(jnp.int32, (t, t), 0)
    colm = lax.broadcasted_iota(jnp.int32, (t, t), 1)
    tri = (row >= colm) if d == 0 else (row <= colm)
    tri_b = tri.astype(BF16)

    z = dt_ref[...] + bias_ref[d]
    dt = jnp.maximum(z, 0.0) + jnp.log(1.0 + jnp.exp(-jnp.abs(z)))
    adt = dt * a_ref[d]
    a1 = adt.astype(BF16)
    r1 = adt - a1.astype(F32)
    a2 = r1.astype(BF16)
    a3 = (r1 - a2.astype(F32)).astype(BF16)
    r = (jnp.dot(tri_b, a1, preferred_element_type=F32)
         + jnp.dot(tri_b, a2, preferred_element_type=F32)
         + jnp.dot(tri_b, a3, preferred_element_type=F32))
    tot = r[t - 1:t, :] if d == 0 else r[0:1, :]
    dt_x = _expand_heads(dt, ex)
    er_x = _expand_heads(jnp.exp(r), ex)
    sd_x = _expand_heads(jnp.exp(tot - r), ex)
    et_x = _split_dot(jnp.exp(tot), ex)
    r_t = r.T

    xs = xs_ref[...].astype(F32)
    xd = xs * dt_x
    xd_b = xd.astype(BF16)
    xdd_b = (xd * sd_x).astype(BF16)

    y_parts = []
    for g in range(SSD_GROUPS):
        bg = b_ref[:, g * SSD_STATE:(g + 1) * SSD_STATE]
        cg = c_ref[:, g * SSD_STATE:(g + 1) * SSD_STATE]
        cb = lax.dot_general(cg, bg, (((1,), (1,)), ((), ())), preferred_element_type=F32)
        lanes = slice(g * gw, (g + 1) * gw)
        s_prev = st_scr[d, g]
        y_off = jnp.dot(cg, s_prev.astype(BF16), preferred_element_type=F32) * er_x[:, lanes]
        s_loc = lax.dot_general(bg, xdd_b[:, lanes], (((0,), (0,)), ((), ())),
                                preferred_element_type=F32)
        st_scr[d, g] = s_prev * et_x[:, lanes] + s_loc
        for hh in range(hpg):
            h = g * hpg + hh
            decay = jnp.exp(jnp.where(tri, r[:, h:h + 1] - r_t[h:h + 1, :], NEG_INF))
            sc = (cb * decay).astype(BF16)
            hl = slice(h * SSD_HEAD_DIM, (h + 1) * SSD_HEAD_DIM)
            y_parts.append(jnp.dot(sc, xd_b[:, hl], preferred_element_type=F32)
                           + y_off[:, hh * SSD_HEAD_DIM:(hh + 1) * SSD_HEAD_DIM])
    o_ref[...] = jnp.concatenate(y_parts, axis=-1).astype(o_ref.dtype)


def _ssd_scan_kernel(xf_ref, bf_ref, cf_ref, dtf_ref, xb_ref, bb_ref, cb_ref, dtb_ref,
                     bias_ref, a_ref, ex_ref, of_ref, ob_ref, st_scr):
    @pl.when(pl.program_id(0) == 0)
    def _():
        st_scr[...] = jnp.zeros(st_scr.shape, F32)

    ex = ex_ref[...]
    _ssd_direction(0, xf_ref, bf_ref, cf_ref, dtf_ref, bias_ref, a_ref, ex, of_ref, st_scr)
    _ssd_direction(1, xb_ref, bb_ref, cb_ref, dtb_ref, bias_ref, a_ref, ex, ob_ref, st_scr)


def _ssd_scan(conv, dt_raw, dt_bias, a_log):
    seq = conv.shape[0]
    t = SSD_CHUNK
    nc = seq // t
    a = -jnp.exp(a_log.astype(F32)).reshape(2, 1, SSD_HEADS)
    bias = dt_bias.astype(F32).reshape(2, 1, SSD_HEADS)
    ex = jnp.repeat(jnp.eye(SSD_HEADS, dtype=BF16), SSD_HEAD_DIM, axis=1)
    dt_f = dt_raw[:, :SSD_HEADS]
    dt_b = dt_raw[:, SSD_HEADS:2 * SSD_HEADS]
    nxb = SSD_WIDTH // SSD_BC
    fwd = lambda c: c
    bwd = lambda c: nc - 1 - c

    def chunk_specs(pos):
        return [pl.BlockSpec((t, SSD_WIDTH), lambda c: (pos(c), 0)),
                pl.BlockSpec((t, SSD_BC), lambda c: (pos(c), nxb)),
                pl.BlockSpec((t, SSD_BC), lambda c: (pos(c), nxb + 1)),
                pl.BlockSpec((t, SSD_HEADS), lambda c: (pos(c), 0))]

    const3 = pl.BlockSpec((2, 1, SSD_HEADS), lambda c: (0, 0, 0))
    return pl.pallas_call(
        _ssd_scan_kernel,
        grid=(nc,),
        in_specs=chunk_specs(fwd) + chunk_specs(bwd) + [
            const3, const3, pl.BlockSpec((SSD_HEADS, SSD_WIDTH), lambda c: (0, 0))],
        out_specs=[pl.BlockSpec((t, SSD_WIDTH), lambda c: (fwd(c), 0)),
                   pl.BlockSpec((t, SSD_WIDTH), lambda c: (bwd(c), 0))],
        out_shape=[jax.ShapeDtypeStruct((seq, SSD_WIDTH), BF16)] * 2,
        scratch_shapes=[pltpu.VMEM((2, SSD_GROUPS, SSD_STATE, SSD_WIDTH // SSD_GROUPS), F32)],
        compiler_params=_cparams(("arbitrary",), 40),
        name="ssd_scan",
    )(conv, conv, conv, dt_f, conv, conv, conv, dt_b, bias, a, ex)


def _gated_norm_kernel(yf_ref, yb_ref, xs_ref, z_ref, d_ref, w_ref, o_ref):
    y = (yf_ref[...].astype(F32) + yb_ref[...].astype(F32)
         + d_ref[...] * xs_ref[...].astype(F32))
    y = y * _silu(z_ref[...].astype(F32))
    ms = jnp.mean(y * y, axis=-1, keepdims=True)
    o_ref[...] = (y * lax.rsqrt(ms + EPS) * w_ref[...]).astype(o_ref.dtype)


def _gated_norm(y_f, y_b, conv, proj, d_x, norm_w, *, z_col, tm=512):
    seq = conv.shape[0]
    w = SSD_WIDTH
    assert z_col % w == 0
    return pl.pallas_call(
        _gated_norm_kernel,
        grid=(seq // tm,),
        in_specs=[pl.BlockSpec((tm, w), lambda i: (i, 0)),
                  pl.BlockSpec((tm, w), lambda i: (i, 0)),
                  pl.BlockSpec((tm, w), lambda i: (i, 0)),
                  pl.BlockSpec((tm, w), lambda i: (i, z_col // w)),
                  pl.BlockSpec((1, w), lambda i: (0, 0)),
                  pl.BlockSpec((1, w), lambda i: (0, 0))],
        out_specs=pl.BlockSpec((tm, w), lambda i: (i, 0)),
        out_shape=jax.ShapeDtypeStruct((seq, w), BF16),
        compiler_params=_cparams(("arbitrary",), 32),
        name="ssd_gated_norm",
    )(y_f, y_b, conv, proj, d_x, norm_w.reshape(1, w).astype(F32))


def _rope_tables(seq):
    rows = seq // GRID_W
    n_axis = HEAD_DIM // 4
    inv = ROPE_THETA ** (-np.arange(n_axis, dtype=np.float64) / n_axis)
    ang_r = np.arange(rows, dtype=np.float64)[:, None] * inv
    ang_c = np.arange(GRID_W, dtype=np.float64)[:, None] * inv

    def expand(fr, fc):
        fr = jnp.asarray(fr.astype(np.float32))
        fc = jnp.asarray(fc.astype(np.float32))
        tab = jnp.concatenate(
            [jnp.broadcast_to(fr[:, None, :], (rows, GRID_W, n_axis)),
             jnp.broadcast_to(fc[None, :, :], (rows, GRID_W, n_axis))], axis=-1)
        return tab.reshape(seq, 2 * n_axis)

    cos = expand(np.cos(ang_r), np.cos(ang_c))
    sin = expand(np.sin(ang_r), np.sin(ang_c))
    return jnp.concatenate([cos, cos], axis=-1), jnp.concatenate([-sin, sin], axis=-1)


def _deinterleave_perm():
    return np.concatenate([np.arange(0, HEAD_DIM, 2), np.arange(1, HEAD_DIM, 2)])


def _layer_attn_s5(x, c, norm_g, ada_w, ada_b, w_in, q_norm, k_norm, lam_re, lam_im, log_step,
                   b_re, b_im, c_re, c_im, s5_d, w_glu, b_glu, w_out):
    seq, d = x.shape
    shift, scale1p, gate = _ada_mod(c, ada_w, ada_b)
    aw = A_HEADS * HEAD_DIM
    akw = A_KV_HEADS * HEAD_DIM
    perm = _deinterleave_perm()
    nqk = (aw + akw) // HEAD_DIM
    colperm = (np.arange(nqk)[:, None] * HEAD_DIM + perm[None, :]).reshape(-1)
    w = jnp.concatenate([w_in[:, colperm], w_in[:, aw + akw:]], axis=1).astype(BF16)
    cos2, sin2 = _rope_tables(seq)
    t = IN_TN
    q_col, k_col, v_col = 0, aw, aw + akw
    g_col = aw + 2 * akw
    u_col = g_col + aw
    gb_col = u_col + d // 2
    roles = ((q_col // t, k_col // t, "q"), (k_col // t, v_col // t, "k"),
             (v_col // t, g_col // t, "plain"), (g_col // t, u_col // t, "silu"),
             (u_col // t, (gb_col + d // 2) // t, "plain"))
    (proj,) = _in_proj(x, norm_g, scale1p, shift, w, q_norm[perm], k_norm[perm], cos2, sin2,
                       roles=roles, rope=True, q_scale=HEAD_DIM ** -0.5 * math.log2(math.e),
                       has_aux=False)
    o_a = _gqa_attention(proj, q_col=q_col, k_col=k_col, v_col=v_col, g_col=g_col)
    tables = _s5_tables(lam_re, lam_im, log_step, b_re, b_im, c_re, c_im, s5_d)
    y = _s5_mixer(proj, u_col, tables)
    o_b = _glu(y, w_glu.astype(BF16), b_glu, proj, s_col=gb_col)
    return _out_proj(o_a, o_b, w_out.astype(BF16), x, gate)


def _layer_na_ssd(x, c, norm_g, ada_w, ada_b, w_in, q_norm, k_norm, rpb, conv_w, conv_b,
                  dt_bias, a_log, ssd_d, norm_w, w_out):
    seq, d = x.shape
    shift, scale1p, gate = _ada_mod(c, ada_w, ada_b)
    cw = C_HEADS * HEAD_DIM
    n_in = w_in.shape[1]
    t = IN_TN
    n_pad = -(-n_in // t) * t
    w = jnp.pad(w_in, ((0, 0), (0, n_pad - n_in))).astype(BF16)
    q_col, k_col, v_col, g_col, z_col = 0, cw, 2 * cw, 3 * cw, 4 * cw
    xbc_col = z_col + SSD_WIDTH
    dt_col = xbc_col + SSD_WIDTH + 2 * SSD_BC
    roles = ((q_col // t, k_col // t, "q"), (k_col // t, v_col // t, "k"),
             (v_col // t, dt_col // t, "plain"), (dt_col // t, n_pad // t, "aux"))
    dummy = jnp.zeros((seq, HEAD_DIM), F32)
    proj, dt_raw = _in_proj(x, norm_g, scale1p, shift, w, q_norm, k_norm, dummy, dummy,
                            roles=roles, rope=False, q_scale=HEAD_DIM ** -0.5 * math.log2(math.e),
                            has_aux=True)
    bias = _na_bias_tables(rpb, seq // GRID_W)
    o_c = _na_attention(proj, bias, q_col=q_col, k_col=k_col, v_col=v_col, g_col=g_col)
    conv = _ssd_conv(proj, conv_w, conv_b, col=xbc_col)
    y_f, y_b = _ssd_scan(conv, dt_raw, dt_bias, a_log)
    d_x = jnp.repeat(ssd_d.astype(F32), SSD_HEAD_DIM).reshape(1, SSD_WIDTH)
    o_d = _gated_norm(y_f, y_b, conv, proj, d_x, norm_w, z_col=z_col)
    return _out_proj(o_c, o_d, w_out.astype(BF16), x, gate)


def kernel(x, c, e_norm_g, e_ada_w, e_ada_b, e_w_in, e_q_norm, e_k_norm, s5_lam_re, s5_lam_im,
           s5_log_step, s5_b_re, s5_b_im, s5_c_re, s5_c_im, s5_d, s5_w_glu, s5_b_glu, e_w_out,
           o_norm_g, o_ada_w, o_ada_b, o_w_in, o_q_norm, o_k_norm, na_rpb, ssd_conv_w, ssd_conv_b,
           ssd_dt_bias, ssd_a_log, ssd_d, ssd_norm_w, o_w_out):
    assert x.shape[0] == 1
    h = x[0]
    h = _layer_attn_s5(h, c, e_norm_g[0], e_ada_w[0], e_ada_b[0], e_w_in[0], e_q_norm[0],
                       e_k_norm[0], s5_lam_re[0], s5_lam_im[0], s5_log_step[0], s5_b_re[0],
                       s5_b_im[0], s5_c_re[0], s5_c_im[0], s5_d[0], s5_w_glu[0], s5_b_glu[0],
                       e_w_out[0])
    h = _layer_na_ssd(h, c, o_norm_g[0], o_ada_w[0], o_ada_b[0], o_w_in[0], o_q_norm[0],
                      o_k_norm[0], na_rpb[0], ssd_conv_w[0], ssd_conv_b[0], ssd_dt_bias[0],
                      ssd_a_log[0], ssd_d[0], ssd_norm_w[0], o_w_out[0])
    return h[None]
```

```python
import functools
import math

import jax
import jax.numpy as jnp
import numpy as np
from jax import lax
from jax.experimental import pallas as pl
from jax.experimental.pallas import tpu as pltpu

F32 = jnp.float32
BF16 = jnp.bfloat16
HIGHEST = lax.Precision.HIGHEST

GRID_W = 64
HEAD_DIM = 128
EPS = 1e-6
NEG_INF = -1e30
ROPE_THETA = 10000.0

A_HEADS = 8
A_KV_HEADS = 2
A_GROUP = A_HEADS // A_KV_HEADS
S5_GROUP = 16
S5_GROUPS = 64
S5_STATE = 64
S5_CHUNK = 32
C_HEADS = 8
NA_ROWS = 8
NA_COLS = 16
NA_QROWS = 4
NA_KROWS = 12
SSD_HEADS = 16
SSD_HEAD_DIM = 64
SSD_GROUPS = 2
SSD_STATE = 128
SSD_CONV = 5
SSD_CHUNK = 128
SSD_WIDTH = SSD_HEADS * SSD_HEAD_DIM
SSD_BC = SSD_GROUPS * SSD_STATE

V7X_VMEM_BYTES = 64 * 1024 * 1024
MiB = 1024 * 1024


def _cparams(semantics, vmem_mib):
    assert vmem_mib * MiB < V7X_VMEM_BYTES
    return pltpu.CompilerParams(dimension_semantics=semantics, vmem_limit_bytes=vmem_mib * MiB)


def _silu(x):
    return x * (1.0 / (1.0 + jnp.exp(-x)))


def _sigmoid(x):
    return 1.0 / (1.0 + jnp.exp(-x))


def _ada_kernel(c_ref, w_ref, b_ref, o_ref):
    sc = _silu(c_ref[...])
    o_ref[...] = jnp.sum(w_ref[...] * sc, axis=0, keepdims=True) + b_ref[...]


def _ada_mod(c, w, b):
    d, n = w.shape
    tn = 512
    out = pl.pallas_call(
        _ada_kernel,
        grid=(n // tn,),
        in_specs=[pl.BlockSpec((d, 1), lambda j: (0, 0)),
                  pl.BlockSpec((d, tn), lambda j: (0, j)),
                  pl.BlockSpec((1, tn), lambda j: (0, j))],
        out_specs=pl.BlockSpec((1, tn), lambda j: (0, j)),
        out_shape=jax.ShapeDtypeStruct((1, n), F32),
        compiler_params=_cparams(("arbitrary",), 24),
        name="ada_mod",
    )(c.astype(F32).reshape(d, 1), w, b.reshape(1, n))
    shift, scale, gate = jnp.split(out, 3, axis=-1)
    return shift, 1.0 + scale, gate


IN_TN = 256
IN_TW = 2304
IN_ROW_CHUNK = 64


def _norm_mod_kernel(x_ref, g_ref, sc_ref, sh_ref, h_ref):
    tm = x_ref.shape[0]
    gain = g_ref[...] * sc_ref[...]

    def body(r, carry):
        rows = pl.ds(pl.multiple_of(r * IN_ROW_CHUNK, IN_ROW_CHUNK), IN_ROW_CHUNK)
        xf = x_ref[rows, :]
        ms = jnp.mean(xf * xf, axis=-1, keepdims=True)
        h_ref[rows, :] = (xf * lax.rsqrt(ms + EPS) * gain + sh_ref[...]).astype(h_ref.dtype)
        return carry

    lax.fori_loop(0, tm // IN_ROW_CHUNK, body, 0, unroll=2)


def _norm_mod(x, norm_g, scale1p, shift, *, tm=512):
    seq, d = x.shape
    const = lambda i: (0, 0)
    return pl.pallas_call(
        _norm_mod_kernel,
        grid=(seq // tm,),
        in_specs=[pl.BlockSpec((tm, d), lambda i: (i, 0)),
                  pl.BlockSpec((1, d), const), pl.BlockSpec((1, d), const),
                  pl.BlockSpec((1, d), const)],
        out_specs=pl.BlockSpec((tm, d), lambda i: (i, 0)),
        out_shape=jax.ShapeDtypeStruct((seq, d), BF16),
        compiler_params=_cparams(("arbitrary",), 32),
        name="norm_mod",
    )(x, norm_g.reshape(1, d), scale1p, shift)


def _in_proj_kernel(h_ref, w_ref, qn_ref, kn_ref, cos_ref, sin_ref, *outs, roles, rope, q_scale,
                    has_aux):
    o_ref = outs[0]
    aux_ref = outs[1] if has_aux else None
    c = pl.program_id(0)
    acc = jnp.dot(h_ref[...], w_ref[...], preferred_element_type=F32)
    per_tile = IN_TW // IN_TN

    def role_of(granule):
        return next(role for lo, hi, role in roles if lo <= granule < hi)

    def qk_epilogue(a2, gain_ref, scale):
        heads = []
        for h in range(IN_TN // HEAD_DIM):
            a = a2[:, h * HEAD_DIM:(h + 1) * HEAD_DIM]
            ms = jnp.mean(a * a, axis=-1, keepdims=True)
            a = a * lax.rsqrt(ms + EPS) * gain_ref[...]
            if rope:
                a = a * cos_ref[...] + pltpu.roll(a, HEAD_DIM // 2, axis=1) * sin_ref[...]
            if scale != 1.0:
                a = a * scale
            heads.append(a)
        return jnp.concatenate(heads, axis=-1)

    for ct in range(roles[-1][1] // per_tile):
        @pl.when(c == ct)
        def _(ct=ct):
            for s in range(per_tile):
                role = role_of(ct * per_tile + s)
                cols = slice(s * IN_TN, (s + 1) * IN_TN)
                a = acc[:, cols]
                if role == "q":
                    a = qk_epilogue(a, qn_ref, q_scale)
                elif role == "k":
                    a = qk_epilogue(a, kn_ref, 1.0)
                elif role == "silu":
                    a = _silu(a)
                elif role == "aux":
                    aux_ref[...] = a
                elif role != "plain":
                    raise ValueError(role)
                o_ref[:, cols] = a.astype(o_ref.dtype)


def _in_proj(x, norm_g, scale1p, shift, w_bf16, q_gain, k_gain, cos2, sin2, *, roles, rope,
             q_scale, has_aux, tm=512):
    seq, d = x.shape
    n = w_bf16.shape[1]
    assert seq % tm == 0 and n % IN_TW == 0 and IN_TW % IN_TN == 0
    assert roles[-1][1] == n // IN_TN
    ncol = n // IN_TW
    h = _norm_mod(x, norm_g, scale1p, shift)
    row = lambda c, i: (i, 0)
    const = lambda c, i: (0, 0)
    out_shape = [jax.ShapeDtypeStruct((seq, n), BF16)]
    out_specs = [pl.BlockSpec((tm, IN_TW), lambda c, i: (i, c))]
    if has_aux:
        assert roles[-1][2] == "aux" and roles[-1][1] - roles[-1][0] == 1
        out_shape.append(jax.ShapeDtypeStruct((seq, IN_TN), F32))
        out_specs.append(
            pl.BlockSpec((tm, IN_TN), lambda c, i: (jnp.where(c == ncol - 1, i, 0), 0)))
    kern = functools.partial(_in_proj_kernel, roles=roles, rope=rope, q_scale=q_scale,
                             has_aux=has_aux)
    return pl.pallas_call(
        kern,
        grid=(ncol, seq // tm),
        in_specs=[pl.BlockSpec((tm, d), row),
                  pl.BlockSpec((d, IN_TW), lambda c, i: (0, c)),
                  pl.BlockSpec((1, HEAD_DIM), const), pl.BlockSpec((1, HEAD_DIM), const),
                  pl.BlockSpec((tm, HEAD_DIM), row), pl.BlockSpec((tm, HEAD_DIM), row)],
        out_specs=out_specs,
        out_shape=out_shape,
        compiler_params=_cparams(("arbitrary", "arbitrary"), 56),
        name="in_proj_rope" if rope else "in_proj",
    )(h, w_bf16, q_gain.reshape(1, HEAD_DIM), k_gain.reshape(1, HEAD_DIM), cos2, sin2)


GQA_TK = 1024


def _gqa_kernel(q_ref, k_ref, v_ref, g_ref, o_ref, acc_scr, m_scr, s_scr):
    tq = q_ref.shape[0]
    nk = k_ref.shape[0] // GQA_TK
    q_all = jnp.concatenate(
        [q_ref[:, h * HEAD_DIM:(h + 1) * HEAD_DIM] for h in range(A_GROUP)], axis=0)
    acc_scr[...] = jnp.zeros(acc_scr.shape, F32)
    m_scr[...] = jnp.full(m_scr.shape, -jnp.inf, F32)
    ones = jnp.ones((GQA_TK, HEAD_DIM), BF16)

    def key_rows(kc):
        return pl.ds(pl.multiple_of(kc * GQA_TK, GQA_TK), GQA_TK)

    def scores(kc):
        return lax.dot_general(q_all, k_ref[key_rows(kc), :], (((1,), (1,)), ((), ())),
                               preferred_element_type=F32)

    def softmax_pv(slot, kc):
        s = s_scr[slot]
        v1 = jnp.concatenate([v_ref[key_rows(kc), :], ones], axis=1)
        m_prev = m_scr[...]
        m_new = jnp.maximum(m_prev, jnp.max(s, axis=-1, keepdims=True))
        alpha = jnp.exp2(m_prev - m_new)
        p = jnp.concatenate(
            [jnp.exp2(s[:, j * HEAD_DIM:(j + 1) * HEAD_DIM] - m_new).astype(BF16)
             for j in range(GQA_TK // HEAD_DIM)], axis=1)
        pv = jnp.dot(p, v1, preferred_element_type=F32)
        acc_scr[...] = jnp.concatenate([alpha, alpha], axis=1) * acc_scr[...] + pv
        m_scr[...] = m_new

    s_scr[0] = scores(0)

    def pair(i, carry):
        kc = 2 * i
        s_scr[1] = scores(kc + 1)
        softmax_pv(0, kc)
        s_scr[0] = scores(kc + 2)
        softmax_pv(1, kc + 1)
        return carry

    lax.fori_loop(0, nk // 2 - 1, pair, 0)
    s_scr[1] = scores(nk - 1)
    softmax_pv(0, nk - 2)
    softmax_pv(1, nk - 1)
    for h in range(A_GROUP):
        cols = slice(h * HEAD_DIM, (h + 1) * HEAD_DIM)
        a = acc_scr[h * tq:(h + 1) * tq, :]
        o = a[:, :HEAD_DIM] * (1.0 / a[:, HEAD_DIM:])
        o_ref[:, cols] = (o * g_ref[:, cols].astype(F32)).astype(o_ref.dtype)


def _gqa_attention(proj, *, q_col, k_col, v_col, g_col, tq=256):
    seq = proj.shape[0]
    gw = A_GROUP * HEAD_DIM
    assert q_col % gw == 0 and g_col % gw == 0 and k_col % HEAD_DIM == 0 and v_col % HEAD_DIM == 0
    assert seq % (2 * GQA_TK) == 0 and seq % tq == 0
    return pl.pallas_call(
        _gqa_kernel,
        grid=(A_KV_HEADS, seq // tq),
        in_specs=[pl.BlockSpec((tq, gw), lambda kh, qi: (qi, q_col // gw + kh)),
                  pl.BlockSpec((seq, HEAD_DIM), lambda kh, qi: (0, k_col // HEAD_DIM + kh)),
                  pl.BlockSpec((seq, HEAD_DIM), lambda kh, qi: (0, v_col // HEAD_DIM + kh)),
                  pl.BlockSpec((tq, gw), lambda kh, qi: (qi, g_col // gw + kh))],
        out_specs=pl.BlockSpec((tq, gw), lambda kh, qi: (qi, kh)),
        out_shape=jax.ShapeDtypeStruct((seq, A_HEADS * HEAD_DIM), BF16),
        scratch_shapes=[pltpu.VMEM((A_GROUP * tq, 2 * HEAD_DIM), F32),
                        pltpu.VMEM((A_GROUP * tq, HEAD_DIM), F32),
                        pltpu.VMEM((2, A_GROUP * tq, GQA_TK), F32)],
        compiler_params=_cparams(("arbitrary", "arbitrary"), 56),
        name="gqa_attention",
    )(proj, proj, proj, proj)


def _s5_tables(lam_re, lam_im, log_step, b_re, b_im, c_re, c_im, s5_d):
    t = S5_CHUNK
    g, p, hh = S5_GROUPS, S5_STATE, S5_GROUP
    lr = lam_re.astype(F32)
    li = lam_im.astype(F32)
    dt = jnp.exp(log_step.astype(F32))[..., None]
    mag = jnp.exp(lr * dt)
    ab_re = mag * jnp.cos(li * dt)
    ab_im = mag * jnp.sin(li * dt)
    den = lr * lr + li * li
    num_re = ab_re - 1.0
    f_re = (num_re * lr + ab_im * li) / den
    f_im = (ab_im * lr - num_re * li) / den
    br = b_re.astype(F32)
    bi = b_im.astype(F32)
    bb_re = f_re[..., None] * br - f_im[..., None] * bi
    bb_im = f_re[..., None] * bi + f_im[..., None] * br
    kk = jnp.arange(t + 1, dtype=F32)[None, None, :, None]
    pmag = jnp.exp(kk * (lr * dt)[:, :, None, :])
    ang = kk * (li * dt)[:, :, None, :]
    pw_re = pmag * jnp.cos(ang)
    pw_im = pmag * jnp.sin(ang)
    bt_re = jnp.swapaxes(bb_re, -1, -2)[:, :, None]
    bt_im = jnp.swapaxes(bb_im, -1, -2)[:, :, None]
    pr = pw_re[:, :, :, None, :]
    pi = pw_im[:, :, :, None, :]
    wt_re = pr * bt_re - pi * bt_im
    wt_im = pr * bt_im + pi * bt_re
    cr = c_re.astype(F32)
    ci = c_im.astype(F32)
    cc_re = cr[:, :, None]
    cc_im = ci[:, :, None]
    cl_re = cc_re * pr - cc_im * pi
    cl_im = cc_re * pi + cc_im * pr
    bt_cat = jnp.concatenate([bt_re[:, :, 0], -bt_im[:, :, 0]], axis=-1)
    cl_cat = jnp.concatenate([cl_re, cl_im], axis=-1).reshape(2, g, (t + 1) * hh, 2 * p)
    kj = jnp.einsum('dgjq,dgnq->dgjn', bt_cat, cl_cat, precision=HIGHEST)
    kj = kj.reshape(2, g, hh, t + 1, hh)
    dmat = s5_d.astype(F32).reshape(g, hh)[:, :, None] * jnp.eye(hh, dtype=F32)[None]
    k0 = kj[0, :, :, 0] + kj[1, :, :, 0] + dmat
    kb = kj[1, :, :, 1:t][:, :, ::-1]
    kf = kj[0, :, :, 1:t]
    kflat = jnp.concatenate([kb, k0[:, :, None], kf], axis=2).reshape(g, hh, (2 * t - 1) * hh)
    m = jnp.stack([kflat[:, :, (t - 1 - tp) * hh:(2 * t - 1 - tp) * hh] for tp in range(t)],
                  axis=1).reshape(g, t * hh, t * hh)
    flat = lambda a: a.reshape(g, t * hh, p)
    fmat = jnp.concatenate([flat(wt_re[0, :, :t][:, ::-1]), flat(wt_re[1, :, :t]),
                            flat(wt_im[0, :, :t][:, ::-1]), flat(wt_im[1, :, :t])], axis=-1)
    e_t = lambda a: jnp.swapaxes(flat(a), 1, 2)
    emat = jnp.concatenate([e_t(cl_re[0, :, 1:t + 1]), e_t(cl_re[1, :, 1:t + 1][:, ::-1]),
                            -e_t(cl_im[0, :, 1:t + 1]), -e_t(cl_im[1, :, 1:t + 1][:, ::-1])],
                           axis=1)
    a_re = jnp.concatenate([pw_re[0, :, t], pw_re[1, :, t]], axis=-1)
    a_im = jnp.concatenate([pw_im[0, :, t], pw_im[1, :, t]], axis=-1)
    return m.astype(BF16), fmat.astype(BF16), emat.astype(BF16), a_re, a_im


S5_GPS = 4


def _s5_state_kernel(u_ref, f_ref, o_ref):
    sw = f_ref.shape[2]
    for a in range(S5_GPS):
        o_ref[:, a * sw:(a + 1) * sw] = jnp.dot(u_ref[a], f_ref[a], preferred_element_type=F32)


def _s5_fwd_lanes(shape):
    lane = lax.broadcasted_iota(jnp.int32, shape, len(shape) - 1)
    return (lane % (2 * S5_STATE)) < S5_STATE


def _s5_scan_kernel(xf_ref, xb_ref, ar_ref, ai_ref, hf_ref, hb_ref, re_scr, im_scr):
    cb = xf_ref.shape[0]
    w = 2 * S5_STATE

    @pl.when(pl.program_id(0) == 0)
    def _():
        re_scr[...] = jnp.zeros(re_scr.shape, F32)
        im_scr[...] = jnp.zeros(im_scr.shape, F32)

    ar = ar_ref[...]
    ai = ai_ref[...]
    fwd = _s5_fwd_lanes(ar.shape)

    def body(i, carry):
        re, im = carry
        j = cb - 1 - i
        st = jnp.concatenate([re, im], axis=-1).astype(hf_ref.dtype)
        hf_ref[i] = st
        hb_ref[j] = st
        xf = xf_ref[i]
        xb = xb_ref[j]
        x_re = jnp.where(fwd, xf[:, :w], xb[:, :w])
        x_im = jnp.where(fwd, xf[:, w:], xb[:, w:])
        return ar * re - ai * im + x_re, ar * im + ai * re + x_im

    re, im = lax.fori_loop(0, cb, body, (re_scr[...], im_scr[...]), unroll=2)
    re_scr[...] = re
    im_scr[...] = im


def _s5_out_kernel(u_ref, m_ref, hf_ref, hb_ref, e_ref, o_ref):
    sw = e_ref.shape[1]
    c0 = math.sqrt(2.0 / math.pi)
    for a in range(S5_GPS):
        lanes = slice(a * sw, (a + 1) * sw)
        hf = hf_ref[:, lanes]
        y = jnp.dot(u_ref[a], m_ref[a], preferred_element_type=F32)
        h = jnp.where(_s5_fwd_lanes(hf.shape), hf, hb_ref[:, lanes])
        y = y + jnp.dot(h, e_ref[a], preferred_element_type=F32)
        y = 0.5 * y * (1.0 + jnp.tanh(c0 * (y + 0.044715 * (y * y * y))))
        o_ref[a] = y.astype(o_ref.dtype)


S5_SUPER = 128 // S5_GROUP


def _block_swap_matrix():
    n = S5_SUPER
    idx = np.arange(n * n * S5_GROUP)
    b, a, j = idx // (n * S5_GROUP), (idx // S5_GROUP) % n, idx % S5_GROUP
    sel = np.zeros((idx.size, idx.size), np.float32)
    sel[idx, a * n * S5_GROUP + b * S5_GROUP + j] = 1.0
    return jnp.asarray(sel, BF16)


S5_TBLK = 16


def _s5_gather_kernel(x_ref, sel_ref, o_ref, rows_scr):
    nc = x_ref.shape[0]
    rows_scr[...] = x_ref[...].astype(F32).reshape(nc * S5_TBLK, 128)
    for q in range(S5_TBLK // S5_SUPER):
        lhs = jnp.concatenate(
            [rows_scr[pl.ds(S5_SUPER * q + b, nc, stride=S5_TBLK), :].astype(BF16)
             for b in range(S5_SUPER)], axis=1)
        out = jnp.dot(lhs, sel_ref[...], preferred_element_type=F32).astype(o_ref.dtype)
        for a in range(S5_SUPER):
            o_ref[a, :, q * 128:(q + 1) * 128] = out[:, a * 128:(a + 1) * 128]


def _s5_gather(proj, col):
    seq, n = proj.shape
    t, g = S5_CHUNK, S5_GROUPS
    nc = seq // t
    assert n % 128 == 0 and col % 128 == 0 and t % S5_TBLK == 0
    lanes = S5_TBLK * S5_GROUP
    return pl.pallas_call(
        _s5_gather_kernel,
        grid=(g // S5_SUPER, t // S5_TBLK),
        in_specs=[pl.BlockSpec((nc, S5_TBLK, 128), lambda sg, h: (0, h, col // 128 + sg)),
                  pl.BlockSpec((1024, 1024), lambda sg, h: (0, 0))],
        out_specs=pl.BlockSpec((S5_SUPER, nc, lanes), lambda sg, h: (sg, 0, h)),
        out_shape=jax.ShapeDtypeStruct((g, nc, t * S5_GROUP), BF16),
        scratch_shapes=[pltpu.VMEM((nc * S5_TBLK, 128), F32)],
        compiler_params=_cparams(("arbitrary", "arbitrary"), 40),
        name="s5_gather",
    )(proj.reshape(nc, t, n), _block_swap_matrix())


def _s5_scatter_kernel(y_ref, sel_ref, o_ref, rows_scr):
    nc = y_ref.shape[1]
    for q in range(S5_TBLK // S5_SUPER):
        lhs = jnp.concatenate([y_ref[a, :, q * 128:(q + 1) * 128] for a in range(S5_SUPER)],
                              axis=1)
        out = jnp.dot(lhs, sel_ref[...], preferred_element_type=F32)
        for b in range(S5_SUPER):
            rows_scr[pl.ds(S5_SUPER * q + b, nc, stride=S5_TBLK), :] = out[:, b * 128:(b + 1) * 128]
    o_ref[...] = rows_scr[...].reshape(nc, S5_TBLK, 128).astype(o_ref.dtype)


def _s5_scatter(yg):
    g, nc, th = yg.shape
    t = S5_CHUNK
    width = g * S5_GROUP
    lanes = S5_TBLK * S5_GROUP
    y3 = pl.pallas_call(
        _s5_scatter_kernel,
        grid=(g // S5_SUPER, t // S5_TBLK),
        in_specs=[pl.BlockSpec((S5_SUPER, nc, lanes), lambda sg, h: (sg, 0, h)),
                  pl.BlockSpec((1024, 1024), lambda sg, h: (0, 0))],
        out_specs=pl.BlockSpec((nc, S5_TBLK, 128), lambda sg, h: (0, h, sg)),
        out_shape=jax.ShapeDtypeStruct((nc, t, width), BF16),
        scratch_shapes=[pltpu.VMEM((nc * S5_TBLK, 128), F32)],
        compiler_params=_cparams(("arbitrary", "arbitrary"), 40),
        name="s5_scatter",
    )(yg, _block_swap_matrix())
    return y3.reshape(nc * t, width)


def _s5_mixer(proj, u_col, tables):
    m, fmat, emat, a_re, a_im = tables
    seq = proj.shape[0]
    t, g, hh, p = S5_CHUNK, S5_GROUPS, S5_GROUP, S5_STATE
    nc = seq // t
    th = t * hh
    sw = 4 * p
    ug = _s5_gather(proj, u_col)
    hend = pl.pallas_call(
        _s5_state_kernel,
        grid=(g // S5_GPS,),
        in_specs=[pl.BlockSpec((S5_GPS, nc, th), lambda i: (i, 0, 0)),
                  pl.BlockSpec((S5_GPS, th, sw), lambda i: (i, 0, 0))],
        out_specs=pl.BlockSpec((nc, S5_GPS * sw), lambda i: (0, i)),
        out_shape=jax.ShapeDtypeStruct((nc, g * sw), F32),
        compiler_params=_cparams(("arbitrary",), 32),
        name="s5_chunk_state",
    )(ug, fmat)
    hend3 = hend.reshape(nc, g, sw)
    cb = min(nc, 64)
    nb = nc // cb
    fwd_blk = lambda c: (c, 0, 0)
    bwd_blk = lambda c: (nb - 1 - c, 0, 0)
    hf, hb = pl.pallas_call(
        _s5_scan_kernel,
        grid=(nb,),
        in_specs=[pl.BlockSpec((cb, g, sw), fwd_blk), pl.BlockSpec((cb, g, sw), bwd_blk),
                  pl.BlockSpec((g, 2 * p), lambda c: (0, 0)),
                  pl.BlockSpec((g, 2 * p), lambda c: (0, 0))],
        out_specs=[pl.BlockSpec((cb, g, sw), fwd_blk), pl.BlockSpec((cb, g, sw), bwd_blk)],
        out_shape=[jax.ShapeDtypeStruct((nc, g, sw), BF16)] * 2,
        scratch_shapes=[pltpu.VMEM((g, 2 * p), F32), pltpu.VMEM((g, 2 * p), F32)],
        compiler_params=_cparams(("arbitrary",), 48),
        name="s5_chunk_scan",
    )(hend3, hend3, a_re, a_im)
    yg = pl.pallas_call(
        _s5_out_kernel,
        grid=(g // S5_GPS,),
        in_specs=[pl.BlockSpec((S5_GPS, nc, th), lambda i: (i, 0, 0)),
                  pl.BlockSpec((S5_GPS, th, th), lambda i: (i, 0, 0)),
                  pl.BlockSpec((nc, S5_GPS * sw), lambda i: (0, i)),
                  pl.BlockSpec((nc, S5_GPS * sw), lambda i: (0, i)),
                  pl.BlockSpec((S5_GPS, sw, th), lambda i: (i, 0, 0))],
        out_specs=pl.BlockSpec((S5_GPS, nc, th), lambda i: (i, 0, 0)),
        out_shape=jax.ShapeDtypeStruct((g, nc, th), BF16),
        compiler_params=_cparams(("arbitrary",), 32),
        name="s5_output",
    )(ug, m, hf.reshape(nc, g * sw), hb.reshape(nc, g * sw), emat)
    return _s5_scatter(yg)


def _glu_kernel(y_ref, wv_ref, wg_ref, bv_ref, bg_ref, s_ref, o_ref):
    y = y_ref[...]
    val = jnp.dot(y, wv_ref[...], preferred_element_type=F32) + bv_ref[...]
    gt = jnp.dot(y, wg_ref[...], preferred_element_type=F32) + bg_ref[...]
    o_ref[...] = (val * _sigmoid(gt) * _silu(s_ref[...].astype(F32))).astype(o_ref.dtype)


def _glu(y, w_bf16, b, proj, *, s_col, tm=1024, tn=512):
    seq, kdim = y.shape
    width = w_bf16.shape[1] // 2
    assert s_col % tn == 0
    nj = width // tn
    b2 = b.reshape(1, 2 * width).astype(F32)
    return pl.pallas_call(
        _glu_kernel,
        grid=(seq // tm, nj),
        in_specs=[pl.BlockSpec((tm, kdim), lambda i, j: (i, 0)),
                  pl.BlockSpec((kdim, tn), lambda i, j: (0, j)),
                  pl.BlockSpec((kdim, tn), lambda i, j: (0, nj + j)),
                  pl.BlockSpec((1, tn), lambda i, j: (0, j)),
                  pl.BlockSpec((1, tn), lambda i, j: (0, nj + j)),
                  pl.BlockSpec((tm, tn), lambda i, j: (i, s_col // tn + j))],
        out_specs=pl.BlockSpec((tm, tn), lambda i, j: (i, j)),
        out_shape=jax.ShapeDtypeStruct((seq, width), BF16),
        compiler_params=_cparams(("arbitrary", "arbitrary"), 40),
        name="s5_glu",
    )(y, w_bf16, w_bf16, b2, b2, proj)


def _out_proj_kernel(a_ref, b_ref, wa_ref, wb_ref, x_ref, gate_ref, o_ref):
    acc = jnp.dot(a_ref[...], wa_ref[...], preferred_element_type=F32)
    acc = acc + jnp.dot(b_ref[...], wb_ref[...], preferred_element_type=F32)
    o_ref[...] = x_ref[...] + gate_ref[...] * acc


def _out_proj(oa, ob, w_bf16, x, gate, *, tm=512, tn=2048):
    seq, half = oa.shape
    d = w_bf16.shape[1]
    tn = min(tn, d)
    return pl.pallas_call(
        _out_proj_kernel,
        grid=(seq // tm, d // tn),
        in_specs=[pl.BlockSpec((tm, half), lambda i, j: (i, 0)),
                  pl.BlockSpec((tm, half), lambda i, j: (i, 0)),
                  pl.BlockSpec((half, tn), lambda i, j: (0, j)),
                  pl.BlockSpec((half, tn), lambda i, j: (1, j)),
                  pl.BlockSpec((tm, tn), lambda i, j: (i, j)),
                  pl.BlockSpec((1, tn), lambda i, j: (0, j))],
        out_specs=pl.BlockSpec((tm, tn), lambda i, j: (i, j)),
        out_shape=jax.ShapeDtypeStruct((seq, d), F32),
        compiler_params=_cparams(("arbitrary", "arbitrary"), 48),
        name="out_proj",
    )(oa, ob, w_bf16, w_bf16, x, gate)


def _na_bias_tables(rpb, rows):
    w = GRID_W
    nrb = rows // NA_QROWS
    assert rows >= NA_KROWS + NA_QROWS
    col = np.arange(w)
    col_start = np.clip(col - NA_COLS // 2, 0, w - NA_COLS)
    col_ok = (col[None, :] >= col_start[:, None]) & (col[None, :] < col_start[:, None] + NA_COLS)
    dc = np.clip(col[None, :] - col[:, None], -(NA_COLS - 1), NA_COLS - 1) + NA_COLS - 1
    onehot = (dc[None] == np.arange(2 * NA_COLS - 1)[:, None, None]).astype(np.float32)
    tt = jnp.einsum('hrd,dqk->hrqk', rpb.astype(F32) * math.log2(math.e), jnp.asarray(onehot),
                    precision=HIGHEST)
    tt = jnp.where(jnp.asarray(col_ok)[None, None], tt, NEG_INF)
    n_dr = 2 * NA_ROWS - 1
    tt = jnp.concatenate([tt, jnp.full((rpb.shape[0], 1, w, w), NEG_INF, F32)], axis=1)
    sel = np.zeros((3, NA_QROWS, NA_KROWS, n_dr + 1), np.float32)
    for ti, rb in enumerate((0, 1, nrb - 1)):
        ks = min(max(rb * NA_QROWS - NA_ROWS // 2, 0), rows - NA_KROWS)
        for rl in range(NA_QROWS):
            r = rb * NA_QROWS + rl
            rs = min(max(r - NA_ROWS // 2, 0), rows - NA_ROWS)
            for kl in range(NA_KROWS):
                kr = ks + kl
                sel[ti, rl, kl, kr - r + NA_ROWS - 1 if rs <= kr < rs + NA_ROWS else n_dr] = 1.0
    tab = jnp.einsum('trkd,hdqc->thrqkc', jnp.asarray(sel), tt, precision=HIGHEST)
    return tab.reshape(3, rpb.shape[0], NA_QROWS * w, NA_KROWS * w)


def _na_kernel(q_ref, k0_ref, k1_ref, k2_ref, v0_ref, v1_ref, v2_ref, b_ref, g_ref, o_ref):
    ones = jnp.ones((k0_ref.shape[0] * 3, HEAD_DIM), BF16)
    outs = []
    for h in range(C_HEADS):
        cols = slice(h * HEAD_DIM, (h + 1) * HEAD_DIM)
        k = jnp.concatenate([k0_ref[:, cols], k1_ref[:, cols], k2_ref[:, cols]], axis=0)
        v = jnp.concatenate([v0_ref[:, cols], v1_ref[:, cols], v2_ref[:, cols]], axis=0)
        s = lax.dot_general(q_ref[:, cols], k, (((1,), (1,)), ((), ())),
                            preferred_element_type=F32) + b_ref[0, h]
        m = jnp.max(s, axis=-1, keepdims=True)
        p = jnp.exp2(s - m).astype(BF16)
        pv = jnp.dot(p, jnp.concatenate([v, ones], axis=1), preferred_element_type=F32)
        o = pv[:, :HEAD_DIM] * (1.0 / pv[:, HEAD_DIM:])
        outs.append((o * _silu(g_ref[:, cols].astype(F32))).astype(o_ref.dtype))
    o_ref[...] = jnp.concatenate(outs, axis=1)


def _na_attention(proj, bias, *, q_col, k_col, v_col, g_col):
    seq = proj.shape[0]
    tq = NA_QROWS * GRID_W
    nrb = seq // tq
    nkb = NA_KROWS // NA_QROWS
    cw = C_HEADS * HEAD_DIM
    assert q_col % cw == 0 and k_col % cw == 0 and v_col % cw == 0 and g_col % cw == 0

    def kv_spec(col, off):
        return pl.BlockSpec((tq, cw), lambda rb: (jnp.clip(rb - 1, 0, nrb - nkb) + off, col // cw))

    btype = lambda rb: (jnp.where(rb == 0, 0, jnp.where(rb == nrb - 1, 2, 1)), 0, 0, 0)
    return pl.pallas_call(
        _na_kernel,
        grid=(nrb,),
        in_specs=[pl.BlockSpec((tq, cw), lambda rb: (rb, q_col // cw)),
                  kv_spec(k_col, 0), kv_spec(k_col, 1), kv_spec(k_col, 2),
                  kv_spec(v_col, 0), kv_spec(v_col, 1), kv_spec(v_col, 2),
                  pl.BlockSpec((1, C_HEADS, tq, nkb * tq), btype),
                  pl.BlockSpec((tq, cw), lambda rb: (rb, g_col // cw))],
        out_specs=pl.BlockSpec((tq, cw), lambda rb: (rb, 0)),
        out_shape=jax.ShapeDtypeStruct((seq, cw), BF16),
        compiler_params=_cparams(("arbitrary",), 48),
        name="na_attention",
    )(proj, proj, proj, proj, proj, proj, proj, bias, proj)


CONV_HALO = 8


def _conv_kernel(prev_ref, cur_ref, next_ref, w_ref, b_ref, o_ref):
    i = pl.program_id(0)
    tm = cur_ref.shape[0]
    prev = jnp.where(i == 0, 0.0, prev_ref[...].astype(F32))
    nxt = jnp.where(i == pl.num_programs(0) - 1, 0.0, next_ref[...].astype(F32))
    ext = jnp.concatenate([prev, cur_ref[...].astype(F32), nxt], axis=0)
    acc = jnp.zeros(cur_ref.shape, F32) + b_ref[...]
    for kk in range(SSD_CONV):
        start = CONV_HALO - SSD_CONV // 2 + kk
        acc = acc + ext[start:start + tm, :] * w_ref[kk:kk + 1, :]
    o_ref[...] = _silu(acc).astype(o_ref.dtype)


def _ssd_conv(proj, conv_w, conv_b, *, col, tm=512, tc=512):
    seq = proj.shape[0]
    ch = conv_w.shape[1]
    assert col % tc == 0 and ch % tc == 0
    nh = tm // CONV_HALO
    nblk = seq // CONV_HALO
    cb = col // tc
    return pl.pallas_call(
        _conv_kernel,
        grid=(seq // tm, ch // tc),
        in_specs=[pl.BlockSpec((CONV_HALO, tc), lambda i, j: (jnp.maximum(i * nh - 1, 0), cb + j)),
                  pl.BlockSpec((tm, tc), lambda i, j: (i, cb + j)),
                  pl.BlockSpec((CONV_HALO, tc),
                               lambda i, j: (jnp.minimum((i + 1) * nh, nblk - 1), cb + j)),
                  pl.BlockSpec((SSD_CONV, tc), lambda i, j: (0, j)),
                  pl.BlockSpec((1, tc), lambda i, j: (0, j))],
        out_specs=pl.BlockSpec((tm, tc), lambda i, j: (i, j)),
        out_shape=jax.ShapeDtypeStruct((seq, ch), BF16),
        compiler_params=_cparams(("arbitrary", "arbitrary"), 32),
        name="ssd_conv",
    )(proj, proj, proj, conv_w.astype(F32), conv_b.reshape(1, ch).astype(F32))


def _split_dot(a, b_bf16):
    hi = a.astype(BF16)
    lo = (a - hi.astype(F32)).astype(BF16)
    return (jnp.dot(hi, b_bf16, preferred_element_type=F32)
            + jnp.dot(lo, b_bf16, preferred_element_type=F32))


def _expand_heads(a, ex_bf16):
    return jnp.dot(a.astype(BF16), ex_bf16, preferred_element_type=F32)


def _ssd_direction(d, xs_ref, b_ref, c_ref, dt_ref, bias_ref, a_ref, ex, o_ref, st_scr):
    t = xs_ref.shape[0]
    gw = SSD_WIDTH // SSD_GROUPS
    hpg = SSD_HEADS // SSD_GROUPS
    row = lax.broadcasted_iota(jnp.int32, (t, t), 0)
    colm = lax.broadcasted_iota(jnp.int32, (t, t), 1)
    tri = (row >= colm) if d == 0 else (row <= colm)
    tri_b = tri.astype(BF16)

    z = dt_ref[...] + bias_ref[d]
    dt = jnp.maximum(z, 0.0) + jnp.log(1.0 + jnp.exp(-jnp.abs(z)))
    adt = dt * a_ref[d]
    a1 = adt.astype(BF16)
    r1 = adt - a1.astype(F32)
    a2 = r1.astype(BF16)
    a3 = (r1 - a2.astype(F32)).astype(BF16)
    r = (jnp.dot(tri_b, a1, preferred_element_type=F32)
         + jnp.dot(tri_b, a2, preferred_element_type=F32)
         + jnp.dot(tri_b, a3, preferred_element_type=F32))
    tot = r[t - 1:t, :] if d == 0 else r[0:1, :]
    dt_x = _expand_heads(dt, ex)
    er_x = _expand_heads(jnp.exp(r), ex)
    sd_x = _expand_heads(jnp.exp(tot - r), ex)
    et_x = _split_dot(jnp.exp(tot), ex)
    r_t = r.T

    xs = xs_ref[...].astype(F32)
    xd = xs * dt_x
    xd_b = xd.astype(BF16)
    xdd_b = (xd * sd_x).astype(BF16)

    y_parts = []
    for g in range(SSD_GROUPS):
        bg = b_ref[:, g * SSD_STATE:(g + 1) * SSD_STATE]
        cg = c_ref[:, g * SSD_STATE:(g + 1) * SSD_STATE]
        cb = lax.dot_general(cg, bg, (((1,), (1,)), ((), ())), preferred_element_type=F32)
        lanes = slice(g * gw, (g + 1) * gw)
        s_prev = st_scr[d, g]
        y_off = jnp.dot(cg, s_prev.astype(BF16), preferred_element_type=F32) * er_x[:, lanes]
        s_loc = lax.dot_general(bg, xdd_b[:, lanes], (((0,), (0,)), ((), ())),
                                preferred_element_type=F32)
        st_scr[d, g] = s_prev * et_x[:, lanes] + s_loc
        for hh in range(hpg):
            h = g * hpg + hh
            decay = jnp.exp(jnp.where(tri, r[:, h:h + 1] - r_t[h:h + 1, :], NEG_INF))
            sc = (cb * decay).astype(BF16)
            hl = slice(h * SSD_HEAD_DIM, (h + 1) * SSD_HEAD_DIM)
            y_parts.append(jnp.dot(sc, xd_b[:, hl], preferred_element_type=F32)
                           + y_off[:, hh * SSD_HEAD_DIM:(hh + 1) * SSD_HEAD_DIM])
    o_ref[...] = jnp.concatenate(y_parts, axis=-1).astype(o_ref.dtype)


def _ssd_scan_kernel(xf_ref, bf_ref, cf_ref, dtf_ref, xb_ref, bb_ref, cb_ref, dtb_ref,
                     bias_ref, a_ref, ex_ref, of_ref, ob_ref, st_scr):
    @pl.when(pl.program_id(0) == 0)
    def _():
        st_scr[...] = jnp.zeros(st_scr.shape, F32)

    ex = ex_ref[...]
    _ssd_direction(0, xf_ref, bf_ref, cf_ref, dtf_ref, bias_ref, a_ref, ex, of_ref, st_scr)
    _ssd_direction(1, xb_ref, bb_ref, cb_ref, dtb_ref, bias_ref, a_ref, ex, ob_ref, st_scr)


def _ssd_scan(conv, dt_raw, dt_bias, a_log):
    seq = conv.shape[0]
    t = SSD_CHUNK
    nc = seq // t
    a = -jnp.exp(a_log.astype(F32)).reshape(2, 1, SSD_HEADS)
    bias = dt_bias.astype(F32).reshape(2, 1, SSD_HEADS)
    ex = jnp.repeat(jnp.eye(SSD_HEADS, dtype=BF16), SSD_HEAD_DIM, axis=1)
    dt_f = dt_raw[:, :SSD_HEADS]
    dt_b = dt_raw[:, SSD_HEADS:2 * SSD_HEADS]
    nxb = SSD_WIDTH // SSD_BC
    fwd = lambda c: c
    bwd = lambda c: nc - 1 - c

    def chunk_specs(pos):
        return [pl.BlockSpec((t, SSD_WIDTH), lambda c: (pos(c), 0)),
                pl.BlockSpec((t, SSD_BC), lambda c: (pos(c), nxb)),
                pl.BlockSpec((t, SSD_BC), lambda c: (pos(c), nxb + 1)),
                pl.BlockSpec((t, SSD_HEADS), lambda c: (pos(c), 0))]

    const3 = pl.BlockSpec((2, 1, SSD_HEADS), lambda c: (0, 0, 0))
    return pl.pallas_call(
        _ssd_scan_kernel,
        grid=(nc,),
        in_specs=chunk_specs(fwd) + chunk_specs(bwd) + [
            const3, const3, pl.BlockSpec((SSD_HEADS, SSD_WIDTH), lambda c: (0, 0))],
        out_specs=[pl.BlockSpec((t, SSD_WIDTH), lambda c: (fwd(c), 0)),
                   pl.BlockSpec((t, SSD_WIDTH), lambda c: (bwd(c), 0))],
        out_shape=[jax.ShapeDtypeStruct((seq, SSD_WIDTH), BF16)] * 2,
        scratch_shapes=[pltpu.VMEM((2, SSD_GROUPS, SSD_STATE, SSD_WIDTH // SSD_GROUPS), F32)],
        compiler_params=_cparams(("arbitrary",), 40),
        name="ssd_scan",
    )(conv, conv, conv, dt_f, conv, conv, conv, dt_b, bias, a, ex)


def _gated_norm_kernel(yf_ref, yb_ref, xs_ref, z_ref, d_ref, w_ref, o_ref):
    y = (yf_ref[...].astype(F32) + yb_ref[...].astype(F32)
         + d_ref[...] * xs_ref[...].astype(F32))
    y = y * _silu(z_ref[...].astype(F32))
    ms = jnp.mean(y * y, axis=-1, keepdims=True)
    o_ref[...] = (y * lax.rsqrt(ms + EPS) * w_ref[...]).astype(o_ref.dtype)


def _gated_norm(y_f, y_b, conv, proj, d_x, norm_w, *, z_col, tm=512):
    seq = conv.shape[0]
    w = SSD_WIDTH
    assert z_col % w == 0
    return pl.pallas_call(
        _gated_norm_kernel,
        grid=(seq // tm,),
        in_specs=[pl.BlockSpec((tm, w), lambda i: (i, 0)),
                  pl.BlockSpec((tm, w), lambda i: (i, 0)),
                  pl.BlockSpec((tm, w), lambda i: (i, 0)),
                  pl.BlockSpec((tm, w), lambda i: (i, z_col // w)),
                  pl.BlockSpec((1, w), lambda i: (0, 0)),
                  pl.BlockSpec((1, w), lambda i: (0, 0))],
        out_specs=pl.BlockSpec((tm, w), lambda i: (i, 0)),
        out_shape=jax.ShapeDtypeStruct((seq, w), BF16),
        compiler_params=_cparams(("arbitrary",), 32),
        name="ssd_gated_norm",
    )(y_f, y_b, conv, proj, d_x, norm_w.reshape(1, w).astype(F32))


def _rope_tables(seq):
    rows = seq // GRID_W
    n_axis = HEAD_DIM // 4
    inv = ROPE_THETA ** (-np.arange(n_axis, dtype=np.float64) / n_axis)
    ang_r = np.arange(rows, dtype=np.float64)[:, None] * inv
    ang_c = np.arange(GRID_W, dtype=np.float64)[:, None] * inv

    def expand(fr, fc):
        fr = jnp.asarray(fr.astype(np.float32))
        fc = jnp.asarray(fc.astype(np.float32))
        tab = jnp.concatenate(
            [jnp.broadcast_to(fr[:, None, :], (rows, GRID_W, n_axis)),
             jnp.broadcast_to(fc[None, :, :], (rows, GRID_W, n_axis))], axis=-1)
        return tab.reshape(seq, 2 * n_axis)

    cos = expand(np.cos(ang_r), np.cos(ang_c))
    sin = expand(np.sin(ang_r), np.sin(ang_c))
    return jnp.concatenate([cos, cos], axis=-1), jnp.concatenate([-sin, sin], axis=-1)


def _deinterleave_perm():
    return np.concatenate([np.arange(0, HEAD_DIM, 2), np.arange(1, HEAD_DIM, 2)])


def _layer_attn_s5(x, c, norm_g, ada_w, ada_b, w_in, q_norm, k_norm, lam_re, lam_im, log_step,
                   b_re, b_im, c_re, c_im, s5_d, w_glu, b_glu, w_out):
    seq, d = x.shape
    shift, scale1p, gate = _ada_mod(c, ada_w, ada_b)
    aw = A_HEADS * HEAD_DIM
    akw = A_KV_HEADS * HEAD_DIM
    perm = _deinterleave_perm()
    nqk = (aw + akw) // HEAD_DIM
    colperm = (np.arange(nqk)[:, None] * HEAD_DIM + perm[None, :]).reshape(-1)
    w = jnp.concatenate([w_in[:, colperm], w_in[:, aw + akw:]], axis=1).astype(BF16)
    cos2, sin2 = _rope_tables(seq)
    t = IN_TN
    q_col, k_col, v_col = 0, aw, aw + akw
    g_col = aw + 2 * akw
    u_col = g_col + aw
    gb_col = u_col + d // 2
    roles = ((q_col // t, k_col // t, "q"), (k_col // t, v_col // t, "k"),
             (v_col // t, g_col // t, "plain"), (g_col // t, u_col // t, "silu"),
             (u_col // t, (gb_col + d // 2) // t, "plain"))
    (proj,) = _in_proj(x, norm_g, scale1p, shift, w, q_norm[perm], k_norm[perm], cos2, sin2,
                       roles=roles, rope=True, q_scale=HEAD_DIM ** -0.5 * math.log2(math.e),
                       has_aux=False)
    o_a = _gqa_attention(proj, q_col=q_col, k_col=k_col, v_col=v_col, g_col=g_col)
    tables = _s5_tables(lam_re, lam_im, log_step, b_re, b_im, c_re, c_im, s5_d)
    y = _s5_mixer(proj, u_col, tables)
    o_b = _glu(y, w_glu.astype(BF16), b_glu, proj, s_col=gb_col)
    return _out_proj(o_a, o_b, w_out.astype(BF16), x, gate)


def _layer_na_ssd(x, c, norm_g, ada_w, ada_b, w_in, q_norm, k_norm, rpb, conv_w, conv_b,
                  dt_bias, a_log, ssd_d, norm_w, w_out):
    seq, d = x.shape
    shift, scale1p, gate = _ada_mod(c, ada_w, ada_b)
    cw = C_HEADS * HEAD_DIM
    n_in = w_in.shape[1]
    t = IN_TN
    n_pad = -(-n_in // t) * t
    w = jnp.pad(w_in, ((0, 0), (0, n_pad - n_in))).astype(BF16)
    q_col, k_col, v_col, g_col, z_col = 0, cw, 2 * cw, 3 * cw, 4 * cw
    xbc_col = z_col + SSD_WIDTH
    dt_col = xbc_col + SSD_WIDTH + 2 * SSD_BC
    roles = ((q_col // t, k_col // t, "q"), (k_col // t, v_col // t, "k"),
             (v_col // t, dt_col // t, "plain"), (dt_col // t, n_pad // t, "aux"))
    dummy = jnp.zeros((seq, HEAD_DIM), F32)
    proj, dt_raw = _in_proj(x, norm_g, scale1p, shift, w, q_norm, k_norm, dummy, dummy,
                            roles=roles, rope=False, q_scale=HEAD_DIM ** -0.5 * math.log2(math.e),
                            has_aux=True)
    bias = _na_bias_tables(rpb, seq // GRID_W)
    o_c = _na_attention(proj, bias, q_col=q_col, k_col=k_col, v_col=v_col, g_col=g_col)
    conv = _ssd_conv(proj, conv_w, conv_b, col=xbc_col)
    y_f, y_b = _ssd_scan(conv, dt_raw, dt_bias, a_log)
    d_x = jnp.repeat(ssd_d.astype(F32), SSD_HEAD_DIM).reshape(1, SSD_WIDTH)
    o_d = _gated_norm(y_f, y_b, conv, proj, d_x, norm_w, z_col=z_col)
    return _out_proj(o_c, o_d, w_out.astype(BF16), x, gate)


def kernel(x, c, e_norm_g, e_ada_w, e_ada_b, e_w_in, e_q_norm, e_k_norm, s5_lam_re, s5_lam_im,
           s5_log_step, s5_b_re, s5_b_im, s5_c_re, s5_c_im, s5_d, s5_w_glu, s5_b_glu, e_w_out,
           o_norm_g, o_ada_w, o_ada_b, o_w_in, o_q_norm, o_k_norm, na_rpb, ssd_conv_w, ssd_conv_b,
           ssd_dt_bias, ssd_a_log, ssd_d, ssd_norm_w, o_w_out):
    assert x.shape[0] == 1
    h = x[0]
    h = _layer_attn_s5(h, c, e_norm_g[0], e_ada_w[0], e_ada_b[0], e_w_in[0], e_q_norm[0],
                       e_k_norm[0], s5_lam_re[0], s5_lam_im[0], s5_log_step[0], s5_b_re[0],
                       s5_b_im[0], s5_c_re[0], s5_c_im[0], s5_d[0], s5_w_glu[0], s5_b_glu[0],
                       e_w_out[0])
    h = _layer_na_ssd(h, c, o_norm_g[0], o_ada_w[0], o_ada_b[0], o_w_in[0], o_q_norm[0],
                      o_k_norm[0], na_rpb[0], ssd_conv_w[0], ssd_conv_b[0], ssd_dt_bias[0],
                      ssd_a_log[0], ssd_d[0], ssd_norm_w[0], o_w_out[0])
    return h[None]
```
